```python
import math
import jax, jax.numpy as jnp
from jax import lax
import numpy as np

D_MODEL = 1024
BATCH = 8
SEQ = 2048
DEPTH = 4

N_MIXERS = 2
N_HEADS = 16
HEAD_DIM = D_MODEL // N_HEADS
CONV_WIDTH = 3
BRANCHES = ((128, 1), (512, 4), (2048, 16))
NUM_BUCKETS = 32
MAX_DISTANCE = 2048
D_FF = ((8 * D_MODEL // 3 + 255) // 256) * 256
EPS = 1e-6
NEG_INF = -1e30

kernel_name = "hybrid_shortconv_dilated_attn_swiglu"


def rms_norm(x, g):
    xf = x.astype(jnp.float32)
    y = xf * lax.rsqrt(jnp.mean(xf * xf, axis=-1, keepdims=True) + EPS)
    return (y * g.astype(jnp.float32)).astype(x.dtype)


def t5_bucket(dist):
    exact = NUM_BUCKETS // 2
    df = jnp.maximum(dist, 1).astype(jnp.float32)
    large = exact + (jnp.log(df / exact) / math.log(MAX_DISTANCE / exact)
                     * (NUM_BUCKETS - exact)).astype(jnp.int32)
    large = jnp.minimum(large, NUM_BUCKETS - 1)
    return jnp.where(dist < exact, dist, large)


def short_conv_mixer(h, w_in, conv_k, w_out):
    b, c, u = jnp.split(h @ w_in, 3, axis=-1)
    y = lax.conv_general_dilated(
        c * u, conv_k[:, None, :].astype(u.dtype),
        window_strides=(1,), padding=[(CONV_WIDTH - 1, 0)],
        dimension_numbers=("NWC", "WIO", "NWC"),
        feature_group_count=D_MODEL)
    return (b * y) @ w_out


def dilated_branch(q, k, v, rel_bias, window, dilation):
    bsz, seq, nh, dh = q.shape
    blk = window // dilation
    L = seq // dilation
    nb = -(-L // blk)
    Lp = nb * blk

    def sub(t):
        t = t.reshape(bsz, L, dilation, nh, dh).transpose(0, 2, 1, 3, 4)
        t = jnp.pad(t, ((0, 0), (0, 0), (0, Lp - L), (0, 0), (0, 0)))
        return t.reshape(bsz, dilation, nb, blk, nh, dh)

    def with_prev(t):
        prev = jnp.pad(t, ((0, 0), (0, 0), (1, 0), (0, 0), (0, 0), (0, 0)))[:, :, :-1]
        return jnp.concatenate([prev, t], axis=3)

    qb = sub(q)
    kk = with_prev(sub(k))
    vv = with_prev(sub(v))

    s = jnp.einsum("brnqhd,brnkhd->brnhqk", qb, kk,
                   preferred_element_type=jnp.float32) * (HEAD_DIM ** -0.5)
    qi = jnp.arange(blk)[:, None]
    ki = jnp.arange(2 * blk)[None, :]
    rel = qi + blk - ki
    band = (rel >= 0) & (rel <= blk)
    valid_start = (jnp.arange(nb)[:, None, None] > 0) | (ki >= blk)[None]
    mask = band[None] & valid_start
    bias = rel_bias[t5_bucket(jnp.clip(rel, 0) * dilation)]
    s = s + bias.transpose(2, 0, 1).astype(jnp.float32)
    s = jnp.where(mask[:, None], s, NEG_INF)
    lse = jax.nn.logsumexp(s, axis=-1)
    p = jnp.exp(s - lse[..., None])
    o = jnp.einsum("brnhqk,brnkhd->brnqhd", p.astype(v.dtype), vv,
                   preferred_element_type=jnp.float32)

    o = o.reshape(bsz, dilation, Lp, nh, dh)[:, :, :L]
    o = o.transpose(0, 2, 1, 3, 4).reshape(bsz, seq, nh, dh)
    lse = lse.transpose(0, 1, 2, 4, 3).reshape(bsz, dilation, Lp, nh)[:, :, :L]
    lse = lse.transpose(0, 2, 1, 3).reshape(bsz, seq, nh)
    return o, lse


def dilated_attention_mixer(h, w_qkv, w_out, rel_bias):
    bsz, seq, _ = h.shape
    qkv = (h @ w_qkv).reshape(bsz, seq, 3, N_HEADS, HEAD_DIM)
    q, k, v = qkv[:, :, 0], qkv[:, :, 1], qkv[:, :, 2]
    outs, lses = [], []
    for window, dilation in BRANCHES:
        o, l = dilated_branch(q, k, v, rel_bias, window, dilation)
        outs.append(o)
        lses.append(l)
    alpha = jax.nn.softmax(jnp.stack(lses, axis=0), axis=0)
    o = jnp.sum(alpha[..., None] * jnp.stack(outs, axis=0), axis=0)
    return o.reshape(bsz, seq, D_MODEL).astype(h.dtype) @ w_out


def swiglu(h, w_gate, w_up, w_down):
    return (jax.nn.silu(h @ w_gate) * (h @ w_up)) @ w_down


def setup_inputs(seed: int = 0) -> dict:
    key = jax.random.key(seed)
    ks = jax.random.split(key, 14)
    n_conv = (DEPTH + 1) // 2
    n_attn = DEPTH // 2
    f32 = jnp.float32
    nrm = lambda k, shape, scale: jax.random.normal(k, shape, f32) * scale
    return {
        "x": nrm(ks[0], (BATCH, SEQ, D_MODEL), 1.0),
        "mix_norm": 1.0 + nrm(ks[1], (DEPTH, D_MODEL), 0.05),
        "ffn_norm": 1.0 + nrm(ks[2], (DEPTH, D_MODEL), 0.05),
        "final_norm": 1.0 + nrm(ks[3], (D_MODEL,), 0.05),
        "conv_w_in": nrm(ks[4], (n_conv, D_MODEL, 3 * D_MODEL), D_MODEL ** -0.5),
        "conv_kernel": nrm(ks[5], (n_conv, CONV_WIDTH, D_MODEL), CONV_WIDTH ** -0.5),
        "conv_w_out": nrm(ks[6], (n_conv, D_MODEL, D_MODEL), D_MODEL ** -0.5),
        "attn_w_qkv": nrm(ks[7], (n_attn, D_MODEL, 3 * D_MODEL), D_MODEL ** -0.5),
        "attn_w_out": nrm(ks[8], (n_attn, D_MODEL, D_MODEL), D_MODEL ** -0.5),
        "rel_bias": nrm(ks[9], (NUM_BUCKETS, N_HEADS), 0.2),
        "ffn_w_gate": nrm(ks[10], (DEPTH, D_MODEL, D_FF), D_MODEL ** -0.5),
        "ffn_w_up": nrm(ks[11], (DEPTH, D_MODEL, D_FF), D_MODEL ** -0.5),
        "ffn_w_down": nrm(ks[12], (DEPTH, D_FF, D_MODEL), D_FF ** -0.5),
    }


def reference(x, mix_norm, ffn_norm, final_norm, conv_w_in, conv_kernel, conv_w_out,
              attn_w_qkv, attn_w_out, rel_bias, ffn_w_gate, ffn_w_up, ffn_w_down):
    for i in range(DEPTH):
        h = rms_norm(x, mix_norm[i])
        j = i // N_MIXERS
        if i % N_MIXERS == 0:
            x = x + short_conv_mixer(h, conv_w_in[j], conv_kernel[j], conv_w_out[j])
        else:
            x = x + dilated_attention_mixer(h, attn_w_qkv[j], attn_w_out[j], rel_bias)
        h = rms_norm(x, ffn_norm[i])
        x = x + swiglu(h, ffn_w_gate[i], ffn_w_up[i], ffn_w_down[i])
    return rms_norm(x, final_norm)
```

```python
import functools
import math

import jax
import jax.numpy as jnp
from jax import lax
from jax.experimental import pallas as pl
from jax.experimental.pallas import tpu as pltpu

N_HEADS = 16
HEAD_DIM = 64
CONV_WIDTH = 3
BRANCHES = ((128, 1), (512, 4), (2048, 16))
NUM_BUCKETS = 32
MAX_DISTANCE = 2048
EPS = 1e-6
NEG_INF = -1e30

LANES = 128
ROW_TILE = 512
COL_CHUNK = 512
FF_CHUNK = 256
ATT_TILE = 256
VMEM_LIMIT = 56 * 1024 * 1024

F32 = jnp.float32
BF16 = jnp.bfloat16


def _rms(x, g):
    ms = jnp.mean(x * x, axis=-1, keepdims=True)
    return x * lax.rsqrt(ms + EPS) * g


def _const_spec(shape):
    nd = len(shape)
    return pl.BlockSpec(shape, lambda *_: (0,) * nd, pipeline_mode=pl.Buffered(1))


def _params(n_axes):
    return pltpu.CompilerParams(
        dimension_semantics=("arbitrary",) * n_axes, vmem_limit_bytes=VMEM_LIMIT)


def _qkv_body(x_ref, g_ref, w_ref, o_ref):
    h = _rms(x_ref[...], g_ref[...]).astype(BF16)
    for j in range(0, w_ref.shape[1], COL_CHUNK):
        o_ref[:, j:j + COL_CHUNK] = jnp.dot(
            h, w_ref[:, j:j + COL_CHUNK], preferred_element_type=F32).astype(BF16)


def _qkv_proj(x2, g, w):
    t, d = x2.shape
    n = w.shape[1]
    return pl.pallas_call(
        _qkv_body,
        grid=(t // ROW_TILE,),
        in_specs=[pl.BlockSpec((ROW_TILE, d), lambda i: (i, 0)),
                  _const_spec((1, d)), _const_spec((d, n))],
        out_specs=pl.BlockSpec((ROW_TILE, n), lambda i: (i, 0)),
        out_shape=jax.ShapeDtypeStruct((t, n), BF16),
        compiler_params=_params(1),
        name="qkv_proj",
    )(x2, g, w)


def _conv_body(x_ref, g_ref, w_ref, k_ref, z_ref, cu_ref):
    tm, d = x_ref.shape
    first = pl.program_id(1) == 0

    @pl.when(first)
    def _():
        cu_ref[0:8, :] = jnp.zeros((8, d), F32)

    @pl.when(jnp.logical_not(first))
    def _():
        cu_ref[0:8, :] = cu_ref[tm:tm + 8, :]

    h = _rms(x_ref[...], g_ref[...]).astype(BF16)
    for j in range(0, d, COL_CHUNK):
        cols = slice(j, j + COL_CHUNK)
        c = jnp.dot(h, w_ref[:, d + j:d + j + COL_CHUNK], preferred_element_type=F32)
        u = jnp.dot(h, w_ref[:, 2 * d + j:2 * d + j + COL_CHUNK], preferred_element_type=F32)
        cu_ref[8:8 + tm, cols] = c * u
        y = (k_ref[2:3, cols] * cu_ref[8:8 + tm, cols]
             + k_ref[1:2, cols] * cu_ref[7:7 + tm, cols]
             + k_ref[0:1, cols] * cu_ref[6:6 + tm, cols])
        b = jnp.dot(h, w_ref[:, cols], preferred_element_type=F32)
        z_ref[:, cols] = (b * y).astype(BF16)


def _conv_pre(x, g, w_in, taps):
    bsz, s, d = x.shape
    tm = ROW_TILE
    return pl.pallas_call(
        _conv_body,
        grid=(bsz, s // tm),
        in_specs=[pl.BlockSpec((None, tm, d), lambda b, i: (b, i, 0)),
                  _const_spec((1, d)), _const_spec((d, 3 * d)),
                  _const_spec((CONV_WIDTH, d))],
        out_specs=pl.BlockSpec((None, tm, d), lambda b, i: (b, i, 0)),
        out_shape=jax.ShapeDtypeStruct((bsz, s, d), BF16),
        scratch_shapes=[pltpu.VMEM((tm + 8, d), F32)],
        compiler_params=_params(2),
        name="conv_pre",
    )(x, g, w_in, taps)


def _ffn_body(x_ref, z_ref, wo_ref, g_ref, wg_ref, wu_ref, wd_ref, gf_ref, o_ref, *, final):
    x1 = x_ref[...] + jnp.dot(z_ref[...], wo_ref[...], preferred_element_type=F32)
    h = _rms(x1, g_ref[...]).astype(BF16)
    o_ref[...] = x1
    for c in range(0, wg_ref.shape[1], FF_CHUNK):
        gate = jnp.dot(h, wg_ref[:, c:c + FF_CHUNK], preferred_element_type=F32)
        up = jnp.dot(h, wu_ref[:, c:c + FF_CHUNK], preferred_element_type=F32)
        a = (gate * jax.nn.sigmoid(gate) * up).astype(BF16)
        o_ref[...] += jnp.dot(a, wd_ref[c:c + FF_CHUNK, :], preferred_element_type=F32)
    if final:
        o_ref[...] = _rms(o_ref[...], gf_ref[...])


def _mix_ffn(x2, z2, wo, g, wg, wu, wd, gf, final):
    t, d = x2.shape
    dff = wg.shape[1]
    row = lambda i: (i, 0)
    return pl.pallas_call(
        functools.partial(_ffn_body, final=final),
        grid=(t // ROW_TILE,),
        in_specs=[pl.BlockSpec((ROW_TILE, d), row), pl.BlockSpec((ROW_TILE, d), row),
                  _const_spec((d, d)), _const_spec((1, d)),
                  _const_spec((d, dff)), _const_spec((d, dff)), _const_spec((dff, d)),
                  _const_spec((1, d))],
        out_specs=pl.BlockSpec((ROW_TILE, d), row),
        out_shape=jax.ShapeDtypeStruct((t, d), F32),
        compiler_params=_params(1),
        name="mix_ffn_final" if final else "mix_ffn",
    )(x2, z2, wo, g, wg, wu, wd, gf)


def _t5_bucket(dist):
    exact = NUM_BUCKETS // 2
    df = jnp.maximum(dist, 1).astype(F32)
    large = exact + (jnp.log(df / exact) / math.log(MAX_DISTANCE / exact)
                     * (NUM_BUCKETS - exact)).astype(jnp.int32)
    large = jnp.minimum(large, NUM_BUCKETS - 1)
    return jnp.where(dist < exact, dist, large)


def _bias_rows(rel_bias, seq):
    period = seq + ATT_TILE
    n = jnp.arange(period, dtype=jnp.int32)
    dist = jnp.where(n < seq, seq - ATT_TILE - n, seq - ATT_TILE + period - n)
    mult = jnp.zeros((period,), F32)
    for window, dilation in BRANCHES:
        reach = min(window, (seq // dilation - 1) * dilation)
        mult = mult + ((dist >= 0) & (dist % dilation == 0) & (dist <= reach)).astype(F32)
    table = rel_bias[_t5_bucket(jnp.clip(dist, 0))].astype(F32)
    total = jnp.where(mult[:, None] > 0, table + jnp.log(jnp.maximum(mult, 1.0))[:, None], NEG_INF)
    return total.T[:, None, :]


def _attn_body(tv_ref, q_ref, k_ref, v_ref, o_ref, bias_ref, m_ref, l_ref, acc_ref):
    seq = q_ref.shape[0]
    tq = ATT_TILE
    n_tiles = seq // tq

    @pl.when(pl.program_id(1) == 0)
    def _():
        for h in range(2):
            row = jnp.broadcast_to(tv_ref[h], (tq, tv_ref.shape[2]))
            toep = pltpu.roll(row, 0, 1, stride=1, stride_axis=0)
            for j in range(n_tiles):
                bias_ref[h, j] = toep[:, j * tq:(j + 1) * tq]

    lane = lax.broadcasted_iota(jnp.int32, (1, LANES), 1)
    head0 = lane < HEAD_DIM

    def q_tile(qb, _):
        q = q_ref[pl.ds(pl.multiple_of(qb * tq, tq), tq), :]
        zero = jnp.zeros_like(q)
        qq = jnp.concatenate([jnp.where(head0, q, zero), jnp.where(head0, zero, q)], axis=0)
        m_ref[...] = jnp.full(m_ref.shape, NEG_INF, F32)
        l_ref[...] = jnp.zeros(l_ref.shape, F32)
        acc_ref[...] = jnp.zeros(acc_ref.shape, F32)

        def kv_tile(kb, _):
            ks = pl.multiple_of(kb * tq, tq)
            k = k_ref[pl.ds(ks, tq), :]
            v = v_ref[pl.ds(ks, tq), :]
            s = lax.dot_general(qq, k, (((1,), (1,)), ((), ())), preferred_element_type=F32)
            tile = n_tiles - 1 - qb + kb
            s = s + jnp.concatenate([bias_ref[0, tile], bias_ref[1, tile]], axis=0)
            m_old = m_ref[...]
            m_new = jnp.maximum(m_old, jnp.max(s, axis=1, keepdims=True))
            alpha = jnp.exp(m_old - m_new)
            p = jnp.exp(s - m_new)
            l_ref[...] = alpha * l_ref[...] + jnp.sum(p, axis=1, keepdims=True)
            pb = p.astype(BF16)
            pv = jnp.concatenate(
                [jnp.dot(pb[:tq], v, preferred_element_type=F32),
                 jnp.dot(pb[tq:], v, preferred_element_type=F32)], axis=0)
            acc_ref[...] = alpha * acc_ref[...] + pv
            m_ref[...] = m_new
            return 0

        lax.fori_loop(0, qb + 1, kv_tile, 0)
        o = acc_ref[...] / l_ref[...]
        o_ref[pl.ds(pl.multiple_of(qb * tq, tq), tq), :] = jnp.where(
            head0, o[:tq], o[tq:]).astype(BF16)
        return 0

    lax.fori_loop(0, n_tiles, q_tile, 0)


def _attention(qkv, tv):
    bsz, s, d3 = qkv.shape
    d = d3 // 3
    n_pairs = d // LANES
    period = tv.shape[2]
    n_tiles = s // ATT_TILE
    col = lambda off: pl.BlockSpec((None, s, LANES), lambda hp, b: (b, 0, off + hp))
    return pl.pallas_call(
        _attn_body,
        grid=(n_pairs, bsz),
        in_specs=[pl.BlockSpec((2, 1, period), lambda hp, b: (hp, 0, 0)),
                  col(0), col(n_pairs), col(2 * n_pairs)],
        out_specs=pl.BlockSpec((None, s, LANES), lambda hp, b: (b, 0, hp)),
        out_shape=jax.ShapeDtypeStruct((bsz, s, d), BF16),
        scratch_shapes=[pltpu.VMEM((2, n_tiles, ATT_TILE, ATT_TILE), F32),
                        pltpu.VMEM((2 * ATT_TILE, 1), F32),
                        pltpu.VMEM((2 * ATT_TILE, 1), F32),
                        pltpu.VMEM((2 * ATT_TILE, LANES), F32)],
        compiler_params=_params(2),
        name="dilated_attention",
    )(tv, qkv, qkv, qkv)


def kernel(x, mix_norm, ffn_norm, final_norm, conv_w_in, conv_kernel, conv_w_out,
           attn_w_qkv, attn_w_out, rel_bias, ffn_w_gate, ffn_w_up, ffn_w_down):
    bsz, s, d = x.shape
    depth = mix_norm.shape[0]
    assert d == N_HEADS * HEAD_DIM and s % ROW_TILE == 0 and s % ATT_TILE == 0
    assert max(w for w, _ in BRANCHES) <= s

    q_scale = jnp.concatenate([jnp.full((d,), HEAD_DIM ** -0.5, F32), jnp.ones((2 * d,), F32)])
    w_qkv = (attn_w_qkv * q_scale).astype(BF16)
    w_in = conv_w_in.astype(BF16)
    w_co = conv_w_out.astype(BF16)
    w_ao = attn_w_out.astype(BF16)
    w_g = ffn_w_gate.astype(BF16)
    w_u = ffn_w_up.astype(BF16)
    w_d = ffn_w_down.astype(BF16)
    tv = _bias_rows(rel_bias, s)
    gf = final_norm.reshape(1, d)

    x2 = x.reshape(bsz * s, d)
    for i in range(depth):
        j = i // 2
        g_mix = mix_norm[i].reshape(1, d)
        if i % 2 == 0:
            z = _conv_pre(x2.reshape(bsz, s, d), g_mix, w_in[j], conv_kernel[j])
            w_out = w_co[j]
        else:
            qkv = _qkv_proj(x2, g_mix, w_qkv[j])
            z = _attention(qkv.reshape(bsz, s, 3 * d), tv)
            w_out = w_ao[j]
        x2 = _mix_ffn(x2, z.reshape(bsz * s, d), w_out, ffn_norm[i].reshape(1, d),
                      w_g[i], w_u[i], w_d[i], gf, final=(i == depth - 1))
    return x2.reshape(bsz, s, d)
```

```python
import functools
import math

import jax
import jax.numpy as jnp
from jax import lax
from jax.experimental import pallas as pl
from jax.experimental.pallas import tpu as pltpu

N_HEADS = 16
HEAD_DIM = 64
CONV_WIDTH = 3
BRANCHES = ((128, 1), (512, 4), (2048, 16))
NUM_BUCKETS = 32
MAX_DISTANCE = 2048
EPS = 1e-6
NEG_INF = -1e30
LOG2E = math.log2(math.e)

LANES = 128
ROW_TILE = 512
COL_CHUNK = 512
FF_CHUNK = 256
ATT_TILE = 256
ONES_ROWS = 16
VMEM_LIMIT = 56 * 1024 * 1024

F32 = jnp.float32
BF16 = jnp.bfloat16
NT_DIMS = (((1,), (1,)), ((), ()))


def _rms(x, g):
    ms = jnp.mean(x * x, axis=-1, keepdims=True)
    return x * lax.rsqrt(ms + EPS) * g


def _const_spec(shape):
    nd = len(shape)
    return pl.BlockSpec(shape, lambda *_: (0,) * nd, pipeline_mode=pl.Buffered(1))


def _params(n_axes):
    return pltpu.CompilerParams(
        dimension_semantics=("arbitrary",) * n_axes, vmem_limit_bytes=VMEM_LIMIT)


def _qkv_body(x_ref, g_ref, wqt_ref, wk_ref, wvt_ref, qt_ref, k_ref, vt_ref):
    h = _rms(x_ref[...], g_ref[...]).astype(BF16)
    for j in range(0, wk_ref.shape[1], COL_CHUNK):
        k_ref[:, j:j + COL_CHUNK] = jnp.dot(
            h, wk_ref[:, j:j + COL_CHUNK], preferred_element_type=F32).astype(BF16)
    q_scale = LOG2E * HEAD_DIM ** -0.5
    for c in range(0, x_ref.shape[0], ATT_TILE):
        hc = h[c:c + ATT_TILE]
        qt = lax.dot_general(wqt_ref[...], hc, NT_DIMS, preferred_element_type=F32)
        qt_ref[:, c:c + ATT_TILE] = (qt * q_scale).astype(BF16)
        vt_ref[:, c:c + ATT_TILE] = lax.dot_general(
            wvt_ref[...], hc, NT_DIMS, preferred_element_type=F32).astype(BF16)


def _qkv_proj(x, g, wqt, wk, wvt):
    bsz, s, d = x.shape
    tm = ROW_TILE
    t_spec = pl.BlockSpec((None, d, tm), lambda b, i: (b, 0, i))
    t_shape = jax.ShapeDtypeStruct((bsz, d, s), BF16)
    return pl.pallas_call(
        _qkv_body,
        grid=(bsz, s // tm),
        in_specs=[pl.BlockSpec((None, tm, d), lambda b, i: (b, i, 0)),
                  _const_spec((1, d)), _const_spec((d, d)), _const_spec((d, d)),
                  _const_spec((d, d))],
        out_specs=[t_spec, pl.BlockSpec((None, tm, d), lambda b, i: (b, i, 0)), t_spec],
        out_shape=[t_shape, jax.ShapeDtypeStruct((bsz, s, d), BF16), t_shape],
        compiler_params=_params(2),
        name="qkv_proj",
    )(x, g, wqt, wk, wvt)


def _conv_body(x_ref, g_ref, w_ref, k_ref, z_ref, cu_ref):
    tm, d = x_ref.shape
    first = pl.program_id(1) == 0

    @pl.when(first)
    def _():
        cu_ref[0:8, :] = jnp.zeros((8, d), F32)

    @pl.when(jnp.logical_not(first))
    def _():
        cu_ref[0:8, :] = cu_ref[tm:tm + 8, :]

    h = _rms(x_ref[...], g_ref[...]).astype(BF16)
    for j in range(0, d, COL_CHUNK):
        cols = slice(j, j + COL_CHUNK)
        c = jnp.dot(h, w_ref[:, d + j:d + j + COL_CHUNK], preferred_element_type=F32)
        u = jnp.dot(h, w_ref[:, 2 * d + j:2 * d + j + COL_CHUNK], preferred_element_type=F32)
        cu_ref[8:8 + tm, cols] = c * u
        y = (k_ref[2:3, cols] * cu_ref[8:8 + tm, cols]
             + k_ref[1:2, cols] * cu_ref[7:7 + tm, cols]
             + k_ref[0:1, cols] * cu_ref[6:6 + tm, cols])
        b = jnp.dot(h, w_ref[:, cols], preferred_element_type=F32)
        z_ref[:, cols] = (b * y).astype(BF16)


def _conv_pre(x, g, w_in, taps):
    bsz, s, d = x.shape
    tm = ROW_TILE
    return pl.pallas_call(
        _conv_body,
        grid=(bsz, s // tm),
        in_specs=[pl.BlockSpec((None, tm, d), lambda b, i: (b, i, 0)),
                  _const_spec((1, d)), _const_spec((d, 3 * d)),
                  _const_spec((CONV_WIDTH, d))],
        out_specs=pl.BlockSpec((None, tm, d), lambda b, i: (b, i, 0)),
        out_shape=jax.ShapeDtypeStruct((bsz, s, d), BF16),
        scratch_shapes=[pltpu.VMEM((tm + 8, d), F32)],
        compiler_params=_params(2),
        name="conv_pre",
    )(x, g, w_in, taps)


def _ffn_body(x_ref, z_ref, wo_ref, g_ref, wg_ref, wu_ref, wd_ref, gf_ref, o_ref, *, final):
    x1 = x_ref[...] + jnp.dot(z_ref[...], wo_ref[...], preferred_element_type=F32)
    h = _rms(x1, g_ref[...]).astype(BF16)
    o_ref[...] = x1
    for c in range(0, wg_ref.shape[1], FF_CHUNK):
        gate = jnp.dot(h, wg_ref[:, c:c + FF_CHUNK], preferred_element_type=F32)
        up = jnp.dot(h, wu_ref[:, c:c + FF_CHUNK], preferred_element_type=F32)
        a = (gate * jax.nn.sigmoid(gate) * up).astype(BF16)
        o_ref[...] += jnp.dot(a, wd_ref[c:c + FF_CHUNK, :], preferred_element_type=F32)
    if final:
        o_ref[...] = _rms(o_ref[...], gf_ref[...])


def _mix_ffn(x2, z2, wo, g, wg, wu, wd, gf, final):
    t, d = x2.shape
    dff = wg.shape[1]
    row = lambda i: (i, 0)
    return pl.pallas_call(
        functools.partial(_ffn_body, final=final),
        grid=(t // ROW_TILE,),
        in_specs=[pl.BlockSpec((ROW_TILE, d), row), pl.BlockSpec((ROW_TILE, d), row),
                  _const_spec((d, d)), _const_spec((1, d)),
                  _const_spec((d, dff)), _const_spec((d, dff)), _const_spec((dff, d)),
                  _const_spec((1, d))],
        out_specs=pl.BlockSpec((ROW_TILE, d), row),
        out_shape=jax.ShapeDtypeStruct((t, d), F32),
        compiler_params=_params(1),
        name="mix_ffn_final" if final else "mix_ffn",
    )(x2, z2, wo, g, wg, wu, wd, gf)


def _t5_bucket(dist):
    exact = NUM_BUCKETS // 2
    df = jnp.maximum(dist, 1).astype(F32)
    large = exact + (jnp.log(df / exact) / math.log(MAX_DISTANCE / exact)
                     * (NUM_BUCKETS - exact)).astype(jnp.int32)
    large = jnp.minimum(large, NUM_BUCKETS - 1)
    return jnp.where(dist < exact, dist, large)


def _bias_rows(rel_bias, seq):
    dist = jnp.arange(seq, dtype=jnp.int32)
    mult = jnp.zeros((seq,), F32)
    for window, dilation in BRANCHES:
        mult = mult + ((dist % dilation == 0) & (dist <= window)).astype(F32)
    table = rel_bias[_t5_bucket(dist)].astype(F32)
    total = jnp.where(mult[:, None] > 0,
                      (table + jnp.log(jnp.maximum(mult, 1.0))[:, None]) * LOG2E, NEG_INF)
    total = jnp.concatenate([total, jnp.full((ATT_TILE, total.shape[1]), NEG_INF, F32)], axis=0)
    return total.T[:, None, :]


def _attn_body(tv_ref, qt_ref, k_ref, vt_ref, o_ref, bias_ref, lhs_ref):
    tq = ATT_TILE
    seq = k_ref.shape[0]
    n_tiles = seq // tq

    @pl.when(pl.program_id(1) == 0)
    def _():
        for h in range(2):
            row = jnp.broadcast_to(tv_ref[h], (tq, tv_ref.shape[2]))
            toep = pltpu.roll(row, 0, 1, stride=1, stride_axis=0)
            for dist in range(n_tiles):
                start = (n_tiles - 1 - dist) * tq
                bias_ref[h, start:start + tq, :] = toep[:, dist * tq:(dist + 1) * tq]

    for h in range(2):
        lhs_ref[h, :HEAD_DIM, :] = vt_ref[h * HEAD_DIM:(h + 1) * HEAD_DIM, :]
        lhs_ref[h, HEAD_DIM:, :] = jnp.ones((ONES_ROWS, seq), BF16)

    zeros = jnp.zeros((HEAD_DIM, tq), BF16)
    for qb in range(n_tiles):
        kv_len = (qb + 1) * tq
        qt = qt_ref[:, qb * tq:(qb + 1) * tq]
        q_heads = (jnp.concatenate([qt[:HEAD_DIM], zeros], axis=0),
                   jnp.concatenate([zeros, qt[HEAD_DIM:]], axis=0))
        outs = []
        for h in range(2):
            s = jnp.dot(k_ref[0:kv_len, :], q_heads[h], preferred_element_type=F32)
            s = s + bias_ref[h, (n_tiles - 1 - qb) * tq:, :]
            m = jnp.max(s, axis=0, keepdims=True)
            p = jnp.exp2(s - m).astype(BF16)
            acc = jnp.dot(lhs_ref[h, :, 0:kv_len], p, preferred_element_type=F32)
            outs.append(acc[:HEAD_DIM] / acc[HEAD_DIM:HEAD_DIM + 1])
        ot = jnp.concatenate(outs, axis=0)
        o_ref[qb * tq:(qb + 1) * tq, :] = ot.T.astype(BF16)


def _attention(qt, k, vt, tv):
    bsz, s, d = k.shape
    n_pairs = d // LANES
    t_spec = pl.BlockSpec((None, LANES, s), lambda hp, b: (b, hp, 0))
    r_spec = pl.BlockSpec((None, s, LANES), lambda hp, b: (b, 0, hp))
    return pl.pallas_call(
        _attn_body,
        grid=(n_pairs, bsz),
        in_specs=[pl.BlockSpec((2, 1, tv.shape[2]), lambda hp, b: (hp, 0, 0)),
                  t_spec, r_spec, t_spec],
        out_specs=r_spec,
        out_shape=jax.ShapeDtypeStruct((bsz, s, d), BF16),
        scratch_shapes=[pltpu.VMEM((2, s, ATT_TILE), F32),
                        pltpu.VMEM((2, HEAD_DIM + ONES_ROWS, s), BF16)],
        compiler_params=_params(2),
        name="dilated_attention",
    )(tv, qt, k, vt)


def kernel(x, mix_norm, ffn_norm, final_norm, conv_w_in, conv_kernel, conv_w_out,
           attn_w_qkv, attn_w_out, rel_bias, ffn_w_gate, ffn_w_up, ffn_w_down):
    bsz, s, d = x.shape
    depth = mix_norm.shape[0]
    assert d == N_HEADS * HEAD_DIM and s % ROW_TILE == 0 and ROW_TILE % ATT_TILE == 0
    assert max(w for w, _ in BRANCHES) <= s

    w_qt = jnp.swapaxes(attn_w_qkv[:, :, :d], 1, 2).astype(BF16)
    w_k = attn_w_qkv[:, :, d:2 * d].astype(BF16)
    w_vt = jnp.swapaxes(attn_w_qkv[:, :, 2 * d:], 1, 2).astype(BF16)
    w_in = conv_w_in.astype(BF16)
    w_co = conv_w_out.astype(BF16)
    w_ao = attn_w_out.astype(BF16)
    w_g = ffn_w_gate.astype(BF16)
    w_u = ffn_w_up.astype(BF16)
    w_d = ffn_w_down.astype(BF16)
    tv = _bias_rows(rel_bias, s)
    gf = final_norm.reshape(1, d)

    x2 = x.reshape(bsz * s, d)
    for i in range(depth):
        j = i // 2
        g_mix = mix_norm[i].reshape(1, d)
        x3 = x2.reshape(bsz, s, d)
        if i % 2 == 0:
            z = _conv_pre(x3, g_mix, w_in[j], conv_kernel[j])
            w_out = w_co[j]
        else:
            qt, k, vt = _qkv_proj(x3, g_mix, w_qt[j], w_k[j], w_vt[j])
            z = _attention(qt, k, vt, tv)
            w_out = w_ao[j]
        x2 = _mix_ffn(x2, z.reshape(bsz * s, d), w_out, ffn_norm[i].reshape(1, d),
                      w_g[i], w_u[i], w_d[i], gf, final=(i == depth - 1))
    return x2.reshape(bsz, s, d)
```

```python
import functools
import math

import jax
import jax.numpy as jnp
from jax import lax
from jax.experimental import pallas as pl
from jax.experimental.pallas import tpu as pltpu

N_HEADS = 16
HEAD_DIM = 64
CONV_WIDTH = 3
BRANCHES = ((128, 1), (512, 4), (2048, 16))
NUM_BUCKETS = 32
MAX_DISTANCE = 2048
EPS = 1e-6
NEG_INF = -1e30
LOG2E = math.log2(math.e)

LANES = 128
ROW_TILE = 512
COL_CHUNK = 512
FF_CHUNK = 256
ATT_TILE = 256
ONES_ROWS = 16
VMEM_LIMIT = 56 * 1024 * 1024

F32 = jnp.float32
BF16 = jnp.bfloat16
NT_DIMS = (((1,), (1,)), ((), ()))


def _rms(x, g):
    ms = jnp.mean(x * x, axis=-1, keepdims=True)
    return x * lax.rsqrt(ms + EPS) * g


def _const_spec(shape):
    nd = len(shape)
    return pl.BlockSpec(shape, lambda *_: (0,) * nd, pipeline_mode=pl.Buffered(1))


def _params(n_axes):
    return pltpu.CompilerParams(
        dimension_semantics=("arbitrary",) * n_axes, vmem_limit_bytes=VMEM_LIMIT)


def _qkv_body(x_ref, g_ref, wqt_ref, wk_ref, wvt_ref, qt_ref, k_ref, vt_ref):
    h = _rms(x_ref[...], g_ref[...]).astype(BF16)
    for j in range(0, wk_ref.shape[1], COL_CHUNK):
        k_ref[:, j:j + COL_CHUNK] = jnp.dot(
            h, wk_ref[:, j:j + COL_CHUNK], preferred_element_type=F32).astype(BF16)
    q_scale = LOG2E * HEAD_DIM ** -0.5
    for c in range(0, x_ref.shape[0], ATT_TILE):
        hc = h[c:c + ATT_TILE]
        qt = lax.dot_general(wqt_ref[...], hc, NT_DIMS, preferred_element_type=F32)
        qt_ref[:, c:c + ATT_TILE] = (qt * q_scale).astype(BF16)
        vt_ref[:, c:c + ATT_TILE] = lax.dot_general(
            wvt_ref[...], hc, NT_DIMS, preferred_element_type=F32).astype(BF16)


def _qkv_proj(x, g, wqt, wk, wvt):
    bsz, s, d = x.shape
    tm = ROW_TILE
    t_spec = pl.BlockSpec((None, d, tm), lambda b, i: (b, 0, i))
    t_shape = jax.ShapeDtypeStruct((bsz, d, s), BF16)
    return pl.pallas_call(
        _qkv_body,
        grid=(bsz, s // tm),
        in_specs=[pl.BlockSpec((None, tm, d), lambda b, i: (b, i, 0)),
                  _const_spec((1, d)), _const_spec((d, d)), _const_spec((d, d)),
                  _const_spec((d, d))],
        out_specs=[t_spec, pl.BlockSpec((None, tm, d), lambda b, i: (b, i, 0)), t_spec],
        out_shape=[t_shape, jax.ShapeDtypeStruct((bsz, s, d), BF16), t_shape],
        compiler_params=_params(2),
        name="qkv_proj",
    )(x, g, wqt, wk, wvt)


def _conv_body(x_ref, g_ref, w_ref, k_ref, z_ref, cu_ref):
    tm, d = x_ref.shape
    first = pl.program_id(1) == 0

    @pl.when(first)
    def _():
        cu_ref[0:8, :] = jnp.zeros((8, d), F32)

    @pl.when(jnp.logical_not(first))
    def _():
        cu_ref[0:8, :] = cu_ref[tm:tm + 8, :]

    h = _rms(x_ref[...], g_ref[...]).astype(BF16)
    for j in range(0, d, COL_CHUNK):
        cols = slice(j, j + COL_CHUNK)
        c = jnp.dot(h, w_ref[:, d + j:d + j + COL_CHUNK], preferred_element_type=F32)
        u = jnp.dot(h, w_ref[:, 2 * d + j:2 * d + j + COL_CHUNK], preferred_element_type=F32)
        cu_ref[8:8 + tm, cols] = c * u
        y = (k_ref[2:3, cols] * cu_ref[8:8 + tm, cols]
             + k_ref[1:2, cols] * cu_ref[7:7 + tm, cols]
             + k_ref[0:1, cols] * cu_ref[6:6 + tm, cols])
        b = jnp.dot(h, w_ref[:, cols], preferred_element_type=F32)
        z_ref[:, cols] = (b * y).astype(BF16)


def _conv_pre(x, g, w_in, taps):
    bsz, s, d = x.shape
    tm = ROW_TILE
    return pl.pallas_call(
        _conv_body,
        grid=(bsz, s // tm),
        in_specs=[pl.BlockSpec((None, tm, d), lambda b, i: (b, i, 0)),
                  _const_spec((1, d)), _const_spec((d, 3 * d)),
                  _const_spec((CONV_WIDTH, d))],
        out_specs=pl.BlockSpec((None, tm, d), lambda b, i: (b, i, 0)),
        out_shape=jax.ShapeDtypeStruct((bsz, s, d), BF16),
        scratch_shapes=[pltpu.VMEM((tm + 8, d), F32)],
        compiler_params=_params(2),
        name="conv_pre",
    )(x, g, w_in, taps)


def _ffn_body(x_ref, z_ref, wo_ref, g_ref, wg_ref, wu_ref, wd_ref, gf_ref, o_ref, *, final):
    x1 = x_ref[...] + jnp.dot(z_ref[...], wo_ref[...], preferred_element_type=F32)
    h = _rms(x1, g_ref[...]).astype(BF16)
    o_ref[...] = x1
    for c in range(0, wg_ref.shape[1], FF_CHUNK):
        gate = jnp.dot(h, wg_ref[:, c:c + FF_CHUNK], preferred_element_type=F32)
        up = jnp.dot(h, wu_ref[:, c:c + FF_CHUNK], preferred_element_type=F32)
        a = (gate * jax.nn.sigmoid(gate) * up).astype(BF16)
        o_ref[...] += jnp.dot(a, wd_ref[c:c + FF_CHUNK, :], preferred_element_type=F32)
    if final:
        o_ref[...] = _rms(o_ref[...], gf_ref[...])


def _mix_ffn(x2, z2, wo, g, wg, wu, wd, gf, final):
    t, d = x2.shape
    dff = wg.shape[1]
    row = lambda i: (i, 0)
    return pl.pallas_call(
        functools.partial(_ffn_body, final=final),
        grid=(t // ROW_TILE,),
        in_specs=[pl.BlockSpec((ROW_TILE, d), row), pl.BlockSpec((ROW_TILE, d), row),
                  _const_spec((d, d)), _const_spec((1, d)),
                  _const_spec((d, dff)), _const_spec((d, dff)), _const_spec((dff, d)),
                  _const_spec((1, d))],
        out_specs=pl.BlockSpec((ROW_TILE, d), row),
        out_shape=jax.ShapeDtypeStruct((t, d), F32),
        compiler_params=_params(1),
        name="mix_ffn_final" if final else "mix_ffn",
    )(x2, z2, wo, g, wg, wu, wd, gf)


def _t5_bucket(dist):
    exact = NUM_BUCKETS // 2
    df = jnp.maximum(dist, 1).astype(F32)
    large = exact + (jnp.log(df / exact) / math.log(MAX_DISTANCE / exact)
                     * (NUM_BUCKETS - exact)).astype(jnp.int32)
    large = jnp.minimum(large, NUM_BUCKETS - 1)
    return jnp.where(dist < exact, dist, large)


def _bias_rows(rel_bias, seq):
    dist = jnp.arange(seq, dtype=jnp.int32)
    mult = jnp.zeros((seq,), F32)
    for window, dilation in BRANCHES:
        mult = mult + ((dist % dilation == 0) & (dist <= window)).astype(F32)
    table = rel_bias[_t5_bucket(dist)].astype(F32)
    total = jnp.where(mult[:, None] > 0,
                      (table + jnp.log(jnp.maximum(mult, 1.0))[:, None]) * LOG2E, NEG_INF)
    total = jnp.concatenate([total, jnp.full((ATT_TILE, total.shape[1]), NEG_INF, F32)], axis=0)
    return total.T[:, None, :]


def _attn_body(tv_ref, qt_ref, k_ref, vt_ref, o_ref, bias_ref, s0_ref, s1_ref, p0_ref, p1_ref):
    tq = ATT_TILE
    seq = k_ref.shape[0]
    n_tiles = seq // tq
    s_bufs, p_bufs = (s0_ref, s1_ref), (p0_ref, p1_ref)

    @pl.when(pl.program_id(1) == 0)
    def _():
        for h in range(2):
            row = jnp.broadcast_to(tv_ref[h], (tq, tv_ref.shape[2]))
            toep = pltpu.roll(row, 0, 1, stride=1, stride_axis=0)
            for dist in range(n_tiles):
                bias_ref[dist, :, h * tq:(h + 1) * tq] = toep[:, dist * tq:(dist + 1) * tq]

    zeros = jnp.zeros((HEAD_DIM, tq), BF16)
    ones = jnp.ones((ONES_ROWS, tq), BF16)
    items = [(qb, kb) for qb in range(n_tiles) for kb in range(qb + 1)]
    col_max, state = {}, {}

    def stage_a(t):
        qb, kb = items[t]
        qt = qt_ref[:, qb * tq:(qb + 1) * tq]
        q_both = jnp.concatenate([jnp.concatenate([qt[:HEAD_DIM], zeros], axis=0),
                                  jnp.concatenate([zeros, qt[HEAD_DIM:]], axis=0)], axis=1)
        s = jnp.dot(k_ref[kb * tq:(kb + 1) * tq, :], q_both, preferred_element_type=F32)
        s = s + bias_ref[qb - kb]
        s_bufs[t % 2][...] = s
        col_max[t] = jnp.max(s, axis=0, keepdims=True)

    def stage_b(t):
        _, kb = items[t]
        m_new = col_max.pop(t)
        alpha = None
        if kb > 0:
            m_old = state["m"]
            m_new = jnp.maximum(m_old, m_new)
            alpha = jnp.exp2(m_old - m_new)
        state["m"] = m_new
        state["alpha", t] = alpha
        p_bufs[t % 2][...] = jnp.exp2(s_bufs[t % 2][...] - m_new).astype(BF16)

    def stage_c(t):
        qb, kb = items[t]
        lhs = jnp.concatenate([vt_ref[:, kb * tq:(kb + 1) * tq], ones], axis=0)
        acc = jnp.dot(lhs, p_bufs[t % 2][...], preferred_element_type=F32)
        alpha = state.pop(("alpha", t))
        if kb > 0:
            acc = alpha * state["acc"] + acc
        state["acc"] = acc
        if kb == qb:
            d2 = 2 * HEAD_DIM
            ot = jnp.concatenate([acc[:HEAD_DIM, :tq] / acc[d2:d2 + 1, :tq],
                                  acc[HEAD_DIM:d2, tq:] / acc[d2:d2 + 1, tq:]], axis=0)
            o_ref[qb * tq:(qb + 1) * tq, :] = ot.T.astype(BF16)

    for t in range(len(items) + 2):
        if t < len(items):
            stage_a(t)
        if 1 <= t <= len(items):
            stage_b(t - 1)
        if t >= 2:
            stage_c(t - 2)


def _attention(qt, k, vt, tv):
    bsz, s, d = k.shape
    n_pairs = d // LANES
    t_spec = pl.BlockSpec((None, LANES, s), lambda hp, b: (b, hp, 0))
    r_spec = pl.BlockSpec((None, s, LANES), lambda hp, b: (b, 0, hp))
    return pl.pallas_call(
        _attn_body,
        grid=(n_pairs, bsz),
        in_specs=[pl.BlockSpec((2, 1, tv.shape[2]), lambda hp, b: (hp, 0, 0)),
                  t_spec, r_spec, t_spec],
        out_specs=r_spec,
        out_shape=jax.ShapeDtypeStruct((bsz, s, d), BF16),
        scratch_shapes=[pltpu.VMEM((s // ATT_TILE, ATT_TILE, 2 * ATT_TILE), F32),
                        pltpu.VMEM((ATT_TILE, 2 * ATT_TILE), F32),
                        pltpu.VMEM((ATT_TILE, 2 * ATT_TILE), F32),
                        pltpu.VMEM((ATT_TILE, 2 * ATT_TILE), BF16),
                        pltpu.VMEM((ATT_TILE, 2 * ATT_TILE), BF16)],
        compiler_params=_params(2),
        name="dilated_attention",
    )(tv, qt, k, vt)


def kernel(x, mix_norm, ffn_norm, final_norm, conv_w_in, conv_kernel, conv_w_out,
           attn_w_qkv, attn_w_out, rel_bias, ffn_w_gate, ffn_w_up, ffn_w_down):
    bsz, s, d = x.shape
    depth = mix_norm.shape[0]
    assert d == N_HEADS * HEAD_DIM and s % ROW_TILE == 0 and ROW_TILE % ATT_TILE == 0
    assert max(w for w, _ in BRANCHES) <= s

    w_qt = jnp.swapaxes(attn_w_qkv[:, :, :d], 1, 2).astype(BF16)
    w_k = attn_w_qkv[:, :, d:2 * d].astype(BF16)
    w_vt = jnp.swapaxes(attn_w_qkv[:, :, 2 * d:], 1, 2).astype(BF16)
    w_in = conv_w_in.astype(BF16)
    w_co = conv_w_out.astype(BF16)
    w_ao = attn_w_out.astype(BF16)
    w_g = ffn_w_gate.astype(BF16)
    w_u = ffn_w_up.astype(BF16)
    w_d = ffn_w_down.astype(BF16)
    tv = _bias_rows(rel_bias, s)
    gf = final_norm.reshape(1, d)

    x2 = x.reshape(bsz * s, d)
    for i in range(depth):
        j = i // 2
        g_mix = mix_norm[i].reshape(1, d)
        x3 = x2.reshape(bsz, s, d)
        if i % 2 == 0:
            z = _conv_pre(x3, g_mix, w_in[j], conv_kernel[j])
            w_out = w_co[j]
        else:
            qt, k, vt = _qkv_proj(x3, g_mix, w_qt[j], w_k[j], w_vt[j])
            z = _attention(qt, k, vt, tv)
            w_out = w_ao[j]
        x2 = _mix_ffn(x2, z.reshape(bsz * s, d), w_out, ffn_norm[i].reshape(1, d),
                      w_g[i], w_u[i], w_d[i], gf, final=(i == depth - 1))
    return x2.reshape(bsz, s, d)
```

```python
import functools
import math

import jax
import jax.numpy as jnp
from jax import lax
from jax.experimental import pallas as pl
from jax.experimental.pallas import tpu as pltpu

N_HEADS = 16
HEAD_DIM = 64
CONV_WIDTH = 3
BRANCHES = ((128, 1), (512, 4), (2048, 16))
NUM_BUCKETS = 32
MAX_DISTANCE = 2048
EPS = 1e-6
NEG_INF = -1e30
LOG2E = math.log2(math.e)

LANES = 128
ROW_TILE = 512
COL_CHUNK = 512
FF_CHUNK = 256
ATT_TILE = 256
ONES_ROWS = 16
VMEM_LIMIT = 56 * 1024 * 1024

F32 = jnp.float32
BF16 = jnp.bfloat16
NT_DIMS = (((1,), (1,)), ((), ()))


def _rms(x, g):
    ms = jnp.mean(x * x, axis=-1, keepdims=True)
    return x * lax.rsqrt(ms + EPS) * g


def _const_spec(shape):
    nd = len(shape)
    return pl.BlockSpec(shape, lambda *_: (0,) * nd, pipeline_mode=pl.Buffered(1))


def _layer_spec(stacked, layer):
    tail = (0,) * (stacked.ndim - 1)
    return pl.BlockSpec((None,) + stacked.shape[1:], lambda *_: (layer,) + tail,
                        pipeline_mode=pl.Buffered(1))


def _params(n_axes):
    return pltpu.CompilerParams(
        dimension_semantics=("arbitrary",) * n_axes, vmem_limit_bytes=VMEM_LIMIT)


def _qkv_body(x_ref, g_ref, wqt_ref, wk_ref, wvt_ref, qt_ref, k_ref, vt_ref):
    h = _rms(x_ref[...], g_ref[...]).astype(BF16)
    for j in range(0, wk_ref.shape[1], COL_CHUNK):
        k_ref[:, j:j + COL_CHUNK] = jnp.dot(
            h, wk_ref[:, j:j + COL_CHUNK], preferred_element_type=F32).astype(BF16)
    q_scale = LOG2E * HEAD_DIM ** -0.5
    for c in range(0, x_ref.shape[0], ATT_TILE):
        hc = h[c:c + ATT_TILE]
        qt = lax.dot_general(wqt_ref[...], hc, NT_DIMS, preferred_element_type=F32)
        qt_ref[:, c:c + ATT_TILE] = (qt * q_scale).astype(BF16)
        vt_ref[:, c:c + ATT_TILE] = lax.dot_general(
            wvt_ref[...], hc, NT_DIMS, preferred_element_type=F32).astype(BF16)


def _qkv_proj(x, g, wqt, wk, wvt, layer):
    bsz, s, d = x.shape
    tm = ROW_TILE
    t_spec = pl.BlockSpec((None, d, tm), lambda b, i: (b, 0, i))
    t_shape = jax.ShapeDtypeStruct((bsz, d, s), BF16)
    return pl.pallas_call(
        _qkv_body,
        grid=(bsz, s // tm),
        in_specs=[pl.BlockSpec((None, tm, d), lambda b, i: (b, i, 0)),
                  _const_spec((1, d)), _layer_spec(wqt, layer), _layer_spec(wk, layer),
                  _layer_spec(wvt, layer)],
        out_specs=[t_spec, pl.BlockSpec((None, tm, d), lambda b, i: (b, i, 0)), t_spec],
        out_shape=[t_shape, jax.ShapeDtypeStruct((bsz, s, d), BF16), t_shape],
        compiler_params=_params(2),
        name="qkv_proj",
    )(x, g, wqt, wk, wvt)


def _conv_body(x_ref, g_ref, w_ref, k_ref, z_ref, cu_ref):
    tm, d = x_ref.shape
    first = pl.program_id(1) == 0

    @pl.when(first)
    def _():
        cu_ref[0:8, :] = jnp.zeros((8, d), F32)

    @pl.when(jnp.logical_not(first))
    def _():
        cu_ref[0:8, :] = cu_ref[tm:tm + 8, :]

    h = _rms(x_ref[...], g_ref[...]).astype(BF16)
    for j in range(0, d, COL_CHUNK):
        cols = slice(j, j + COL_CHUNK)
        c = jnp.dot(h, w_ref[:, d + j:d + j + COL_CHUNK], preferred_element_type=F32)
        u = jnp.dot(h, w_ref[:, 2 * d + j:2 * d + j + COL_CHUNK], preferred_element_type=F32)
        cu_ref[8:8 + tm, cols] = c * u
        y = (k_ref[2:3, cols] * cu_ref[8:8 + tm, cols]
             + k_ref[1:2, cols] * cu_ref[7:7 + tm, cols]
             + k_ref[0:1, cols] * cu_ref[6:6 + tm, cols])
        b = jnp.dot(h, w_ref[:, cols], preferred_element_type=F32)
        z_ref[:, cols] = (b * y).astype(BF16)


def _conv_pre(x, g, w_in, taps, layer):
    bsz, s, d = x.shape
    tm = ROW_TILE
    return pl.pallas_call(
        _conv_body,
        grid=(bsz, s // tm),
        in_specs=[pl.BlockSpec((None, tm, d), lambda b, i: (b, i, 0)),
                  _const_spec((1, d)), _layer_spec(w_in, layer), _layer_spec(taps, layer)],
        out_specs=pl.BlockSpec((None, tm, d), lambda b, i: (b, i, 0)),
        out_shape=jax.ShapeDtypeStruct((bsz, s, d), BF16),
        scratch_shapes=[pltpu.VMEM((tm + 8, d), F32)],
        compiler_params=_params(2),
        name="conv_pre",
    )(x, g, w_in, taps)


def _ffn_body(x_ref, z_ref, wo_ref, g_ref, wg_ref, wu_ref, wd_ref, gf_ref, o_ref, *, final):
    x1 = x_ref[...] + jnp.dot(z_ref[...], wo_ref[...], preferred_element_type=F32)
    h = _rms(x1, g_ref[...]).astype(BF16)
    o_ref[...] = x1
    for c in range(0, wg_ref.shape[1], FF_CHUNK):
        gate = jnp.dot(h, wg_ref[:, c:c + FF_CHUNK], preferred_element_type=F32)
        up = jnp.dot(h, wu_ref[:, c:c + FF_CHUNK], preferred_element_type=F32)
        a = (gate * jax.nn.sigmoid(gate) * up).astype(BF16)
        o_ref[...] += jnp.dot(a, wd_ref[c:c + FF_CHUNK, :], preferred_element_type=F32)
    if final:
        o_ref[...] = _rms(o_ref[...], gf_ref[...])


def _mix_ffn(x2, z2, wo, mix_layer, g, wg, wu, wd, layer, gf, final):
    t, d = x2.shape
    row = lambda i: (i, 0)
    return pl.pallas_call(
        functools.partial(_ffn_body, final=final),
        grid=(t // ROW_TILE,),
        in_specs=[pl.BlockSpec((ROW_TILE, d), row), pl.BlockSpec((ROW_TILE, d), row),
                  _layer_spec(wo, mix_layer), _const_spec((1, d)),
                  _layer_spec(wg, layer), _layer_spec(wu, layer), _layer_spec(wd, layer),
                  _const_spec((1, d))],
        out_specs=pl.BlockSpec((ROW_TILE, d), row),
        out_shape=jax.ShapeDtypeStruct((t, d), F32),
        compiler_params=_params(1),
        name="mix_ffn_final" if final else "mix_ffn",
    )(x2, z2, wo, g, wg, wu, wd, gf)


def _t5_bucket(dist):
    exact = NUM_BUCKETS // 2
    df = jnp.maximum(dist, 1).astype(F32)
    large = exact + (jnp.log(df / exact) / math.log(MAX_DISTANCE / exact)
                     * (NUM_BUCKETS - exact)).astype(jnp.int32)
    large = jnp.minimum(large, NUM_BUCKETS - 1)
    return jnp.where(dist < exact, dist, large)


def _bias_rows(rel_bias, seq):
    dist = jnp.arange(seq, dtype=jnp.int32)
    mult = jnp.zeros((seq,), F32)
    for window, dilation in BRANCHES:
        mult = mult + ((dist % dilation == 0) & (dist <= window)).astype(F32)
    table = rel_bias[_t5_bucket(dist)].astype(F32)
    total = jnp.where(mult[:, None] > 0,
                      (table + jnp.log(jnp.maximum(mult, 1.0))[:, None]) * LOG2E, NEG_INF)
    total = jnp.concatenate([total, jnp.full((ATT_TILE, total.shape[1]), NEG_INF, F32)], axis=0)
    return total.T[:, None, :]


def _attn_body(tv_ref, qt_ref, k_ref, vt_ref, o_ref, bias_ref, s0_ref, s1_ref, p0_ref, p1_ref):
    tq = ATT_TILE
    seq = k_ref.shape[0]
    n_tiles = seq // tq
    s_bufs, p_bufs = (s0_ref, s1_ref), (p0_ref, p1_ref)

    @pl.when(pl.program_id(1) == 0)
    def _():
        for h in range(2):
            row = jnp.broadcast_to(tv_ref[h], (tq, tv_ref.shape[2]))
            toep = pltpu.roll(row, 0, 1, stride=1, stride_axis=0)
            for dist in range(n_tiles):
                bias_ref[dist, :, h * tq:(h + 1) * tq] = toep[:, dist * tq:(dist + 1) * tq]

    zeros = jnp.zeros((HEAD_DIM, tq), BF16)
    ones = jnp.ones((ONES_ROWS, tq), BF16)
    items = [(qb, kb) for qb in range(n_tiles) for kb in range(qb + 1)]
    col_max, state = {}, {}

    def stage_a(t):
        qb, kb = items[t]
        qt = qt_ref[:, qb * tq:(qb + 1) * tq]
        q_both = jnp.concatenate([jnp.concatenate([qt[:HEAD_DIM], zeros], axis=0),
                                  jnp.concatenate([zeros, qt[HEAD_DIM:]], axis=0)], axis=1)
        s = jnp.dot(k_ref[kb * tq:(kb + 1) * tq, :], q_both, preferred_element_type=F32)
        s = s + bias_ref[qb - kb]
        s_bufs[t % 2][...] = s
        col_max[t] = jnp.max(s, axis=0, keepdims=True)

    def stage_b(t):
        _, kb = items[t]
        m_new = col_max.pop(t)
        alpha = None
        if kb > 0:
            m_old = state["m"]
            m_new = jnp.maximum(m_old, m_new)
            alpha = jnp.exp2(m_old - m_new)
        state["m"] = m_new
        state["alpha", t] = alpha
        p_bufs[t % 2][...] = jnp.exp2(s_bufs[t % 2][...] - m_new).astype(BF16)

    def stage_c(t):
        qb, kb = items[t]
        lhs = jnp.concatenate([vt_ref[:, kb * tq:(kb + 1) * tq], ones], axis=0)
        acc = jnp.dot(lhs, p_bufs[t % 2][...], preferred_element_type=F32)
        alpha = state.pop(("alpha", t))
        if kb > 0:
            acc = alpha * state["acc"] + acc
        state["acc"] = acc
        if kb == qb:
            d2 = 2 * HEAD_DIM
            ot = jnp.concatenate([acc[:HEAD_DIM, :tq] / acc[d2:d2 + 1, :tq],
                                  acc[HEAD_DIM:d2, tq:] / acc[d2:d2 + 1, tq:]], axis=0)
            o_ref[qb * tq:(qb + 1) * tq, :] = ot.T.astype(BF16)

    for t in range(len(items) + 2):
        if t < len(items):
            stage_a(t)
        if 1 <= t <= len(items):
            stage_b(t - 1)
        if t >= 2:
            stage_c(t - 2)


def _attention(qt, k, vt, tv):
    bsz, s, d = k.shape
    n_pairs = d // LANES
    t_spec = pl.BlockSpec((None, LANES, s), lambda hp, b: (b, hp, 0))
    r_spec = pl.BlockSpec((None, s, LANES), lambda hp, b: (b, 0, hp))
    return pl.pallas_call(
        _attn_body,
        grid=(n_pairs, bsz),
        in_specs=[pl.BlockSpec((2, 1, tv.shape[2]), lambda hp, b: (hp, 0, 0)),
                  t_spec, r_spec, t_spec],
        out_specs=r_spec,
        out_shape=jax.ShapeDtypeStruct((bsz, s, d), BF16),
        scratch_shapes=[pltpu.VMEM((s // ATT_TILE, ATT_TILE, 2 * ATT_TILE), F32),
                        pltpu.VMEM((ATT_TILE, 2 * ATT_TILE), F32),
                        pltpu.VMEM((ATT_TILE, 2 * ATT_TILE), F32),
                        pltpu.VMEM((ATT_TILE, 2 * ATT_TILE), BF16),
                        pltpu.VMEM((ATT_TILE, 2 * ATT_TILE), BF16)],
        compiler_params=_params(2),
        name="dilated_attention",
    )(tv, qt, k, vt)


def kernel(x, mix_norm, ffn_norm, final_norm, conv_w_in, conv_kernel, conv_w_out,
           attn_w_qkv, attn_w_out, rel_bias, ffn_w_gate, ffn_w_up, ffn_w_down):
    bsz, s, d = x.shape
    depth = mix_norm.shape[0]
    assert d == N_HEADS * HEAD_DIM and s % ROW_TILE == 0 and ROW_TILE % ATT_TILE == 0
    assert max(w for w, _ in BRANCHES) <= s

    w_qt = jnp.swapaxes(attn_w_qkv[:, :, :d], 1, 2).astype(BF16)
    w_k = attn_w_qkv[:, :, d:2 * d].astype(BF16)
    w_vt = jnp.swapaxes(attn_w_qkv[:, :, 2 * d:], 1, 2).astype(BF16)
    w_in = conv_w_in.astype(BF16)
    w_co = conv_w_out.astype(BF16)
    w_ao = attn_w_out.astype(BF16)
    w_g = ffn_w_gate.astype(BF16)
    w_u = ffn_w_up.astype(BF16)
    w_d = ffn_w_down.astype(BF16)
    tv = _bias_rows(rel_bias, s)
    gf = final_norm.reshape(1, d)

    x2 = x.reshape(bsz * s, d)
    for i in range(depth):
        j = i // 2
        g_mix = mix_norm[i].reshape(1, d)
        x3 = x2.reshape(bsz, s, d)
        if i % 2 == 0:
            z = _conv_pre(x3, g_mix, w_in, conv_kernel, j)
            w_out = w_co
        else:
            qt, k, vt = _qkv_proj(x3, g_mix, w_qt, w_k, w_vt, j)
            z = _attention(qt, k, vt, tv)
            w_out = w_ao
        x2 = _mix_ffn(x2, z.reshape(bsz * s, d), w_out, j, ffn_norm[i].reshape(1, d),
                      w_g, w_u, w_d, i, gf, final=(i == depth - 1))
    return x2.reshape(bsz, s, d)
```

```python
import functools
import math

import jax
import jax.numpy as jnp
from jax import lax
from jax.experimental import pallas as pl
from jax.experimental.pallas import tpu as pltpu

N_HEADS = 16
HEAD_DIM = 64
CONV_WIDTH = 3
BRANCHES = ((128, 1), (512, 4), (2048, 16))
NUM_BUCKETS = 32
MAX_DISTANCE = 2048
EPS = 1e-6
NEG_INF = -1e30
LOG2E = math.log2(math.e)

LANES = 128
ROW_TILE = 512
COL_CHUNK = 512
FF_CHUNK = 256
ATT_TILE = 256
STRIDE = 4
NEAR = ATT_TILE // 2
ONES_ROWS = 16
VMEM_LIMIT = 56 * 1024 * 1024

F32 = jnp.float32
BF16 = jnp.bfloat16
NT_DIMS = (((1,), (1,)), ((), ()))


def _rms(x, g):
    ms = jnp.mean(x * x, axis=-1, keepdims=True)
    return x * lax.rsqrt(ms + EPS) * g


def _const_spec(shape):
    nd = len(shape)
    return pl.BlockSpec(shape, lambda *_: (0,) * nd, pipeline_mode=pl.Buffered(1))


def _layer_spec(stacked, layer):
    tail = (0,) * (stacked.ndim - 1)
    return pl.BlockSpec((None,) + stacked.shape[1:], lambda *_: (layer,) + tail,
                        pipeline_mode=pl.Buffered(1))


def _params(n_axes):
    return pltpu.CompilerParams(
        dimension_semantics=("arbitrary",) * n_axes, vmem_limit_bytes=VMEM_LIMIT)


def _qkv_body(x_ref, g_ref, w_ref, o_ref, os_ref, slab_ref):
    tm, d = x_ref.shape
    h = _rms(x_ref[...], g_ref[...]).astype(BF16)
    q_scale = LOG2E * HEAD_DIM ** -0.5
    for j in range(0, w_ref.shape[1], COL_CHUNK):
        y = jnp.dot(h, w_ref[:, j:j + COL_CHUNK], preferred_element_type=F32)
        if j < d:
            y = y * q_scale
        o_ref[:, j:j + COL_CHUNK] = y.astype(BF16)
        for c in range(COL_CHUNK // LANES):
            slab_ref[c] = y[:, c * LANES:(c + 1) * LANES]
        for r in range(STRIDE):
            for c in range(COL_CHUNK // LANES):
                part = slab_ref[c, pl.ds(r, tm // STRIDE, stride=STRIDE), :]
                os_ref[r, :, j + c * LANES:j + (c + 1) * LANES] = part.astype(BF16)


def _qkv_proj(x2, g, w, layer, seq):
    t, d = x2.shape
    n = w.shape[2]
    tm = ROW_TILE
    per_seq = seq // tm
    return pl.pallas_call(
        _qkv_body,
        grid=(t // tm,),
        in_specs=[pl.BlockSpec((tm, d), lambda i: (i, 0)),
                  _const_spec((1, d)), _layer_spec(w, layer)],
        out_specs=[pl.BlockSpec((tm, n), lambda i: (i, 0)),
                   pl.BlockSpec((None, STRIDE, tm // STRIDE, n),
                                lambda i: (i // per_seq, 0, i % per_seq, 0))],
        out_shape=[jax.ShapeDtypeStruct((t, n), BF16),
                   jax.ShapeDtypeStruct((t // seq, STRIDE, seq // STRIDE, n), BF16)],
        scratch_shapes=[pltpu.VMEM((COL_CHUNK // LANES, tm, LANES), F32)],
        compiler_params=_params(1),
        name="qkv_proj",
    )(x2, g, w)


def _conv_body(x_ref, g_ref, w_ref, k_ref, z_ref, cu_ref):
    tm, d = x_ref.shape
    first = pl.program_id(1) == 0

    @pl.when(first)
    def _():
        cu_ref[0:8, :] = jnp.zeros((8, d), F32)

    @pl.when(jnp.logical_not(first))
    def _():
        cu_ref[0:8, :] = cu_ref[tm:tm + 8, :]

    h = _rms(x_ref[...], g_ref[...]).astype(BF16)
    for j in range(0, d, COL_CHUNK):
        cols = slice(j, j + COL_CHUNK)
        c = jnp.dot(h, w_ref[:, d + j:d + j + COL_CHUNK], preferred_element_type=F32)
        u = jnp.dot(h, w_ref[:, 2 * d + j:2 * d + j + COL_CHUNK], preferred_element_type=F32)
        cu_ref[8:8 + tm, cols] = c * u
        y = (k_ref[2:3, cols] * cu_ref[8:8 + tm, cols]
             + k_ref[1:2, cols] * cu_ref[7:7 + tm, cols]
             + k_ref[0:1, cols] * cu_ref[6:6 + tm, cols])
        b = jnp.dot(h, w_ref[:, cols], preferred_element_type=F32)
        z_ref[:, cols] = (b * y).astype(BF16)


def _conv_pre(x, g, w_in, taps, layer):
    bsz, s, d = x.shape
    tm = ROW_TILE
    return pl.pallas_call(
        _conv_body,
        grid=(bsz, s // tm),
        in_specs=[pl.BlockSpec((None, tm, d), lambda b, i: (b, i, 0)),
                  _const_spec((1, d)), _layer_spec(w_in, layer), _layer_spec(taps, layer)],
        out_specs=pl.BlockSpec((None, tm, d), lambda b, i: (b, i, 0)),
        out_shape=jax.ShapeDtypeStruct((bsz, s, d), BF16),
        scratch_shapes=[pltpu.VMEM((tm + 8, d), F32)],
        compiler_params=_params(2),
        name="conv_pre",
    )(x, g, w_in, taps)


def _ffn_body(x_ref, z_ref, wo_ref, g_ref, wg_ref, wu_ref, wd_ref, gf_ref, o_ref, *, final):
    x1 = x_ref[...] + jnp.dot(z_ref[...], wo_ref[...], preferred_element_type=F32)
    h = _rms(x1, g_ref[...]).astype(BF16)
    o_ref[...] = x1
    for c in range(0, wg_ref.shape[1], FF_CHUNK):
        gate = jnp.dot(h, wg_ref[:, c:c + FF_CHUNK], preferred_element_type=F32)
        up = jnp.dot(h, wu_ref[:, c:c + FF_CHUNK], preferred_element_type=F32)
        a = (gate * jax.nn.sigmoid(gate) * up).astype(BF16)
        o_ref[...] += jnp.dot(a, wd_ref[c:c + FF_CHUNK, :], preferred_element_type=F32)
    if final:
        o_ref[...] = _rms(o_ref[...], gf_ref[...])


def _mix_ffn(x2, z2, wo, mix_layer, g, wg, wu, wd, layer, gf, final):
    t, d = x2.shape
    row = lambda i: (i, 0)
    return pl.pallas_call(
        functools.partial(_ffn_body, final=final),
        grid=(t // ROW_TILE,),
        in_specs=[pl.BlockSpec((ROW_TILE, d), row), pl.BlockSpec((ROW_TILE, d), row),
                  _layer_spec(wo, mix_layer), _const_spec((1, d)),
                  _layer_spec(wg, layer), _layer_spec(wu, layer), _layer_spec(wd, layer),
                  _const_spec((1, d))],
        out_specs=pl.BlockSpec((ROW_TILE, d), row),
        out_shape=jax.ShapeDtypeStruct((t, d), F32),
        compiler_params=_params(1),
        name="mix_ffn_final" if final else "mix_ffn",
    )(x2, z2, wo, g, wg, wu, wd, gf)


def _t5_bucket(dist):
    exact = NUM_BUCKETS // 2
    df = jnp.maximum(dist, 1).astype(F32)
    large = exact + (jnp.log(df / exact) / math.log(MAX_DISTANCE / exact)
                     * (NUM_BUCKETS - exact)).astype(jnp.int32)
    large = jnp.minimum(large, NUM_BUCKETS - 1)
    return jnp.where(dist < exact, dist, large)


def _bias_rows(rel_bias, n_dist, stride, branches, period):
    dist = jnp.arange(n_dist, dtype=jnp.int32) * stride
    mult = jnp.zeros((n_dist,), F32)
    for window, dilation in branches:
        mult = mult + ((dist % dilation == 0) & (dist <= window)).astype(F32)
    table = rel_bias[_t5_bucket(dist)].astype(F32)
    total = jnp.where(mult[:, None] > 0,
                      (table + jnp.log(jnp.maximum(mult, 1.0))[:, None]) * LOG2E, NEG_INF)
    pad = jnp.full((period - n_dist, total.shape[1]), NEG_INF, F32)
    return jnp.concatenate([total, pad], axis=0).T[:, None, :]


def _attn_body(tv1_ref, tv2_ref, q_ref, k_ref, v_ref, qs_ref, ks_ref, vs_ref, o_ref,
               bias1_ref, bias10_ref, bias2_ref, lhs_ref, lhss_ref, o2_ref, lse2_ref,
               s0_ref, s1_ref, p0_ref, p1_ref):
    tq = ATT_TILE
    seq = k_ref.shape[0]
    sub = ks_ref.shape[1]
    n_sub = sub // tq
    s_bufs, p_bufs = (s0_ref, s1_ref), (p0_ref, p1_ref)

    @pl.when(pl.program_id(1) == 0)
    def _():
        for h in range(2):
            cols = slice(h * tq, (h + 1) * tq)
            row = jnp.broadcast_to(tv1_ref[h], (tq + NEAR, tv1_ref.shape[2]))
            toep = pltpu.roll(row, 0, 1, stride=1, stride_axis=0)
            bias10_ref[:, cols] = toep[:tq, :tq]
            bias1_ref[:, cols] = toep[:, NEAR:NEAR + tq]
            row = jnp.broadcast_to(tv2_ref[h], (tq, tv2_ref.shape[2]))
            toep = pltpu.roll(row, 0, 1, stride=1, stride_axis=0)
            for dist in range(n_sub):
                start = (n_sub - 1 - dist) * tq
                bias2_ref[start:start + tq, cols] = toep[:, dist * tq:(dist + 1) * tq]

    for kb in range(seq // tq):
        lhs_ref[:LANES, kb * tq:(kb + 1) * tq] = v_ref[kb * tq:(kb + 1) * tq, :].T
    lhs_ref[LANES:, :] = jnp.ones((ONES_ROWS, seq), BF16)
    for r in range(STRIDE):
        for kb in range(n_sub):
            lhss_ref[r, :LANES, kb * tq:(kb + 1) * tq] = vs_ref[r, kb * tq:(kb + 1) * tq, :].T
        lhss_ref[r, LANES:, :] = jnp.ones((ONES_ROWS, sub), BF16)

    head0 = lax.broadcasted_iota(jnp.int32, (1, LANES), 1) < HEAD_DIM
    items = [(r, j, 0, (j + 1) * tq) for r in range(STRIDE) for j in range(n_sub)]
    items += [(None, qb, max(0, qb * tq - NEAR), (qb + 1) * tq) for qb in range(seq // tq)]
    col_max = {}

    def stage_a(t):
        r, qb, lo, hi = items[t]
        rows = slice(qb * tq, (qb + 1) * tq)
        if r is None:
            q, k = q_ref[rows, :], k_ref[lo:hi, :]
            bias = bias10_ref[...] if lo == qb * tq else bias1_ref[...]
        else:
            q, k = qs_ref[r, rows, :], ks_ref[r, lo:hi, :]
            bias = bias2_ref[sub - hi:, :]
        zero = jnp.zeros_like(q)
        q_both = jnp.concatenate([jnp.where(head0, q, zero), jnp.where(head0, zero, q)], axis=0)
        s = lax.dot_general(k, q_both, NT_DIMS, preferred_element_type=F32) + bias
        s_bufs[t % 2][:hi - lo, :] = s
        col_max[t] = jnp.max(s, axis=0, keepdims=True)

    def stage_b(t):
        _, _, lo, hi = items[t]
        p_bufs[t % 2][:hi - lo, :] = jnp.exp2(
            s_bufs[t % 2][:hi - lo, :] - col_max[t]).astype(BF16)

    def stage_c(t):
        r, qb, lo, hi = items[t]
        lhs = lhs_ref[:, lo:hi] if r is None else lhss_ref[r, :, lo:hi]
        acc = jnp.dot(lhs, p_bufs[t % 2][:hi - lo, :], preferred_element_type=F32)
        den = acc[LANES:LANES + 1]
        lse = col_max.pop(t) + jnp.log2(den)
        out_t = jnp.concatenate([acc[:HEAD_DIM, :tq] / den[:, :tq],
                                 acc[HEAD_DIM:LANES, tq:] / den[:, tq:]], axis=0)
        lse_t = jnp.concatenate([jnp.broadcast_to(lse[:, :tq], (HEAD_DIM, tq)),
                                 jnp.broadcast_to(lse[:, tq:], (HEAD_DIM, tq))], axis=0)
        out, lse = out_t.T, lse_t.T
        if r is not None:
            token_rows = pl.ds(qb * tq * STRIDE + r, tq, stride=STRIDE)
            o2_ref[token_rows, :] = out
            lse2_ref[token_rows, :] = lse
        else:
            rows = slice(qb * tq, (qb + 1) * tq)
            out2, lse2 = o2_ref[rows, :], lse2_ref[rows, :]
            top = jnp.maximum(lse, lse2)
            w1, w2 = jnp.exp2(lse - top), jnp.exp2(lse2 - top)
            o_ref[rows, :] = ((w1 * out + w2 * out2) / (w1 + w2)).astype(BF16)

    for t in range(len(items) + 2):
        if t < len(items):
            stage_a(t)
        if 1 <= t <= len(items):
            stage_b(t - 1)
        if t >= 2:
            stage_c(t - 2)


def _attention(qkv, qkv_s, tv1, tv2):
    bsz, s, d3 = qkv.shape
    d = d3 // 3
    n_pairs = d // LANES
    sub = s // STRIDE
    col = lambda off: pl.BlockSpec((None, s, LANES), lambda hp, b: (b, 0, off + hp))
    col_s = lambda off: pl.BlockSpec((None, STRIDE, sub, LANES),
                                     lambda hp, b: (b, 0, 0, off + hp))
    bias_row = lambda tv: pl.BlockSpec((2, 1, tv.shape[2]), lambda hp, b: (hp, 0, 0))
    wide = 2 * ATT_TILE
    return pl.pallas_call(
        _attn_body,
        grid=(n_pairs, bsz),
        in_specs=[bias_row(tv1), bias_row(tv2),
                  col(0), col(n_pairs), col(2 * n_pairs),
                  col_s(0), col_s(n_pairs), col_s(2 * n_pairs)],
        out_specs=col(0),
        out_shape=jax.ShapeDtypeStruct((bsz, s, d), BF16),
        scratch_shapes=[pltpu.VMEM((ATT_TILE + NEAR, wide), F32),
                        pltpu.VMEM((ATT_TILE, wide), F32),
                        pltpu.VMEM((sub, wide), F32),
                        pltpu.VMEM((LANES + ONES_ROWS, s), BF16),
                        pltpu.VMEM((STRIDE, LANES + ONES_ROWS, sub), BF16),
                        pltpu.VMEM((s, LANES), F32),
                        pltpu.VMEM((s, LANES), F32),
                        pltpu.VMEM((sub, wide), F32),
                        pltpu.VMEM((sub, wide), F32),
                        pltpu.VMEM((sub, wide), BF16),
                        pltpu.VMEM((sub, wide), BF16)],
        compiler_params=_params(2),
        name="dilated_attention",
    )(tv1, tv2, qkv, qkv, qkv, qkv_s, qkv_s, qkv_s)


def kernel(x, mix_norm, ffn_norm, final_norm, conv_w_in, conv_kernel, conv_w_out,
           attn_w_qkv, attn_w_out, rel_bias, ffn_w_gate, ffn_w_up, ffn_w_down):
    bsz, s, d = x.shape
    depth = mix_norm.shape[0]
    assert d == N_HEADS * HEAD_DIM and s % ROW_TILE == 0 and ROW_TILE % ATT_TILE == 0
    near = tuple((w, dil) for w, dil in BRANCHES if dil == 1)
    far = tuple((w, dil) for w, dil in BRANCHES if dil > 1)
    assert all(w <= NEAR for w, _ in near) and all(dil % STRIDE == 0 for _, dil in far)
    assert s % (STRIDE * ATT_TILE) == 0 and ROW_TILE % STRIDE == 0

    w_qkv = attn_w_qkv.astype(BF16)
    w_in = conv_w_in.astype(BF16)
    w_co = conv_w_out.astype(BF16)
    w_ao = attn_w_out.astype(BF16)
    w_g = ffn_w_gate.astype(BF16)
    w_u = ffn_w_up.astype(BF16)
    w_d = ffn_w_down.astype(BF16)
    tv1 = _bias_rows(rel_bias, NEAR + 1, 1, near, 3 * ATT_TILE)
    tv2 = _bias_rows(rel_bias, s // STRIDE, STRIDE, far, s // STRIDE + ATT_TILE)
    gf = final_norm.reshape(1, d)

    x2 = x.reshape(bsz * s, d)
    for i in range(depth):
        j = i // 2
        g_mix = mix_norm[i].reshape(1, d)
        if i % 2 == 0:
            z = _conv_pre(x2.reshape(bsz, s, d), g_mix, w_in, conv_kernel, j)
            w_out = w_co
        else:
            qkv, qkv_s = _qkv_proj(x2, g_mix, w_qkv, j, s)
            z = _attention(qkv.reshape(bsz, s, 3 * d), qkv_s, tv1, tv2)
            w_out = w_ao
        x2 = _mix_ffn(x2, z.reshape(bsz * s, d), w_out, j, ffn_norm[i].reshape(1, d),
                      w_g, w_u, w_d, i, gf, final=(i == depth - 1))
    return x2.reshape(bsz, s, d)
```

```python
import functools
import math

import jax
import jax.numpy as jnp
from jax import lax
from jax.experimental import pallas as pl
from jax.experimental.pallas import tpu as pltpu

N_HEADS = 16
HEAD_DIM = 64
CONV_WIDTH = 3
BRANCHES = ((128, 1), (512, 4), (2048, 16))
NUM_BUCKETS = 32
MAX_DISTANCE = 2048
EPS = 1e-6
NEG_INF = -1e30
LOG2E = math.log2(math.e)

LANES = 128
ROW_TILE = 1024
COL_CHUNK = 512
FF_CHUNK = 256
ATT_TILE = 256
STRIDE = 4
NEAR = ATT_TILE // 2
ONES_ROWS = 16
VMEM_LIMIT = 56 * 1024 * 1024

F32 = jnp.float32
BF16 = jnp.bfloat16
NT_DIMS = (((1,), (1,)), ((), ()))


def _rms(x, g):
    ms = jnp.mean(x * x, axis=-1, keepdims=True)
    return x * lax.rsqrt(ms + EPS) * g


def _const_spec(shape):
    nd = len(shape)
    return pl.BlockSpec(shape, lambda *_: (0,) * nd, pipeline_mode=pl.Buffered(1))


def _layer_spec(stacked, layer):
    tail = (0,) * (stacked.ndim - 1)
    return pl.BlockSpec((None,) + stacked.shape[1:], lambda *_: (layer,) + tail,
                        pipeline_mode=pl.Buffered(1))


def _params(n_axes):
    return pltpu.CompilerParams(
        dimension_semantics=("arbitrary",) * n_axes, vmem_limit_bytes=VMEM_LIMIT)


def _qkv_body(x_ref, g_ref, w_ref, o_ref, os_ref, slab_ref):
    tm, d = x_ref.shape
    h = _rms(x_ref[...], g_ref[...]).astype(BF16)
    q_scale = LOG2E * HEAD_DIM ** -0.5
    for j in range(0, w_ref.shape[1], COL_CHUNK):
        y = jnp.dot(h, w_ref[:, j:j + COL_CHUNK], preferred_element_type=F32)
        if j < d:
            y = y * q_scale
        o_ref[:, j:j + COL_CHUNK] = y.astype(BF16)
        for c in range(COL_CHUNK // LANES):
            slab_ref[c] = y[:, c * LANES:(c + 1) * LANES]
        for r in range(STRIDE):
            for c in range(COL_CHUNK // LANES):
                part = slab_ref[c, pl.ds(r, tm // STRIDE, stride=STRIDE), :]
                os_ref[r, :, j + c * LANES:j + (c + 1) * LANES] = part.astype(BF16)


def _qkv_proj(x2, g, w, layer, seq):
    t, d = x2.shape
    n = w.shape[2]
    tm = ROW_TILE
    per_seq = seq // tm
    return pl.pallas_call(
        _qkv_body,
        grid=(t // tm,),
        in_specs=[pl.BlockSpec((tm, d), lambda i: (i, 0)),
                  _const_spec((1, d)), _layer_spec(w, layer)],
        out_specs=[pl.BlockSpec((tm, n), lambda i: (i, 0)),
                   pl.BlockSpec((None, STRIDE, tm // STRIDE, n),
                                lambda i: (i // per_seq, 0, i % per_seq, 0))],
        out_shape=[jax.ShapeDtypeStruct((t, n), BF16),
                   jax.ShapeDtypeStruct((t // seq, STRIDE, seq // STRIDE, n), BF16)],
        scratch_shapes=[pltpu.VMEM((COL_CHUNK // LANES, tm, LANES), F32)],
        compiler_params=_params(1),
        name="qkv_proj",
    )(x2, g, w)


def _conv_body(x_ref, g_ref, w_ref, k_ref, z_ref, cu_ref):
    tm, d = x_ref.shape
    first = pl.program_id(1) == 0

    @pl.when(first)
    def _():
        cu_ref[0:8, :] = jnp.zeros((8, d), F32)

    @pl.when(jnp.logical_not(first))
    def _():
        cu_ref[0:8, :] = cu_ref[tm:tm + 8, :]

    h = _rms(x_ref[...], g_ref[...]).astype(BF16)
    for j in range(0, d, COL_CHUNK):
        cols = slice(j, j + COL_CHUNK)
        c = jnp.dot(h, w_ref[:, d + j:d + j + COL_CHUNK], preferred_element_type=F32)
        u = jnp.dot(h, w_ref[:, 2 * d + j:2 * d + j + COL_CHUNK], preferred_element_type=F32)
        cu_ref[8:8 + tm, cols] = c * u
        y = (k_ref[2:3, cols] * cu_ref[8:8 + tm, cols]
             + k_ref[1:2, cols] * cu_ref[7:7 + tm, cols]
             + k_ref[0:1, cols] * cu_ref[6:6 + tm, cols])
        b = jnp.dot(h, w_ref[:, cols], preferred_element_type=F32)
        z_ref[:, cols] = (b * y).astype(BF16)


def _conv_pre(x, g, w_in, taps, layer):
    bsz, s, d = x.shape
    tm = ROW_TILE
    return pl.pallas_call(
        _conv_body,
        grid=(bsz, s // tm),
        in_specs=[pl.BlockSpec((None, tm, d), lambda b, i: (b, i, 0)),
                  _const_spec((1, d)), _layer_spec(w_in, layer), _layer_spec(taps, layer)],
        out_specs=pl.BlockSpec((None, tm, d), lambda b, i: (b, i, 0)),
        out_shape=jax.ShapeDtypeStruct((bsz, s, d), BF16),
        scratch_shapes=[pltpu.VMEM((tm + 8, d), F32)],
        compiler_params=_params(2),
        name="conv_pre",
    )(x, g, w_in, taps)


def _ffn_body(x_ref, z_ref, wo_ref, g_ref, wg_ref, wu_ref, wd_ref, gf_ref, o_ref, *, final):
    x1 = x_ref[...] + jnp.dot(z_ref[...], wo_ref[...], preferred_element_type=F32)
    h = _rms(x1, g_ref[...]).astype(BF16)
    o_ref[...] = x1
    for c in range(0, wg_ref.shape[1], FF_CHUNK):
        gate = jnp.dot(h, wg_ref[:, c:c + FF_CHUNK], preferred_element_type=F32)
        up = jnp.dot(h, wu_ref[:, c:c + FF_CHUNK], preferred_element_type=F32)
        a = (gate * jax.nn.sigmoid(gate) * up).astype(BF16)
        o_ref[...] += jnp.dot(a, wd_ref[c:c + FF_CHUNK, :], preferred_element_type=F32)
    if final:
        o_ref[...] = _rms(o_ref[...], gf_ref[...])


def _mix_ffn(x2, z2, wo, mix_layer, g, wg, wu, wd, layer, gf, final):
    t, d = x2.shape
    row = lambda i: (i, 0)
    return pl.pallas_call(
        functools.partial(_ffn_body, final=final),
        grid=(t // ROW_TILE,),
        in_specs=[pl.BlockSpec((ROW_TILE, d), row), pl.BlockSpec((ROW_TILE, d), row),
                  _layer_spec(wo, mix_layer), _const_spec((1, d)),
                  _layer_spec(wg, layer), _layer_spec(wu, layer), _layer_spec(wd, layer),
                  _const_spec((1, d))],
        out_specs=pl.BlockSpec((ROW_TILE, d), row),
        out_shape=jax.ShapeDtypeStruct((t, d), F32),
        compiler_params=_params(1),
        name="mix_ffn_final" if final else "mix_ffn",
    )(x2, z2, wo, g, wg, wu, wd, gf)


def _t5_bucket(dist):
    exact = NUM_BUCKETS // 2
    df = jnp.maximum(dist, 1).astype(F32)
    large = exact + (jnp.log(df / exact) / math.log(MAX_DISTANCE / exact)
                     * (NUM_BUCKETS - exact)).astype(jnp.int32)
    large = jnp.minimum(large, NUM_BUCKETS - 1)
    return jnp.where(dist < exact, dist, large)


def _bias_rows(rel_bias, n_dist, stride, branches, period):
    dist = jnp.arange(n_dist, dtype=jnp.int32) * stride
    mult = jnp.zeros((n_dist,), F32)
    for window, dilation in branches:
        mult = mult + ((dist % dilation == 0) & (dist <= window)).astype(F32)
    table = rel_bias[_t5_bucket(dist)].astype(F32)
    total = jnp.where(mult[:, None] > 0,
                      (table + jnp.log(jnp.maximum(mult, 1.0))[:, None]) * LOG2E, NEG_INF)
    pad = jnp.full((period - n_dist, total.shape[1]), NEG_INF, F32)
    return jnp.concatenate([total, pad], axis=0).T[:, None, :]


def _attn_body(tv1_ref, tv2_ref, q_ref, k_ref, v_ref, qs_ref, ks_ref, vs_ref, o_ref,
               bias1_ref, bias10_ref, bias2_ref, lhs_ref, lhss_ref, o2_ref, lse2_ref):
    tq = ATT_TILE
    seq = k_ref.shape[0]
    sub = ks_ref.shape[1]
    n_sub = sub // tq

    @pl.when(pl.program_id(1) == 0)
    def _():
        for h in range(2):
            cols = slice(h * tq, (h + 1) * tq)
            row = jnp.broadcast_to(tv1_ref[h], (tq + NEAR, tv1_ref.shape[2]))
            toep = pltpu.roll(row, 0, 1, stride=1, stride_axis=0)
            bias10_ref[:, cols] = toep[:tq, :tq]
            bias1_ref[:, cols] = toep[:, NEAR:NEAR + tq]
            row = jnp.broadcast_to(tv2_ref[h], (tq, tv2_ref.shape[2]))
            toep = pltpu.roll(row, 0, 1, stride=1, stride_axis=0)
            for dist in range(n_sub):
                start = (n_sub - 1 - dist) * tq
                bias2_ref[start:start + tq, cols] = toep[:, dist * tq:(dist + 1) * tq]

    for kb in range(seq // tq):
        lhs_ref[:LANES, kb * tq:(kb + 1) * tq] = v_ref[kb * tq:(kb + 1) * tq, :].T
    lhs_ref[LANES:, :] = jnp.ones((ONES_ROWS, seq), BF16)
    for r in range(STRIDE):
        for kb in range(n_sub):
            lhss_ref[r, :LANES, kb * tq:(kb + 1) * tq] = vs_ref[r, kb * tq:(kb + 1) * tq, :].T
        lhss_ref[r, LANES:, :] = jnp.ones((ONES_ROWS, sub), BF16)

    head0 = lax.broadcasted_iota(jnp.int32, (1, LANES), 1) < HEAD_DIM
    items = [(r, j, 0, (j + 1) * tq) for r in range(STRIDE) for j in range(n_sub)]
    items += [(None, qb, max(0, qb * tq - NEAR), (qb + 1) * tq) for qb in range(seq // tq)]
    col_max, scores, probs = {}, {}, {}

    def stage_a(t):
        r, qb, lo, hi = items[t]
        rows = slice(qb * tq, (qb + 1) * tq)
        if r is None:
            q, k = q_ref[rows, :], k_ref[lo:hi, :]
            bias = bias10_ref[...] if lo == qb * tq else bias1_ref[...]
        else:
            q, k = qs_ref[r, rows, :], ks_ref[r, lo:hi, :]
            bias = bias2_ref[sub - hi:, :]
        zero = jnp.zeros_like(q)
        q_both = jnp.concatenate([jnp.where(head0, q, zero), jnp.where(head0, zero, q)], axis=0)
        s = lax.dot_general(k, q_both, NT_DIMS, preferred_element_type=F32) + bias
        scores[t] = s
        col_max[t] = jnp.max(s, axis=0, keepdims=True)

    def stage_b(t):
        probs[t] = jnp.exp2(scores.pop(t) - col_max[t]).astype(BF16)

    def stage_c(t):
        r, qb, lo, hi = items[t]
        lhs = lhs_ref[:, lo:hi] if r is None else lhss_ref[r, :, lo:hi]
        acc = jnp.dot(lhs, probs.pop(t), preferred_element_type=F32)
        den = acc[LANES:LANES + 1]
        lse = col_max.pop(t) + jnp.log2(den)
        out_t = jnp.concatenate([acc[:HEAD_DIM, :tq] / den[:, :tq],
                                 acc[HEAD_DIM:LANES, tq:] / den[:, tq:]], axis=0)
        lse_t = jnp.concatenate([jnp.broadcast_to(lse[:, :tq], (HEAD_DIM, tq)),
                                 jnp.broadcast_to(lse[:, tq:], (HEAD_DIM, tq))], axis=0)
        out, lse = out_t.T, lse_t.T
        if r is not None:
            token_rows = pl.ds(qb * tq * STRIDE + r, tq, stride=STRIDE)
            o2_ref[token_rows, :] = out
            lse2_ref[token_rows, :] = lse
        else:
            rows = slice(qb * tq, (qb + 1) * tq)
            out2, lse2 = o2_ref[rows, :], lse2_ref[rows, :]
            top = jnp.maximum(lse, lse2)
            w1, w2 = jnp.exp2(lse - top), jnp.exp2(lse2 - top)
            o_ref[rows, :] = ((w1 * out + w2 * out2) / (w1 + w2)).astype(BF16)

    for t in range(len(items) + 2):
        if t < len(items):
            stage_a(t)
        if 1 <= t <= len(items):
            stage_b(t - 1)
        if t >= 2:
            stage_c(t - 2)


def _attention(qkv, qkv_s, tv1, tv2):
    bsz, s, d3 = qkv.shape
    d = d3 // 3
    n_pairs = d // LANES
    sub = s // STRIDE
    col = lambda off: pl.BlockSpec((None, s, LANES), lambda hp, b: (b, 0, off + hp))
    col_s = lambda off: pl.BlockSpec((None, STRIDE, sub, LANES),
                                     lambda hp, b: (b, 0, 0, off + hp))
    bias_row = lambda tv: pl.BlockSpec((2, 1, tv.shape[2]), lambda hp, b: (hp, 0, 0))
    wide = 2 * ATT_TILE
    return pl.pallas_call(
        _attn_body,
        grid=(n_pairs, bsz),
        in_specs=[bias_row(tv1), bias_row(tv2),
                  col(0), col(n_pairs), col(2 * n_pairs),
                  col_s(0), col_s(n_pairs), col_s(2 * n_pairs)],
        out_specs=col(0),
        out_shape=jax.ShapeDtypeStruct((bsz, s, d), BF16),
        scratch_shapes=[pltpu.VMEM((ATT_TILE + NEAR, wide), F32),
                        pltpu.VMEM((ATT_TILE, wide), F32),
                        pltpu.VMEM((sub, wide), F32),
                        pltpu.VMEM((LANES + ONES_ROWS, s), BF16),
                        pltpu.VMEM((STRIDE, LANES + ONES_ROWS, sub), BF16),
                        pltpu.VMEM((s, LANES), F32),
                        pltpu.VMEM((s, LANES), F32)],
        compiler_params=_params(2),
        name="dilated_attention",
    )(tv1, tv2, qkv, qkv, qkv, qkv_s, qkv_s, qkv_s)


def kernel(x, mix_norm, ffn_norm, final_norm, conv_w_in, conv_kernel, conv_w_out,
           attn_w_qkv, attn_w_out, rel_bias, ffn_w_gate, ffn_w_up, ffn_w_down):
    bsz, s, d = x.shape
    depth = mix_norm.shape[0]
    assert d == N_HEADS * HEAD_DIM and s % ROW_TILE == 0 and ROW_TILE % ATT_TILE == 0
    near = tuple((w, dil) for w, dil in BRANCHES if dil == 1)
    far = tuple((w, dil) for w, dil in BRANCHES if dil > 1)
    assert all(w <= NEAR for w, _ in near) and all(dil % STRIDE == 0 for _, dil in far)
    assert s % (STRIDE * ATT_TILE) == 0 and ROW_TILE % STRIDE == 0

    w_qkv = attn_w_qkv.astype(BF16)
    w_in = conv_w_in.astype(BF16)
    w_co = conv_w_out.astype(BF16)
    w_ao = attn_w_out.astype(BF16)
    w_g = ffn_w_gate.astype(BF16)
    w_u = ffn_w_up.astype(BF16)
    w_d = ffn_w_down.astype(BF16)
    tv1 = _bias_rows(rel_bias, NEAR + 1, 1, near, 3 * ATT_TILE)
    tv2 = _bias_rows(rel_bias, s // STRIDE, STRIDE, far, s // STRIDE + ATT_TILE)
    gf = final_norm.reshape(1, d)

    x2 = x.reshape(bsz * s, d)
    for i in range(depth):
        j = i // 2
        g_mix = mix_norm[i].reshape(1, d)
        if i % 2 == 0:
            z = _conv_pre(x2.reshape(bsz, s, d), g_mix, w_in, conv_kernel, j)
            w_out = w_co
        else:
            qkv, qkv_s = _qkv_proj(x2, g_mix, w_qkv, j, s)
            z = _attention(qkv.reshape(bsz, s, 3 * d), qkv_s, tv1, tv2)
            w_out = w_ao
        x2 = _mix_ffn(x2, z.reshape(bsz * s, d), w_out, j, ffn_norm[i].reshape(1, d),
                      w_g, w_u, w_d, i, gf, final=(i == depth - 1))
    return x2.reshape(bsz, s, d)
```

```python
import functools
import math

import jax
import jax.numpy as jnp
from jax import lax
from jax.experimental import pallas as pl
from jax.experimental.pallas import tpu as pltpu

N_HEADS = 16
HEAD_DIM = 64
CONV_WIDTH = 3
BRANCHES = ((128, 1), (512, 4), (2048, 16))
NUM_BUCKETS = 32
MAX_DISTANCE = 2048
EPS = 1e-6
NEG_INF = -1e30
LOG2E = math.log2(math.e)

LANES = 128
ROW_TILE = 1024
COL_CHUNK = 512
FF_CHUNK = 256
ATT_TILE = 256
HEAD_GROUPS = 2
STRIDE = 4
NEAR = ATT_TILE // 2
ONES_ROWS = 16
VMEM_LIMIT = 56 * 1024 * 1024

F32 = jnp.float32
BF16 = jnp.bfloat16
NT_DIMS = (((1,), (1,)), ((), ()))


def _rms(x, g):
    ms = jnp.mean(x * x, axis=-1, keepdims=True)
    return x * lax.rsqrt(ms + EPS) * g


def _const_spec(shape):
    nd = len(shape)
    return pl.BlockSpec(shape, lambda *_: (0,) * nd, pipeline_mode=pl.Buffered(1))


def _layer_spec(stacked, layer):
    tail = (0,) * (stacked.ndim - 1)
    return pl.BlockSpec((None,) + stacked.shape[1:], lambda *_: (layer,) + tail,
                        pipeline_mode=pl.Buffered(1))


def _params(n_axes):
    return pltpu.CompilerParams(
        dimension_semantics=("arbitrary",) * n_axes, vmem_limit_bytes=VMEM_LIMIT)


def _qkv_body(x_ref, g_ref, w_ref, o_ref, os_ref, slab0_ref, slab1_ref):
    tm, d = x_ref.shape
    h = _rms(x_ref[...], g_ref[...]).astype(BF16)
    q_scale = LOG2E * HEAD_DIM ** -0.5
    for j in range(0, w_ref.shape[1], COL_CHUNK):
        slab_ref = (slab0_ref, slab1_ref)[(j // COL_CHUNK) % 2]
        y = jnp.dot(h, w_ref[:, j:j + COL_CHUNK], preferred_element_type=F32)
        if j < d:
            y = y * q_scale
        o_ref[:, j:j + COL_CHUNK] = y.astype(BF16)
        for c in range(COL_CHUNK // LANES):
            slab_ref[c] = y[:, c * LANES:(c + 1) * LANES]
        for r in range(STRIDE):
            for c in range(COL_CHUNK // LANES):
                part = slab_ref[c, pl.ds(r, tm // STRIDE, stride=STRIDE), :]
                os_ref[r, :, j + c * LANES:j + (c + 1) * LANES] = part.astype(BF16)


def _qkv_proj(x2, g, w, layer, seq):
    t, d = x2.shape
    n = w.shape[2]
    tm = ROW_TILE
    per_seq = seq // tm
    return pl.pallas_call(
        _qkv_body,
        grid=(t // tm,),
        in_specs=[pl.BlockSpec((tm, d), lambda i: (i, 0)),
                  _const_spec((1, d)), _layer_spec(w, layer)],
        out_specs=[pl.BlockSpec((tm, n), lambda i: (i, 0)),
                   pl.BlockSpec((None, STRIDE, tm // STRIDE, n),
                                lambda i: (i // per_seq, 0, i % per_seq, 0))],
        out_shape=[jax.ShapeDtypeStruct((t, n), BF16),
                   jax.ShapeDtypeStruct((t // seq, STRIDE, seq // STRIDE, n), BF16)],
        scratch_shapes=[pltpu.VMEM((COL_CHUNK // LANES, tm, LANES), F32)] * 2,
        compiler_params=_params(1),
        name="qkv_proj",
    )(x2, g, w)


def _conv_body(x_ref, g_ref, w_ref, k_ref, z_ref, cu_ref):
    tm, d = x_ref.shape
    first = pl.program_id(1) == 0

    @pl.when(first)
    def _():
        cu_ref[0:8, :] = jnp.zeros((8, d), F32)

    @pl.when(jnp.logical_not(first))
    def _():
        cu_ref[0:8, :] = cu_ref[tm:tm + 8, :]

    h = _rms(x_ref[...], g_ref[...]).astype(BF16)
    for j in range(0, d, COL_CHUNK):
        cols = slice(j, j + COL_CHUNK)
        c = jnp.dot(h, w_ref[:, d + j:d + j + COL_CHUNK], preferred_element_type=F32)
        u = jnp.dot(h, w_ref[:, 2 * d + j:2 * d + j + COL_CHUNK], preferred_element_type=F32)
        cu_ref[8:8 + tm, cols] = c * u
        y = (k_ref[2:3, cols] * cu_ref[8:8 + tm, cols]
             + k_ref[1:2, cols] * cu_ref[7:7 + tm, cols]
             + k_ref[0:1, cols] * cu_ref[6:6 + tm, cols])
        b = jnp.dot(h, w_ref[:, cols], preferred_element_type=F32)
        z_ref[:, cols] = (b * y).astype(BF16)


def _conv_pre(x, g, w_in, taps, layer):
    bsz, s, d = x.shape
    tm = ROW_TILE
    return pl.pallas_call(
        _conv_body,
        grid=(bsz, s // tm),
        in_specs=[pl.BlockSpec((None, tm, d), lambda b, i: (b, i, 0)),
                  _const_spec((1, d)), _layer_spec(w_in, layer), _layer_spec(taps, layer)],
        out_specs=pl.BlockSpec((None, tm, d), lambda b, i: (b, i, 0)),
        out_shape=jax.ShapeDtypeStruct((bsz, s, d), BF16),
        scratch_shapes=[pltpu.VMEM((tm + 8, d), F32)],
        compiler_params=_params(2),
        name="conv_pre",
    )(x, g, w_in, taps)


def _ffn_body(x_ref, z_ref, wo_ref, g_ref, wg_ref, wu_ref, wd_ref, gf_ref, o_ref, *, final):
    x1 = x_ref[...] + jnp.dot(z_ref[...], wo_ref[...], preferred_element_type=F32)
    h = _rms(x1, g_ref[...]).astype(BF16)
    o_ref[...] = x1
    for c in range(0, wg_ref.shape[1], FF_CHUNK):
        gate = jnp.dot(h, wg_ref[:, c:c + FF_CHUNK], preferred_element_type=F32)
        up = jnp.dot(h, wu_ref[:, c:c + FF_CHUNK], preferred_element_type=F32)
        a = (gate * jax.nn.sigmoid(gate) * up).astype(BF16)
        o_ref[...] += jnp.dot(a, wd_ref[c:c + FF_CHUNK, :], preferred_element_type=F32)
    if final:
        o_ref[...] = _rms(o_ref[...], gf_ref[...])


def _mix_ffn(x2, z2, wo, mix_layer, g, wg, wu, wd, layer, gf, final):
    t, d = x2.shape
    row = lambda i: (i, 0)
    return pl.pallas_call(
        functools.partial(_ffn_body, final=final),
        grid=(t // ROW_TILE,),
        in_specs=[pl.BlockSpec((ROW_TILE, d), row), pl.BlockSpec((ROW_TILE, d), row),
                  _layer_spec(wo, mix_layer), _const_spec((1, d)),
                  _layer_spec(wg, layer), _layer_spec(wu, layer), _layer_spec(wd, layer),
                  _const_spec((1, d))],
        out_specs=pl.BlockSpec((ROW_TILE, d), row),
        out_shape=jax.ShapeDtypeStruct((t, d), F32),
        compiler_params=_params(1),
        name="mix_ffn_final" if final else "mix_ffn",
    )(x2, z2, wo, g, wg, wu, wd, gf)


def _t5_bucket(dist):
    exact = NUM_BUCKETS // 2
    df = jnp.maximum(dist, 1).astype(F32)
    large = exact + (jnp.log(df / exact) / math.log(MAX_DISTANCE / exact)
                     * (NUM_BUCKETS - exact)).astype(jnp.int32)
    large = jnp.minimum(large, NUM_BUCKETS - 1)
    return jnp.where(dist < exact, dist, large)


def _bias_rows(rel_bias, n_dist, stride, branches, period):
    dist = jnp.arange(n_dist, dtype=jnp.int32) * stride
    mult = jnp.zeros((n_dist,), F32)
    for window, dilation in branches:
        mult = mult + ((dist % dilation == 0) & (dist <= window)).astype(F32)
    table = rel_bias[_t5_bucket(dist)].astype(F32)
    total = jnp.where(mult[:, None] > 0,
                      (table + jnp.log(jnp.maximum(mult, 1.0))[:, None]) * LOG2E, NEG_INF)
    pad = jnp.full((period - n_dist, total.shape[1]), NEG_INF, F32)
    return jnp.concatenate([total, pad], axis=0).T[:, None, :]


def _attn_body(tv1_ref, tv2_ref, q_ref, k_ref, v_ref, qs_ref, ks_ref, vs_ref, o_ref,
               bias1_ref, bias10_ref, bias2_ref, lhs_ref, lhss_ref, o2_ref, lse2_ref):
    tq = ATT_TILE
    seq = k_ref.shape[0]
    sub = ks_ref.shape[1]
    n_sub = sub // tq
    groups = k_ref.shape[1] // LANES

    @pl.when(pl.program_id(1) == 0)
    def _():
        for g in range(groups):
            for h in range(2):
                cols = slice(h * tq, (h + 1) * tq)
                row = jnp.broadcast_to(tv1_ref[2 * g + h], (tq + NEAR, tv1_ref.shape[2]))
                toep = pltpu.roll(row, 0, 1, stride=1, stride_axis=0)
                bias10_ref[g, :, cols] = toep[:tq, :tq]
                bias1_ref[g, :, cols] = toep[:, NEAR:NEAR + tq]
                row = jnp.broadcast_to(tv2_ref[2 * g + h], (tq, tv2_ref.shape[2]))
                toep = pltpu.roll(row, 0, 1, stride=1, stride_axis=0)
                for dist in range(n_sub):
                    start = (n_sub - 1 - dist) * tq
                    bias2_ref[g, start:start + tq, cols] = toep[:, dist * tq:(dist + 1) * tq]

    for g in range(groups):
        lanes = slice(g * LANES, (g + 1) * LANES)
        for kb in range(seq // tq):
            lhs_ref[g, :LANES, kb * tq:(kb + 1) * tq] = v_ref[kb * tq:(kb + 1) * tq, lanes].T
        lhs_ref[g, LANES:, :] = jnp.ones((ONES_ROWS, seq), BF16)
        for r in range(STRIDE):
            for kb in range(n_sub):
                lhss_ref[g, r, :LANES, kb * tq:(kb + 1) * tq] = (
                    vs_ref[r, kb * tq:(kb + 1) * tq, lanes].T)
            lhss_ref[g, r, LANES:, :] = jnp.ones((ONES_ROWS, sub), BF16)

    head0 = lax.broadcasted_iota(jnp.int32, (1, LANES), 1) < HEAD_DIM
    items = []
    for g in range(groups):
        items += [(g, r, j, 0, (j + 1) * tq) for r in range(STRIDE) for j in range(n_sub)]
        items += [(g, None, qb, max(0, qb * tq - NEAR), (qb + 1) * tq)
                  for qb in range(seq // tq)]
    col_max, scores, probs = {}, {}, {}

    def stage_a(t):
        g, r, qb, lo, hi = items[t]
        rows, lanes = slice(qb * tq, (qb + 1) * tq), slice(g * LANES, (g + 1) * LANES)
        if r is None:
            q, k = q_ref[rows, lanes], k_ref[lo:hi, lanes]
            bias = bias10_ref[g] if lo == qb * tq else bias1_ref[g]
        else:
            q, k = qs_ref[r, rows, lanes], ks_ref[r, lo:hi, lanes]
            bias = bias2_ref[g, sub - hi:, :]
        zero = jnp.zeros_like(q)
        q_both = jnp.concatenate([jnp.where(head0, q, zero), jnp.where(head0, zero, q)], axis=0)
        s = lax.dot_general(k, q_both, NT_DIMS, preferred_element_type=F32) + bias
        scores[t] = s
        col_max[t] = jnp.max(s, axis=0, keepdims=True)

    def stage_b(t):
        probs[t] = jnp.exp2(scores.pop(t) - col_max[t]).astype(BF16)

    def stage_c(t):
        g, r, qb, lo, hi = items[t]
        lhs = lhs_ref[g, :, lo:hi] if r is None else lhss_ref[g, r, :, lo:hi]
        acc = jnp.dot(lhs, probs.pop(t), preferred_element_type=F32)
        den = acc[LANES:LANES + 1]
        lse = col_max.pop(t) + jnp.log2(den)
        out_t = jnp.concatenate([acc[:HEAD_DIM, :tq] / den[:, :tq],
                                 acc[HEAD_DIM:LANES, tq:] / den[:, tq:]], axis=0)
        lse_t = jnp.concatenate([jnp.broadcast_to(lse[:, :tq], (HEAD_DIM, tq)),
                                 jnp.broadcast_to(lse[:, tq:], (HEAD_DIM, tq))], axis=0)
        out, lse = out_t.T, lse_t.T
        if r is not None:
            token_rows = pl.ds(qb * tq * STRIDE + r, tq, stride=STRIDE)
            o2_ref[g, token_rows, :] = out
            lse2_ref[g, token_rows, :] = lse
        else:
            rows = slice(qb * tq, (qb + 1) * tq)
            out2, lse2 = o2_ref[g, rows, :], lse2_ref[g, rows, :]
            top = jnp.maximum(lse, lse2)
            w1, w2 = jnp.exp2(lse - top), jnp.exp2(lse2 - top)
            o_ref[rows, g * LANES:(g + 1) * LANES] = (
                (w1 * out + w2 * out2) / (w1 + w2)).astype(BF16)

    for t in range(len(items) + 2):
        if t < len(items):
            stage_a(t)
        if 1 <= t <= len(items):
            stage_b(t - 1)
        if t >= 2:
            stage_c(t - 2)


def _attention(qkv, qkv_s, tv1, tv2):
    bsz, s, d3 = qkv.shape
    d = d3 // 3
    width = HEAD_GROUPS * LANES
    n_blocks = d // width
    sub = s // STRIDE
    col = lambda off: pl.BlockSpec((None, s, width), lambda hb, b: (b, 0, off + hb))
    col_s = lambda off: pl.BlockSpec((None, STRIDE, sub, width),
                                     lambda hb, b: (b, 0, 0, off + hb))
    bias_row = lambda tv: pl.BlockSpec((2 * HEAD_GROUPS, 1, tv.shape[2]),
                                       lambda hb, b: (hb, 0, 0))
    wide = 2 * ATT_TILE
    return pl.pallas_call(
        _attn_body,
        grid=(n_blocks, bsz),
        in_specs=[bias_row(tv1), bias_row(tv2),
                  col(0), col(n_blocks), col(2 * n_blocks),
                  col_s(0), col_s(n_blocks), col_s(2 * n_blocks)],
        out_specs=col(0),
        out_shape=jax.ShapeDtypeStruct((bsz, s, d), BF16),
        scratch_shapes=[pltpu.VMEM((HEAD_GROUPS, ATT_TILE + NEAR, wide), F32),
                        pltpu.VMEM((HEAD_GROUPS, ATT_TILE, wide), F32),
                        pltpu.VMEM((HEAD_GROUPS, sub, wide), F32),
                        pltpu.VMEM((HEAD_GROUPS, LANES + ONES_ROWS, s), BF16),
                        pltpu.VMEM((HEAD_GROUPS, STRIDE, LANES + ONES_ROWS, sub), BF16),
                        pltpu.VMEM((HEAD_GROUPS, s, LANES), F32),
                        pltpu.VMEM((HEAD_GROUPS, s, LANES), F32)],
        compiler_params=_params(2),
        name="dilated_attention",
    )(tv1, tv2, qkv, qkv, qkv, qkv_s, qkv_s, qkv_s)


def kernel(x, mix_norm, ffn_norm, final_norm, conv_w_in, conv_kernel, conv_w_out,
           attn_w_qkv, attn_w_out, rel_bias, ffn_w_gate, ffn_w_up, ffn_w_down):
    bsz, s, d = x.shape
    depth = mix_norm.shape[0]
    assert d == N_HEADS * HEAD_DIM and s % ROW_TILE == 0 and ROW_TILE % ATT_TILE == 0
    near = tuple((w, dil) for w, dil in BRANCHES if dil == 1)
    far = tuple((w, dil) for w, dil in BRANCHES if dil > 1)
    assert all(w <= NEAR for w, _ in near) and all(dil % STRIDE == 0 for _, dil in far)
    assert s % (STRIDE * ATT_TILE) == 0 and ROW_TILE % STRIDE == 0

    w_qkv = attn_w_qkv.astype(BF16)
    w_in = conv_w_in.astype(BF16)
    w_co = conv_w_out.astype(BF16)
    w_ao = attn_w_out.astype(BF16)
    w_g = ffn_w_gate.astype(BF16)
    w_u = ffn_w_up.astype(BF16)
    w_d = ffn_w_down.astype(BF16)
    tv1 = _bias_rows(rel_bias, NEAR + 1, 1, near, 3 * ATT_TILE)
    tv2 = _bias_rows(rel_bias, s // STRIDE, STRIDE, far, s // STRIDE + ATT_TILE)
    gf = final_norm.reshape(1, d)

    x2 = x.reshape(bsz * s, d)
    for i in range(depth):
        j = i // 2
        g_mix = mix_norm[i].reshape(1, d)
        if i % 2 == 0:
            z = _conv_pre(x2.reshape(bsz, s, d), g_mix, w_in, conv_kernel, j)
            w_out = w_co
        else:
            qkv, qkv_s = _qkv_proj(x2, g_mix, w_qkv, j, s)
            z = _attention(qkv.reshape(bsz, s, 3 * d), qkv_s, tv1, tv2)
            w_out = w_ao
        x2 = _mix_ffn(x2, z.reshape(bsz * s, d), w_out, j, ffn_norm[i].reshape(1, d),
                      w_g, w_u, w_d, i, gf, final=(i == depth - 1))
    return x2.reshape(bsz, s, d)
```

```python
import functools
import math

import jax
import jax.numpy as jnp
from jax import lax
from jax.experimental import pallas as pl
from jax.experimental.pallas import tpu as pltpu

N_HEADS = 16
HEAD_DIM = 64
CONV_WIDTH = 3
BRANCHES = ((128, 1), (512, 4), (2048, 16))
NUM_BUCKETS = 32
MAX_DISTANCE = 2048
EPS = 1e-6
NEG_INF = -1e30
LOG2E = math.log2(math.e)

LANES = 128
ROW_TILE = 1024
COL_CHUNK = 512
FF_CHUNK = 256
ATT_TILE = 256
HEAD_GROUPS = 2
STRIDE = 4
NEAR = ATT_TILE // 2
ONES_ROWS = 16
VMEM_LIMIT = 56 * 1024 * 1024

F32 = jnp.float32
BF16 = jnp.bfloat16
NT_DIMS = (((1,), (1,)), ((), ()))


def _rms(x, g):
    ms = jnp.mean(x * x, axis=-1, keepdims=True)
    return x * lax.rsqrt(ms + EPS) * g


def _const_spec(shape):
    nd = len(shape)
    return pl.BlockSpec(shape, lambda *_: (0,) * nd, pipeline_mode=pl.Buffered(1))


def _layer_spec(stacked, layer):
    tail = (0,) * (stacked.ndim - 1)
    return pl.BlockSpec((None,) + stacked.shape[1:], lambda *_: (layer,) + tail,
                        pipeline_mode=pl.Buffered(1))


def _params(n_axes):
    return pltpu.CompilerParams(
        dimension_semantics=("arbitrary",) * n_axes, vmem_limit_bytes=VMEM_LIMIT)


def _qkv_body(x_ref, g_ref, w_ref, o_ref, os_ref, slab0_ref, slab1_ref):
    tm, d = x_ref.shape
    h = _rms(x_ref[...], g_ref[...]).astype(BF16)
    q_scale = LOG2E * HEAD_DIM ** -0.5
    for j in range(0, w_ref.shape[1], COL_CHUNK):
        slab_ref = (slab0_ref, slab1_ref)[(j // COL_CHUNK) % 2]
        y = jnp.dot(h, w_ref[:, j:j + COL_CHUNK], preferred_element_type=F32)
        if j < d:
            y = y * q_scale
        o_ref[:, j:j + COL_CHUNK] = y.astype(BF16)
        for c in range(COL_CHUNK // LANES):
            slab_ref[c] = y[:, c * LANES:(c + 1) * LANES]
        for r in range(STRIDE):
            for c in range(COL_CHUNK // LANES):
                part = slab_ref[c, pl.ds(r, tm // STRIDE, stride=STRIDE), :]
                os_ref[r, :, j + c * LANES:j + (c + 1) * LANES] = part.astype(BF16)


def _qkv_proj(x2, g, w, layer, seq):
    t, d = x2.shape
    n = w.shape[2]
    tm = ROW_TILE
    per_seq = seq // tm
    return pl.pallas_call(
        _qkv_body,
        grid=(t // tm,),
        in_specs=[pl.BlockSpec((tm, d), lambda i: (i, 0)),
                  _const_spec((1, d)), _layer_spec(w, layer)],
        out_specs=[pl.BlockSpec((tm, n), lambda i: (i, 0)),
                   pl.BlockSpec((None, STRIDE, tm // STRIDE, n),
                                lambda i: (i // per_seq, 0, i % per_seq, 0))],
        out_shape=[jax.ShapeDtypeStruct((t, n), BF16),
                   jax.ShapeDtypeStruct((t // seq, STRIDE, seq // STRIDE, n), BF16)],
        scratch_shapes=[pltpu.VMEM((COL_CHUNK // LANES, tm, LANES), F32)] * 2,
        compiler_params=_params(1),
        name="qkv_proj",
    )(x2, g, w)


def _conv_body(x_ref, g_ref, w_ref, k_ref, z_ref, cu_ref):
    tm, d = x_ref.shape
    first = pl.program_id(1) == 0

    @pl.when(first)
    def _():
        cu_ref[0:8, :] = jnp.zeros((8, d), F32)

    @pl.when(jnp.logical_not(first))
    def _():
        cu_ref[0:8, :] = cu_ref[tm:tm + 8, :]

    h = _rms(x_ref[...], g_ref[...]).astype(BF16)
    for j in range(0, d, COL_CHUNK):
        cols = slice(j, j + COL_CHUNK)
        c = jnp.dot(h, w_ref[:, d + j:d + j + COL_CHUNK], preferred_element_type=F32)
        u = jnp.dot(h, w_ref[:, 2 * d + j:2 * d + j + COL_CHUNK], preferred_element_type=F32)
        cu_ref[8:8 + tm, cols] = c * u
        y = (k_ref[2:3, cols] * cu_ref[8:8 + tm, cols]
             + k_ref[1:2, cols] * cu_ref[7:7 + tm, cols]
             + k_ref[0:1, cols] * cu_ref[6:6 + tm, cols])
        b = jnp.dot(h, w_ref[:, cols], preferred_element_type=F32)
        z_ref[:, cols] = (b * y).astype(BF16)


def _conv_pre(x, g, w_in, taps, layer):
    bsz, s, d = x.shape
    tm = ROW_TILE
    return pl.pallas_call(
        _conv_body,
        grid=(bsz, s // tm),
        in_specs=[pl.BlockSpec((None, tm, d), lambda b, i: (b, i, 0)),
                  _const_spec((1, d)), _layer_spec(w_in, layer), _layer_spec(taps, layer)],
        out_specs=pl.BlockSpec((None, tm, d), lambda b, i: (b, i, 0)),
        out_shape=jax.ShapeDtypeStruct((bsz, s, d), BF16),
        scratch_shapes=[pltpu.VMEM((tm + 8, d), F32)],
        compiler_params=_params(2),
        name="conv_pre",
    )(x, g, w_in, taps)


def _ffn_body(x_ref, z_ref, wo_ref, g_ref, wg_ref, wu_ref, wd_ref, gf_ref, o_ref, *, final):
    x1 = x_ref[...] + jnp.dot(z_ref[...], wo_ref[...], preferred_element_type=F32)
    h = _rms(x1, g_ref[...]).astype(BF16)
    o_ref[...] = x1
    for c in range(0, wg_ref.shape[1], FF_CHUNK):
        gate = jnp.dot(h, wg_ref[:, c:c + FF_CHUNK], preferred_element_type=F32)
        up = jnp.dot(h, wu_ref[:, c:c + FF_CHUNK], preferred_element_type=F32)
        a = (gate * jax.nn.sigmoid(gate) * up).astype(BF16)
        o_ref[...] += jnp.dot(a, wd_ref[c:c + FF_CHUNK, :], preferred_element_type=F32)
    if final:
        o_ref[...] = _rms(o_ref[...], gf_ref[...])


def _mix_ffn(x2, z2, wo, mix_layer, g, wg, wu, wd, layer, gf, final):
    t, d = x2.shape
    row = lambda i: (i, 0)
    return pl.pallas_call(
        functools.partial(_ffn_body, final=final),
        grid=(t // ROW_TILE,),
        in_specs=[pl.BlockSpec((ROW_TILE, d), row), pl.BlockSpec((ROW_TILE, d), row),
                  _layer_spec(wo, mix_layer), _const_spec((1, d)),
                  _layer_spec(wg, layer), _layer_spec(wu, layer), _layer_spec(wd, layer),
                  _const_spec((1, d))],
        out_specs=pl.BlockSpec((ROW_TILE, d), row),
        out_shape=jax.ShapeDtypeStruct((t, d), F32),
        compiler_params=_params(1),
        name="mix_ffn_final" if final else "mix_ffn",
    )(x2, z2, wo, g, wg, wu, wd, gf)


def _t5_bucket(dist):
    exact = NUM_BUCKETS // 2
    df = jnp.maximum(dist, 1).astype(F32)
    large = exact + (jnp.log(df / exact) / math.log(MAX_DISTANCE / exact)
                     * (NUM_BUCKETS - exact)).astype(jnp.int32)
    large = jnp.minimum(large, NUM_BUCKETS - 1)
    return jnp.where(dist < exact, dist, large)


def _bias_rows(rel_bias, n_dist, stride, branches, period):
    dist = jnp.arange(n_dist, dtype=jnp.int32) * stride
    mult = jnp.zeros((n_dist,), F32)
    for window, dilation in branches:
        mult = mult + ((dist % dilation == 0) & (dist <= window)).astype(F32)
    table = rel_bias[_t5_bucket(dist)].astype(F32)
    total = jnp.where(mult[:, None] > 0,
                      (table + jnp.log(jnp.maximum(mult, 1.0))[:, None]) * LOG2E, NEG_INF)
    pad = jnp.full((period - n_dist, total.shape[1]), NEG_INF, F32)
    return jnp.concatenate([total, pad], axis=0).T[:, None, :]


def _attn_body(tv1_ref, tv2_ref, q_ref, k_ref, v_ref, qs_ref, ks_ref, vs_ref, o_ref,
               bias1_ref, bias10_ref, bias2_ref, lhs_ref, lhss_ref, o2_ref, lse2_ref):
    tq = ATT_TILE
    seq, width = k_ref.shape
    sub = ks_ref.shape[1]
    n_sub = sub // tq
    groups = width // LANES
    heads = width // HEAD_DIM

    @pl.when(pl.program_id(1) == 0)
    def _():
        for h in range(heads):
            row = jnp.broadcast_to(tv1_ref[h], (2 * NEAR, tv1_ref.shape[2]))
            toep = pltpu.roll(row, 0, 1, stride=1, stride_axis=0)
            bias10_ref[:, h * NEAR:(h + 1) * NEAR] = toep[:NEAR, :NEAR]
            bias1_ref[:, h * NEAR:(h + 1) * NEAR] = toep[:, NEAR:2 * NEAR]
            g, cols = h // 2, slice((h % 2) * tq, (h % 2 + 1) * tq)
            row = jnp.broadcast_to(tv2_ref[h], (tq, tv2_ref.shape[2]))
            toep = pltpu.roll(row, 0, 1, stride=1, stride_axis=0)
            for dist in range(n_sub):
                start = (n_sub - 1 - dist) * tq
                bias2_ref[g, start:start + tq, cols] = toep[:, dist * tq:(dist + 1) * tq]

    for kb in range(seq // tq):
        lhs_ref[:width, kb * tq:(kb + 1) * tq] = v_ref[kb * tq:(kb + 1) * tq, :].T
    lhs_ref[width:, :] = jnp.ones((ONES_ROWS, seq), BF16)
    for g in range(groups):
        lanes = slice(g * LANES, (g + 1) * LANES)
        for r in range(STRIDE):
            for kb in range(n_sub):
                lhss_ref[g, r, :LANES, kb * tq:(kb + 1) * tq] = (
                    vs_ref[r, kb * tq:(kb + 1) * tq, lanes].T)
            lhss_ref[g, r, LANES:, :] = jnp.ones((ONES_ROWS, sub), BF16)

    lane_head = lax.broadcasted_iota(jnp.int32, (1, width), 1) // HEAD_DIM
    items = [(g, r, j, 0, (j + 1) * tq)
             for g in range(groups) for r in range(STRIDE) for j in range(n_sub)]
    items += [(None, None, qb, max(0, (qb - 1) * NEAR), (qb + 1) * NEAR)
              for qb in range(seq // NEAR)]
    col_max, scores, probs = {}, {}, {}

    def per_head_rows(q):
        zero = jnp.zeros_like(q)
        lanes = lane_head[:, :q.shape[1]]
        return jnp.concatenate(
            [jnp.where(lanes == h, q, zero) for h in range(q.shape[1] // HEAD_DIM)], axis=0)

    def stage_a(t):
        g, r, qb, lo, hi = items[t]
        if g is None:
            q = per_head_rows(q_ref[qb * NEAR:(qb + 1) * NEAR, :])
            k = k_ref[lo:hi, :]
            bias = bias10_ref[...] if lo == qb * NEAR else bias1_ref[...]
        else:
            lanes = slice(g * LANES, (g + 1) * LANES)
            q = per_head_rows(qs_ref[r, qb * tq:(qb + 1) * tq, lanes])
            k = ks_ref[r, lo:hi, lanes]
            bias = bias2_ref[g, sub - hi:, :]
        s = lax.dot_general(k, q, NT_DIMS, preferred_element_type=F32) + bias
        scores[t] = s
        col_max[t] = jnp.max(s, axis=0, keepdims=True)

    def stage_b(t):
        probs[t] = jnp.exp2(scores.pop(t) - col_max[t]).astype(BF16)

    def stage_c(t):
        g, r, qb, lo, hi = items[t]
        lhs = lhs_ref[:, lo:hi] if g is None else lhss_ref[g, r, :, lo:hi]
        acc = jnp.dot(lhs, probs.pop(t), preferred_element_type=F32)
        n_heads = (lhs.shape[0] - ONES_ROWS) // HEAD_DIM
        nq = acc.shape[1] // n_heads
        den = acc[n_heads * HEAD_DIM:n_heads * HEAD_DIM + 1]
        lse = col_max.pop(t) + jnp.log2(den)
        out_t = jnp.concatenate(
            [acc[h * HEAD_DIM:(h + 1) * HEAD_DIM, h * nq:(h + 1) * nq] / den[:, h * nq:(h + 1) * nq]
             for h in range(n_heads)], axis=0)
        lse_t = jnp.concatenate(
            [jnp.broadcast_to(lse[:, h * nq:(h + 1) * nq], (HEAD_DIM, nq)) for h in range(n_heads)],
            axis=0)
        out, lse = out_t.T, lse_t.T
        if g is not None:
            token_rows = pl.ds(qb * tq * STRIDE + r, tq, stride=STRIDE)
            o2_ref[g, token_rows, :] = out
            lse2_ref[g, token_rows, :] = lse
        else:
            rows = slice(qb * NEAR, (qb + 1) * NEAR)
            out2 = jnp.concatenate([o2_ref[i, rows, :] for i in range(groups)], axis=1)
            lse2 = jnp.concatenate([lse2_ref[i, rows, :] for i in range(groups)], axis=1)
            top = jnp.maximum(lse, lse2)
            w1, w2 = jnp.exp2(lse - top), jnp.exp2(lse2 - top)
            o_ref[rows, :] = ((w1 * out + w2 * out2) / (w1 + w2)).astype(BF16)

    for t in range(len(items) + 2):
        if t < len(items):
            stage_a(t)
        if 1 <= t <= len(items):
            stage_b(t - 1)
        if t >= 2:
            stage_c(t - 2)


def _attention(qkv, qkv_s, tv1, tv2):
    bsz, s, d3 = qkv.shape
    d = d3 // 3
    width = HEAD_GROUPS * LANES
    heads = width // HEAD_DIM
    n_blocks = d // width
    sub = s // STRIDE
    col = lambda off: pl.BlockSpec((None, s, width), lambda hb, b: (b, 0, off + hb))
    col_s = lambda off: pl.BlockSpec((None, STRIDE, sub, width),
                                     lambda hb, b: (b, 0, 0, off + hb))
    bias_row = lambda tv: pl.BlockSpec((heads, 1, tv.shape[2]), lambda hb, b: (hb, 0, 0))
    wide = 2 * ATT_TILE
    assert heads * NEAR == wide
    return pl.pallas_call(
        _attn_body,
        grid=(n_blocks, bsz),
        in_specs=[bias_row(tv1), bias_row(tv2),
                  col(0), col(n_blocks), col(2 * n_blocks),
                  col_s(0), col_s(n_blocks), col_s(2 * n_blocks)],
        out_specs=col(0),
        out_shape=jax.ShapeDtypeStruct((bsz, s, d), BF16),
        scratch_shapes=[pltpu.VMEM((2 * NEAR, wide), F32),
                        pltpu.VMEM((NEAR, wide), F32),
                        pltpu.VMEM((HEAD_GROUPS, sub, wide), F32),
                        pltpu.VMEM((width + ONES_ROWS, s), BF16),
                        pltpu.VMEM((HEAD_GROUPS, STRIDE, LANES + ONES_ROWS, sub), BF16),
                        pltpu.VMEM((HEAD_GROUPS, s, LANES), F32),
                        pltpu.VMEM((HEAD_GROUPS, s, LANES), F32)],
        compiler_params=_params(2),
        name="dilated_attention",
    )(tv1, tv2, qkv, qkv, qkv, qkv_s, qkv_s, qkv_s)


def kernel(x, mix_norm, ffn_norm, final_norm, conv_w_in, conv_kernel, conv_w_out,
           attn_w_qkv, attn_w_out, rel_bias, ffn_w_gate, ffn_w_up, ffn_w_down):
    bsz, s, d = x.shape
    depth = mix_norm.shape[0]
    assert d == N_HEADS * HEAD_DIM and s % ROW_TILE == 0 and ROW_TILE % ATT_TILE == 0
    near = tuple((w, dil) for w, dil in BRANCHES if dil == 1)
    far = tuple((w, dil) for w, dil in BRANCHES if dil > 1)
    assert all(w <= NEAR for w, _ in near) and all(dil % STRIDE == 0 for _, dil in far)
    assert s % (STRIDE * ATT_TILE) == 0 and ROW_TILE % STRIDE == 0

    w_qkv = attn_w_qkv.astype(BF16)
    w_in = conv_w_in.astype(BF16)
    w_co = conv_w_out.astype(BF16)
    w_ao = attn_w_out.astype(BF16)
    w_g = ffn_w_gate.astype(BF16)
    w_u = ffn_w_up.astype(BF16)
    w_d = ffn_w_down.astype(BF16)
    tv1 = _bias_rows(rel_bias, NEAR + 1, 1, near, 3 * ATT_TILE)
    tv2 = _bias_rows(rel_bias, s // STRIDE, STRIDE, far, s // STRIDE + ATT_TILE)
    gf = final_norm.reshape(1, d)

    x2 = x.reshape(bsz * s, d)
    for i in range(depth):
        j = i // 2
        g_mix = mix_norm[i].reshape(1, d)
        if i % 2 == 0:
            z = _conv_pre(x2.reshape(bsz, s, d), g_mix, w_in, conv_kernel, j)
            w_out = w_co
        else:
            qkv, qkv_s = _qkv_proj(x2, g_mix, w_qkv, j, s)
            z = _attention(qkv.reshape(bsz, s, 3 * d), qkv_s, tv1, tv2)
            w_out = w_ao
        x2 = _mix_ffn(x2, z.reshape(bsz * s, d), w_out, j, ffn_norm[i].reshape(1, d),
                      w_g, w_u, w_d, i, gf, final=(i == depth - 1))
    return x2.reshape(bsz, s, d)
```

```python
import functools
import math

import jax
import jax.numpy as jnp
from jax import lax
from jax.experimental import pallas as pl
from jax.experimental.pallas import tpu as pltpu

N_HEADS = 16
HEAD_DIM = 64
CONV_WIDTH = 3
BRANCHES = ((128, 1), (512, 4), (2048, 16))
NUM_BUCKETS = 32
MAX_DISTANCE = 2048
EPS = 1e-6
NEG_INF = -1e30
LOG2E = math.log2(math.e)

LANES = 128
ROW_TILE = 1024
COL_CHUNK = 512
FF_CHUNK = 256
ATT_TILE = 256
HEAD_GROUPS = 2
STRIDE = 4
NEAR = ATT_TILE // 2
STAGE_ROWS = 128
ONES_ROWS = 16
VMEM_LIMIT = 56 * 1024 * 1024

F32 = jnp.float32
BF16 = jnp.bfloat16
NT_DIMS = (((1,), (1,)), ((), ()))


def _rms(x, g):
    ms = jnp.mean(x * x, axis=-1, keepdims=True)
    return x * lax.rsqrt(ms + EPS) * g


def _const_spec(shape):
    nd = len(shape)
    return pl.BlockSpec(shape, lambda *_: (0,) * nd, pipeline_mode=pl.Buffered(1))


def _layer_spec(stacked, layer):
    tail = (0,) * (stacked.ndim - 1)
    return pl.BlockSpec((None,) + stacked.shape[1:], lambda *_: (layer,) + tail,
                        pipeline_mode=pl.Buffered(1))


def _load_cast(src_hbm, dst_ref, stage_ref, sem):
    rows = stage_ref.shape[1]
    n_chunks = src_hbm.shape[0] // rows
    assert src_hbm.shape[0] % rows == 0 and src_hbm.shape[1] == stage_ref.shape[2]

    def chunk_copy(i, slot):
        return pltpu.make_async_copy(
            src_hbm.at[pl.ds(i * rows, rows), :], stage_ref.at[slot], sem.at[slot])

    chunk_copy(0, 0).start()

    def body(i, carry):
        slot = lax.rem(i, 2)

        @pl.when(i + 1 < n_chunks)
        def _():
            chunk_copy(i + 1, 1 - slot).start()

        chunk_copy(i, slot).wait()
        dst_ref[pl.ds(pl.multiple_of(i * rows, rows), rows), :] = stage_ref[slot].astype(BF16)
        return carry

    lax.fori_loop(0, n_chunks, body, 0)


def _params(n_axes):
    return pltpu.CompilerParams(
        dimension_semantics=("arbitrary",) * n_axes, vmem_limit_bytes=VMEM_LIMIT)


def _qkv_body(x_ref, g_ref, w_ref, o_ref, os_ref, slab0_ref, slab1_ref):
    tm, d = x_ref.shape
    h = _rms(x_ref[...], g_ref[...]).astype(BF16)
    q_scale = LOG2E * HEAD_DIM ** -0.5
    for j in range(0, w_ref.shape[1], COL_CHUNK):
        slab_ref = (slab0_ref, slab1_ref)[(j // COL_CHUNK) % 2]
        y = jnp.dot(h, w_ref[:, j:j + COL_CHUNK], preferred_element_type=F32)
        if j < d:
            y = y * q_scale
        o_ref[:, j:j + COL_CHUNK] = y.astype(BF16)
        for c in range(COL_CHUNK // LANES):
            slab_ref[c] = y[:, c * LANES:(c + 1) * LANES]
        for r in range(STRIDE):
            for c in range(COL_CHUNK // LANES):
                part = slab_ref[c, pl.ds(r, tm // STRIDE, stride=STRIDE), :]
                os_ref[r, :, j + c * LANES:j + (c + 1) * LANES] = part.astype(BF16)


def _qkv_proj(x2, g, w, layer, seq):
    t, d = x2.shape
    n = w.shape[2]
    tm = ROW_TILE
    per_seq = seq // tm
    return pl.pallas_call(
        _qkv_body,
        grid=(t // tm,),
        in_specs=[pl.BlockSpec((tm, d), lambda i: (i, 0)),
                  _const_spec((1, d)), _layer_spec(w, layer)],
        out_specs=[pl.BlockSpec((tm, n), lambda i: (i, 0)),
                   pl.BlockSpec((None, STRIDE, tm // STRIDE, n),
                                lambda i: (i // per_seq, 0, i % per_seq, 0))],
        out_shape=[jax.ShapeDtypeStruct((t, n), BF16),
                   jax.ShapeDtypeStruct((t // seq, STRIDE, seq // STRIDE, n), BF16)],
        scratch_shapes=[pltpu.VMEM((COL_CHUNK // LANES, tm, LANES), F32)] * 2,
        compiler_params=_params(1),
        name="qkv_proj",
    )(x2, g, w)


def _conv_body(x_ref, g_ref, w_ref, k_ref, z_ref, cu_ref):
    tm, d = x_ref.shape
    first = pl.program_id(1) == 0

    @pl.when(first)
    def _():
        cu_ref[0:8, :] = jnp.zeros((8, d), F32)

    @pl.when(jnp.logical_not(first))
    def _():
        cu_ref[0:8, :] = cu_ref[tm:tm + 8, :]

    h = _rms(x_ref[...], g_ref[...]).astype(BF16)
    for j in range(0, d, COL_CHUNK):
        cols = slice(j, j + COL_CHUNK)
        c = jnp.dot(h, w_ref[:, d + j:d + j + COL_CHUNK], preferred_element_type=F32)
        u = jnp.dot(h, w_ref[:, 2 * d + j:2 * d + j + COL_CHUNK], preferred_element_type=F32)
        cu_ref[8:8 + tm, cols] = c * u
        y = (k_ref[2:3, cols] * cu_ref[8:8 + tm, cols]
             + k_ref[1:2, cols] * cu_ref[7:7 + tm, cols]
             + k_ref[0:1, cols] * cu_ref[6:6 + tm, cols])
        b = jnp.dot(h, w_ref[:, cols], preferred_element_type=F32)
        z_ref[:, cols] = (b * y).astype(BF16)


def _conv_pre(x, g, w_in, taps, layer):
    bsz, s, d = x.shape
    tm = ROW_TILE
    return pl.pallas_call(
        _conv_body,
        grid=(bsz, s // tm),
        in_specs=[pl.BlockSpec((None, tm, d), lambda b, i: (b, i, 0)),
                  _const_spec((1, d)), _layer_spec(w_in, layer), _layer_spec(taps, layer)],
        out_specs=pl.BlockSpec((None, tm, d), lambda b, i: (b, i, 0)),
        out_shape=jax.ShapeDtypeStruct((bsz, s, d), BF16),
        scratch_shapes=[pltpu.VMEM((tm + 8, d), F32)],
        compiler_params=_params(2),
        name="conv_pre",
    )(x, g, w_in, taps)


def _ffn_body(x_ref, z_ref, wo_hbm, g_ref, wg_hbm, wu_hbm, wd_hbm, gf_ref, o_ref,
              wo_ref, wg_ref, wu_ref, wd_ref, wide_stage, narrow_stage, sem, *,
              final, mix_layer, layer):
    @pl.when(pl.program_id(0) == 0)
    def _():
        _load_cast(wo_hbm.at[mix_layer], wo_ref, narrow_stage, sem)
        _load_cast(wg_hbm.at[layer], wg_ref, wide_stage, sem)
        _load_cast(wu_hbm.at[layer], wu_ref, wide_stage, sem)
        _load_cast(wd_hbm.at[layer], wd_ref, narrow_stage, sem)

    x1 = x_ref[...] + jnp.dot(z_ref[...], wo_ref[...], preferred_element_type=F32)
    h = _rms(x1, g_ref[...]).astype(BF16)
    o_ref[...] = x1
    for c in range(0, wg_ref.shape[1], FF_CHUNK):
        gate = jnp.dot(h, wg_ref[:, c:c + FF_CHUNK], preferred_element_type=F32)
        up = jnp.dot(h, wu_ref[:, c:c + FF_CHUNK], preferred_element_type=F32)
        a = (gate * jax.nn.sigmoid(gate) * up).astype(BF16)
        o_ref[...] += jnp.dot(a, wd_ref[c:c + FF_CHUNK, :], preferred_element_type=F32)
    if final:
        o_ref[...] = _rms(o_ref[...], gf_ref[...])


def _mix_ffn(x2, z2, wo, mix_layer, g, wg, wu, wd, layer, gf, final):
    t, d = x2.shape
    dff = wg.shape[2]
    row = lambda i: (i, 0)
    hbm = pl.BlockSpec(memory_space=pl.ANY)
    return pl.pallas_call(
        functools.partial(_ffn_body, final=final, mix_layer=mix_layer, layer=layer),
        grid=(t // ROW_TILE,),
        in_specs=[pl.BlockSpec((ROW_TILE, d), row), pl.BlockSpec((ROW_TILE, d), row),
                  hbm, _const_spec((1, d)), hbm, hbm, hbm, _const_spec((1, d))],
        out_specs=pl.BlockSpec((ROW_TILE, d), row),
        out_shape=jax.ShapeDtypeStruct((t, d), F32),
        scratch_shapes=[pltpu.VMEM((d, d), BF16), pltpu.VMEM((d, dff), BF16),
                        pltpu.VMEM((d, dff), BF16), pltpu.VMEM((dff, d), BF16),
                        pltpu.VMEM((2, STAGE_ROWS, dff), F32),
                        pltpu.VMEM((2, STAGE_ROWS, d), F32),
                        pltpu.SemaphoreType.DMA((2,))],
        compiler_params=_params(1),
        name="mix_ffn_final" if final else "mix_ffn",
    )(x2, z2, wo, g, wg, wu, wd, gf)


def _t5_bucket(dist):
    exact = NUM_BUCKETS // 2
    df = jnp.maximum(dist, 1).astype(F32)
    large = exact + (jnp.log(df / exact) / math.log(MAX_DISTANCE / exact)
                     * (NUM_BUCKETS - exact)).astype(jnp.int32)
    large = jnp.minimum(large, NUM_BUCKETS - 1)
    return jnp.where(dist < exact, dist, large)


def _bias_rows(rel_bias, n_dist, stride, branches, period):
    dist = jnp.arange(n_dist, dtype=jnp.int32) * stride
    mult = jnp.zeros((n_dist,), F32)
    for window, dilation in branches:
        mult = mult + ((dist % dilation == 0) & (dist <= window)).astype(F32)
    table = rel_bias[_t5_bucket(dist)].astype(F32)
    total = jnp.where(mult[:, None] > 0,
                      (table + jnp.log(jnp.maximum(mult, 1.0))[:, None]) * LOG2E, NEG_INF)
    pad = jnp.full((period - n_dist, total.shape[1]), NEG_INF, F32)
    return jnp.concatenate([total, pad], axis=0).T[:, None, :]


def _attn_body(tv1_ref, tv2_ref, q_ref, k_ref, v_ref, qs_ref, ks_ref, vs_ref, o_ref,
               bias1_ref, bias10_ref, bias2_ref, lhs_ref, lhss_ref, o2_ref, lse2_ref):
    tq = ATT_TILE
    seq, width = k_ref.shape
    sub = ks_ref.shape[1]
    n_sub = sub // tq
    groups = width // LANES
    heads = width // HEAD_DIM

    @pl.when(pl.program_id(1) == 0)
    def _():
        for h in range(heads):
            row = jnp.broadcast_to(tv1_ref[h], (2 * NEAR, tv1_ref.shape[2]))
            toep = pltpu.roll(row, 0, 1, stride=1, stride_axis=0)
            bias10_ref[:, h * NEAR:(h + 1) * NEAR] = toep[:NEAR, :NEAR]
            bias1_ref[:, h * NEAR:(h + 1) * NEAR] = toep[:, NEAR:2 * NEAR]
            g, cols = h // 2, slice((h % 2) * tq, (h % 2 + 1) * tq)
            row = jnp.broadcast_to(tv2_ref[h], (tq, tv2_ref.shape[2]))
            toep = pltpu.roll(row, 0, 1, stride=1, stride_axis=0)
            for dist in range(n_sub):
                start = (n_sub - 1 - dist) * tq
                bias2_ref[g, start:start + tq, cols] = toep[:, dist * tq:(dist + 1) * tq]

    for kb in range(seq // tq):
        lhs_ref[:width, kb * tq:(kb + 1) * tq] = v_ref[kb * tq:(kb + 1) * tq, :].T
    lhs_ref[width:, :] = jnp.ones((ONES_ROWS, seq), BF16)
    for g in range(groups):
        lanes = slice(g * LANES, (g + 1) * LANES)
        for r in range(STRIDE):
            for kb in range(n_sub):
                lhss_ref[g, r, :LANES, kb * tq:(kb + 1) * tq] = (
                    vs_ref[r, kb * tq:(kb + 1) * tq, lanes].T)
            lhss_ref[g, r, LANES:, :] = jnp.ones((ONES_ROWS, sub), BF16)

    lane_head = lax.broadcasted_iota(jnp.int32, (1, width), 1) // HEAD_DIM
    items = [(g, r, j, 0, (j + 1) * tq)
             for g in range(groups) for r in range(STRIDE) for j in range(n_sub)]
    items += [(None, None, qb, max(0, (qb - 1) * NEAR), (qb + 1) * NEAR)
              for qb in range(seq // NEAR)]
    col_max, scores, probs = {}, {}, {}

    def per_head_rows(q):
        zero = jnp.zeros_like(q)
        lanes = lane_head[:, :q.shape[1]]
        return jnp.concatenate(
            [jnp.where(lanes == h, q, zero) for h in range(q.shape[1] // HEAD_DIM)], axis=0)

    def stage_a(t):
        g, r, qb, lo, hi = items[t]
        if g is None:
            q = per_head_rows(q_ref[qb * NEAR:(qb + 1) * NEAR, :])
            k = k_ref[lo:hi, :]
            bias = bias10_ref[...] if lo == qb * NEAR else bias1_ref[...]
        else:
            lanes = slice(g * LANES, (g + 1) * LANES)
            q = per_head_rows(qs_ref[r, qb * tq:(qb + 1) * tq, lanes])
            k = ks_ref[r, lo:hi, lanes]
            bias = bias2_ref[g, sub - hi:, :]
        s = lax.dot_general(k, q, NT_DIMS, preferred_element_type=F32) + bias
        scores[t] = s
        col_max[t] = jnp.max(s, axis=0, keepdims=True)

    def stage_b(t):
        probs[t] = jnp.exp2(scores.pop(t) - col_max[t]).astype(BF16)

    def stage_c(t):
        g, r, qb, lo, hi = items[t]
        lhs = lhs_ref[:, lo:hi] if g is None else lhss_ref[g, r, :, lo:hi]
        acc = jnp.dot(lhs, probs.pop(t), preferred_element_type=F32)
        n_heads = (lhs.shape[0] - ONES_ROWS) // HEAD_DIM
        nq = acc.shape[1] // n_heads
        den = acc[n_heads * HEAD_DIM:n_heads * HEAD_DIM + 1]
        lse = col_max.pop(t) + jnp.log2(den)
        out_t = jnp.concatenate(
            [acc[h * HEAD_DIM:(h + 1) * HEAD_DIM, h * nq:(h + 1) * nq] / den[:, h * nq:(h + 1) * nq]
             for h in range(n_heads)], axis=0)
        lse_t = jnp.concatenate(
            [jnp.broadcast_to(lse[:, h * nq:(h + 1) * nq], (HEAD_DIM, nq)) for h in range(n_heads)],
            axis=0)
        out, lse = out_t.T, lse_t.T
        if g is not None:
            token_rows = pl.ds(qb * tq * STRIDE + r, tq, stride=STRIDE)
            o2_ref[g, token_rows, :] = out
            lse2_ref[g, token_rows, :] = lse
        else:
            rows = slice(qb * NEAR, (qb + 1) * NEAR)
            out2 = jnp.concatenate([o2_ref[i, rows, :] for i in range(groups)], axis=1)
            lse2 = jnp.concatenate([lse2_ref[i, rows, :] for i in range(groups)], axis=1)
            top = jnp.maximum(lse, lse2)
            w1, w2 = jnp.exp2(lse - top), jnp.exp2(lse2 - top)
            o_ref[rows, :] = ((w1 * out + w2 * out2) / (w1 + w2)).astype(BF16)

    for t in range(len(items) + 2):
        if t < len(items):
            stage_a(t)
        if 1 <= t <= len(items):
            stage_b(t - 1)
        if t >= 2:
            stage_c(t - 2)


def _attention(qkv, qkv_s, tv1, tv2):
    bsz, s, d3 = qkv.shape
    d = d3 // 3
    width = HEAD_GROUPS * LANES
    heads = width // HEAD_DIM
    n_blocks = d // width
    sub = s // STRIDE
    col = lambda off: pl.BlockSpec((None, s, width), lambda hb, b: (b, 0, off + hb))
    col_s = lambda off: pl.BlockSpec((None, STRIDE, sub, width),
                                     lambda hb, b: (b, 0, 0, off + hb))
    bias_row = lambda tv: pl.BlockSpec((heads, 1, tv.shape[2]), lambda hb, b: (hb, 0, 0))
    wide = 2 * ATT_TILE
    assert heads * NEAR == wide
    return pl.pallas_call(
        _attn_body,
        grid=(n_blocks, bsz),
        in_specs=[bias_row(tv1), bias_row(tv2),
                  col(0), col(n_blocks), col(2 * n_blocks),
                  col_s(0), col_s(n_blocks), col_s(2 * n_blocks)],
        out_specs=col(0),
        out_shape=jax.ShapeDtypeStruct((bsz, s, d), BF16),
        scratch_shapes=[pltpu.VMEM((2 * NEAR, wide), F32),
                        pltpu.VMEM((NEAR, wide), F32),
                        pltpu.VMEM((HEAD_GROUPS, sub, wide), F32),
                        pltpu.VMEM((width + ONES_ROWS, s), BF16),
                        pltpu.VMEM((HEAD_GROUPS, STRIDE, LANES + ONES_ROWS, sub), BF16),
                        pltpu.VMEM((HEAD_GROUPS, s, LANES), F32),
                        pltpu.VMEM((HEAD_GROUPS, s, LANES), F32)],
        compiler_params=_params(2),
        name="dilated_attention",
    )(tv1, tv2, qkv, qkv, qkv, qkv_s, qkv_s, qkv_s)


def kernel(x, mix_norm, ffn_norm, final_norm, conv_w_in, conv_kernel, conv_w_out,
           attn_w_qkv, attn_w_out, rel_bias, ffn_w_gate, ffn_w_up, ffn_w_down):
    bsz, s, d = x.shape
    depth = mix_norm.shape[0]
    assert d == N_HEADS * HEAD_DIM and s % ROW_TILE == 0 and ROW_TILE % ATT_TILE == 0
    near = tuple((w, dil) for w, dil in BRANCHES if dil == 1)
    far = tuple((w, dil) for w, dil in BRANCHES if dil > 1)
    assert all(w <= NEAR for w, _ in near) and all(dil % STRIDE == 0 for _, dil in far)
    assert s % (STRIDE * ATT_TILE) == 0 and ROW_TILE % STRIDE == 0

    w_qkv = attn_w_qkv.astype(BF16)
    w_in = conv_w_in.astype(BF16)
    tv1 = _bias_rows(rel_bias, NEAR + 1, 1, near, 3 * ATT_TILE)
    tv2 = _bias_rows(rel_bias, s // STRIDE, STRIDE, far, s // STRIDE + ATT_TILE)
    gf = final_norm.reshape(1, d)

    x2 = x.reshape(bsz * s, d)
    for i in range(depth):
        j = i // 2
        g_mix = mix_norm[i].reshape(1, d)
        if i % 2 == 0:
            z = _conv_pre(x2.reshape(bsz, s, d), g_mix, w_in, conv_kernel, j)
            w_out = conv_w_out
        else:
            qkv, qkv_s = _qkv_proj(x2, g_mix, w_qkv, j, s)
            z = _attention(qkv.reshape(bsz, s, 3 * d), qkv_s, tv1, tv2)
            w_out = attn_w_out
        x2 = _mix_ffn(x2, z.reshape(bsz * s, d), w_out, j, ffn_norm[i].reshape(1, d),
                      ffn_w_gate, ffn_w_up, ffn_w_down, i, gf, final=(i == depth - 1))
    return x2.reshape(bsz, s, d)
```

```python
import functools
import math

import jax
import jax.numpy as jnp
from jax import lax
from jax.experimental import pallas as pl
from jax.experimental.pallas import tpu as pltpu

N_HEADS = 16
HEAD_DIM = 64
CONV_WIDTH = 3
BRANCHES = ((128, 1), (512, 4), (2048, 16))
NUM_BUCKETS = 32
MAX_DISTANCE = 2048
EPS = 1e-6
NEG_INF = -1e30
LOG2E = math.log2(math.e)

LANES = 128
ROW_TILE = 1024
COL_CHUNK = 512
FF_CHUNK = 256
ATT_TILE = 256
HEAD_GROUPS = 2
STRIDE = 4
NEAR = ATT_TILE // 2
BF16_SUBLANES = 16
ONES_ROWS = 16
VMEM_LIMIT = 56 * 1024 * 1024

F32 = jnp.float32
BF16 = jnp.bfloat16
NT_DIMS = (((1,), (1,)), ((), ()))


def _rms(x, g):
    ms = jnp.mean(x * x, axis=-1, keepdims=True)
    return x * lax.rsqrt(ms + EPS) * g


def _const_spec(shape):
    nd = len(shape)
    return pl.BlockSpec(shape, lambda *_: (0,) * nd, pipeline_mode=pl.Buffered(1))


def _layer_spec(stacked, layer):
    tail = (0,) * (stacked.ndim - 1)
    return pl.BlockSpec((None,) + stacked.shape[1:], lambda *_: (layer,) + tail,
                        pipeline_mode=pl.Buffered(1))


def _cast_io(jobs, steps):
    in_specs, out_specs, out_shapes = [], [], []
    for w, layer in jobs:
        rows, cols = w.shape[1] // steps, w.shape[2]
        assert w.shape[1] % steps == 0 and rows % BF16_SUBLANES == 0
        in_specs.append(pl.BlockSpec((None, rows, cols), lambda i, layer=layer: (layer, i, 0)))
        out_specs.append(pl.BlockSpec((rows, cols), lambda i: (i, 0)))
        out_shapes.append(jax.ShapeDtypeStruct(w.shape[1:], BF16))
    return in_specs, out_specs, out_shapes


def _split_refs(refs, n_in, n_out, n_cast):
    a, b, c = n_in + n_cast, n_in + n_cast + n_out, n_in + 2 * n_cast + n_out
    return refs[:n_in], refs[n_in:a], refs[a:b], refs[b:c], refs[c:]


def _run_casts(srcs, dsts):
    for src, dst in zip(srcs, dsts):
        dst[...] = src[...].astype(BF16)


def _params(n_axes):
    return pltpu.CompilerParams(
        dimension_semantics=("arbitrary",) * n_axes, vmem_limit_bytes=VMEM_LIMIT)


def _qkv_body(x_ref, g_ref, w_ref, o_ref, os_ref, slab0_ref, slab1_ref):
    tm, d = x_ref.shape
    h = _rms(x_ref[...], g_ref[...]).astype(BF16)
    q_scale = LOG2E * HEAD_DIM ** -0.5
    for j in range(0, w_ref.shape[1], COL_CHUNK):
        slab_ref = (slab0_ref, slab1_ref)[(j // COL_CHUNK) % 2]
        y = jnp.dot(h, w_ref[:, j:j + COL_CHUNK], preferred_element_type=F32)
        if j < d:
            y = y * q_scale
        o_ref[:, j:j + COL_CHUNK] = y.astype(BF16)
        for c in range(COL_CHUNK // LANES):
            slab_ref[c] = y[:, c * LANES:(c + 1) * LANES]
        for r in range(STRIDE):
            for c in range(COL_CHUNK // LANES):
                part = slab_ref[c, pl.ds(r, tm // STRIDE, stride=STRIDE), :]
                os_ref[r, :, j + c * LANES:j + (c + 1) * LANES] = part.astype(BF16)


def _qkv_proj(x2, g, w, seq):
    t, d = x2.shape
    n = w.shape[1]
    tm = ROW_TILE
    per_seq = seq // tm
    return pl.pallas_call(
        _qkv_body,
        grid=(t // tm,),
        in_specs=[pl.BlockSpec((tm, d), lambda i: (i, 0)),
                  _const_spec((1, d)), _const_spec(w.shape)],
        out_specs=[pl.BlockSpec((tm, n), lambda i: (i, 0)),
                   pl.BlockSpec((None, STRIDE, tm // STRIDE, n),
                                lambda i: (i // per_seq, 0, i % per_seq, 0))],
        out_shape=[jax.ShapeDtypeStruct((t, n), BF16),
                   jax.ShapeDtypeStruct((t // seq, STRIDE, seq // STRIDE, n), BF16)],
        scratch_shapes=[pltpu.VMEM((COL_CHUNK // LANES, tm, LANES), F32)] * 2,
        compiler_params=_params(1),
        name="qkv_proj",
    )(x2, g, w)


def _conv_body(*refs, n_cast, per_seq):
    (x_ref, g_ref, w_ref, k_ref), cast_src, (z_ref,), cast_dst, (cu_ref,) = _split_refs(
        refs, 4, 1, n_cast)
    _run_casts(cast_src, cast_dst)
    tm, d = x_ref.shape
    first = pl.program_id(0) % per_seq == 0

    @pl.when(first)
    def _():
        cu_ref[0:8, :] = jnp.zeros((8, d), F32)

    @pl.when(jnp.logical_not(first))
    def _():
        cu_ref[0:8, :] = cu_ref[tm:tm + 8, :]

    h = _rms(x_ref[...], g_ref[...]).astype(BF16)
    for j in range(0, d, COL_CHUNK):
        cols = slice(j, j + COL_CHUNK)
        c = jnp.dot(h, w_ref[:, d + j:d + j + COL_CHUNK], preferred_element_type=F32)
        u = jnp.dot(h, w_ref[:, 2 * d + j:2 * d + j + COL_CHUNK], preferred_element_type=F32)
        cu_ref[8:8 + tm, cols] = c * u
        y = (k_ref[2:3, cols] * cu_ref[8:8 + tm, cols]
             + k_ref[1:2, cols] * cu_ref[7:7 + tm, cols]
             + k_ref[0:1, cols] * cu_ref[6:6 + tm, cols])
        b = jnp.dot(h, w_ref[:, cols], preferred_element_type=F32)
        z_ref[:, cols] = (b * y).astype(BF16)


def _conv_pre(x2, g, w_in, taps, layer, seq, cast_jobs):
    t, d = x2.shape
    tm = ROW_TILE
    steps = t // tm
    cast_in, cast_out, cast_shapes = _cast_io(cast_jobs, steps)
    row = pl.BlockSpec((tm, d), lambda i: (i, 0))
    outs = pl.pallas_call(
        functools.partial(_conv_body, n_cast=len(cast_jobs), per_seq=seq // tm),
        grid=(steps,),
        in_specs=[row, _const_spec((1, d)), _const_spec(w_in.shape), _layer_spec(taps, layer)]
        + cast_in,
        out_specs=[row] + cast_out,
        out_shape=[jax.ShapeDtypeStruct((t, d), BF16)] + cast_shapes,
        scratch_shapes=[pltpu.VMEM((tm + 8, d), F32)],
        compiler_params=_params(1),
        name="conv_pre",
    )(x2, g, w_in, taps, *[w for w, _ in cast_jobs])
    return outs[0], outs[1:]


def _ffn_body(*refs, final, n_cast):
    ins, cast_src, (o_ref,), cast_dst, _ = _split_refs(refs, 8, 1, n_cast)
    x_ref, z_ref, wo_ref, g_ref, wg_ref, wu_ref, wd_ref, gf_ref = ins
    _run_casts(cast_src, cast_dst)
    x1 = x_ref[...] + jnp.dot(z_ref[...], wo_ref[...], preferred_element_type=F32)
    h = _rms(x1, g_ref[...]).astype(BF16)
    o_ref[...] = x1
    for c in range(0, wg_ref.shape[1], FF_CHUNK):
        gate = jnp.dot(h, wg_ref[:, c:c + FF_CHUNK], preferred_element_type=F32)
        up = jnp.dot(h, wu_ref[:, c:c + FF_CHUNK], preferred_element_type=F32)
        a = (gate * jax.nn.sigmoid(gate) * up).astype(BF16)
        o_ref[...] += jnp.dot(a, wd_ref[c:c + FF_CHUNK, :], preferred_element_type=F32)
    if final:
        o_ref[...] = _rms(o_ref[...], gf_ref[...])


def _mix_ffn(x2, z2, wo, g, wg, wu, wd, gf, final, cast_jobs):
    t, d = x2.shape
    steps = t // ROW_TILE
    cast_in, cast_out, cast_shapes = _cast_io(cast_jobs, steps)
    row = pl.BlockSpec((ROW_TILE, d), lambda i: (i, 0))
    outs = pl.pallas_call(
        functools.partial(_ffn_body, final=final, n_cast=len(cast_jobs)),
        grid=(steps,),
        in_specs=[row, row, _const_spec(wo.shape), _const_spec((1, d)), _const_spec(wg.shape),
                  _const_spec(wu.shape), _const_spec(wd.shape), _const_spec((1, d))] + cast_in,
        out_specs=[row] + cast_out,
        out_shape=[jax.ShapeDtypeStruct((t, d), F32)] + cast_shapes,
        compiler_params=_params(1),
        name="mix_ffn_final" if final else "mix_ffn",
    )(x2, z2, wo, g, wg, wu, wd, gf, *[w for w, _ in cast_jobs])
    return outs[0], outs[1:]


def _t5_bucket(dist):
    exact = NUM_BUCKETS // 2
    df = jnp.maximum(dist, 1).astype(F32)
    large = exact + (jnp.log(df / exact) / math.log(MAX_DISTANCE / exact)
                     * (NUM_BUCKETS - exact)).astype(jnp.int32)
    large = jnp.minimum(large, NUM_BUCKETS - 1)
    return jnp.where(dist < exact, dist, large)


def _bias_rows(rel_bias, n_dist, stride, branches, period):
    dist = jnp.arange(n_dist, dtype=jnp.int32) * stride
    mult = jnp.zeros((n_dist,), F32)
    for window, dilation in branches:
        mult = mult + ((dist % dilation == 0) & (dist <= window)).astype(F32)
    table = rel_bias[_t5_bucket(dist)].astype(F32)
    total = jnp.where(mult[:, None] > 0,
                      (table + jnp.log(jnp.maximum(mult, 1.0))[:, None]) * LOG2E, NEG_INF)
    pad = jnp.full((period - n_dist, total.shape[1]), NEG_INF, F32)
    return jnp.concatenate([total, pad], axis=0).T[:, None, :]


def _attn_body(tv1_ref, tv2_ref, q_ref, k_ref, v_ref, qs_ref, ks_ref, vs_ref, o_ref,
               bias1_ref, bias10_ref, bias2_ref, lhs_ref, lhss_ref, o2_ref, lse2_ref):
    tq = ATT_TILE
    seq, width = k_ref.shape
    sub = ks_ref.shape[1]
    n_sub = sub // tq
    groups = width // LANES
    heads = width // HEAD_DIM

    @pl.when(pl.program_id(1) == 0)
    def _():
        for h in range(heads):
            row = jnp.broadcast_to(tv1_ref[h], (2 * NEAR, tv1_ref.shape[2]))
            toep = pltpu.roll(row, 0, 1, stride=1, stride_axis=0)
            bias10_ref[:, h * NEAR:(h + 1) * NEAR] = toep[:NEAR, :NEAR]
            bias1_ref[:, h * NEAR:(h + 1) * NEAR] = toep[:, NEAR:2 * NEAR]
            g, cols = h // 2, slice((h % 2) * tq, (h % 2 + 1) * tq)
            row = jnp.broadcast_to(tv2_ref[h], (tq, tv2_ref.shape[2]))
            toep = pltpu.roll(row, 0, 1, stride=1, stride_axis=0)
            for dist in range(n_sub):
                start = (n_sub - 1 - dist) * tq
                bias2_ref[g, start:start + tq, cols] = toep[:, dist * tq:(dist + 1) * tq]

    for kb in range(seq // tq):
        lhs_ref[:width, kb * tq:(kb + 1) * tq] = v_ref[kb * tq:(kb + 1) * tq, :].T
    lhs_ref[width:, :] = jnp.ones((ONES_ROWS, seq), BF16)
    for g in range(groups):
        lanes = slice(g * LANES, (g + 1) * LANES)
        for r in range(STRIDE):
            for kb in range(n_sub):
                lhss_ref[g, r, :LANES, kb * tq:(kb + 1) * tq] = (
                    vs_ref[r, kb * tq:(kb + 1) * tq, lanes].T)
            lhss_ref[g, r, LANES:, :] = jnp.ones((ONES_ROWS, sub), BF16)

    lane_head = lax.broadcasted_iota(jnp.int32, (1, width), 1) // HEAD_DIM
    items = [(g, r, j, 0, (j + 1) * tq)
             for g in range(groups) for r in range(STRIDE) for j in range(n_sub)]
    items += [(None, None, qb, max(0, (qb - 1) * NEAR), (qb + 1) * NEAR)
              for qb in range(seq // NEAR)]
    col_max, scores, probs = {}, {}, {}

    def per_head_rows(q):
        zero = jnp.zeros_like(q)
        lanes = lane_head[:, :q.shape[1]]
        return jnp.concatenate(
            [jnp.where(lanes == h, q, zero) for h in range(q.shape[1] // HEAD_DIM)], axis=0)

    def stage_a(t):
        g, r, qb, lo, hi = items[t]
        if g is None:
            q = per_head_rows(q_ref[qb * NEAR:(qb + 1) * NEAR, :])
            k = k_ref[lo:hi, :]
            bias = bias10_ref[...] if lo == qb * NEAR else bias1_ref[...]
        else:
            lanes = slice(g * LANES, (g + 1) * LANES)
            q = per_head_rows(qs_ref[r, qb * tq:(qb + 1) * tq, lanes])
            k = ks_ref[r, lo:hi, lanes]
            bias = bias2_ref[g, sub - hi:, :]
        s = lax.dot_general(k, q, NT_DIMS, preferred_element_type=F32) + bias
        scores[t] = s
        col_max[t] = jnp.max(s, axis=0, keepdims=True)

    def stage_b(t):
        probs[t] = jnp.exp2(scores.pop(t) - col_max[t]).astype(BF16)

    def stage_c(t):
        g, r, qb, lo, hi = items[t]
        lhs = lhs_ref[:, lo:hi] if g is None else lhss_ref[g, r, :, lo:hi]
        acc = jnp.dot(lhs, probs.pop(t), preferred_element_type=F32)
        n_heads = (lhs.shape[0] - ONES_ROWS) // HEAD_DIM
        nq = acc.shape[1] // n_heads
        den = acc[n_heads * HEAD_DIM:n_heads * HEAD_DIM + 1]
        lse = col_max.pop(t) + jnp.log2(den)
        out_t = jnp.concatenate(
            [acc[h * HEAD_DIM:(h + 1) * HEAD_DIM, h * nq:(h + 1) * nq] / den[:, h * nq:(h + 1) * nq]
             for h in range(n_heads)], axis=0)
        lse_t = jnp.concatenate(
            [jnp.broadcast_to(lse[:, h * nq:(h + 1) * nq], (HEAD_DIM, nq)) for h in range(n_heads)],
            axis=0)
        out, lse = out_t.T, lse_t.T
        if g is not None:
            token_rows = pl.ds(qb * tq * STRIDE + r, tq, stride=STRIDE)
            o2_ref[g, token_rows, :] = out
            lse2_ref[g, token_rows, :] = lse
        else:
            rows = slice(qb * NEAR, (qb + 1) * NEAR)
            out2 = jnp.concatenate([o2_ref[i, rows, :] for i in range(groups)], axis=1)
            lse2 = jnp.concatenate([lse2_ref[i, rows, :] for i in range(groups)], axis=1)
            top = jnp.maximum(lse, lse2)
            w1, w2 = jnp.exp2(lse - top), jnp.exp2(lse2 - top)
            o_ref[rows, :] = ((w1 * out + w2 * out2) / (w1 + w2)).astype(BF16)

    for t in range(len(items) + 2):
        if t < len(items):
            stage_a(t)
        if 1 <= t <= len(items):
            stage_b(t - 1)
        if t >= 2:
            stage_c(t - 2)


def _attention(qkv, qkv_s, tv1, tv2):
    bsz, s, d3 = qkv.shape
    d = d3 // 3
    width = HEAD_GROUPS * LANES
    heads = width // HEAD_DIM
    n_blocks = d // width
    sub = s // STRIDE
    col = lambda off: pl.BlockSpec((None, s, width), lambda hb, b: (b, 0, off + hb))
    col_s = lambda off: pl.BlockSpec((None, STRIDE, sub, width),
                                     lambda hb, b: (b, 0, 0, off + hb))
    bias_row = lambda tv: pl.BlockSpec((heads, 1, tv.shape[2]), lambda hb, b: (hb, 0, 0))
    wide = 2 * ATT_TILE
    assert heads * NEAR == wide
    return pl.pallas_call(
        _attn_body,
        grid=(n_blocks, bsz),
        in_specs=[bias_row(tv1), bias_row(tv2),
                  col(0), col(n_blocks), col(2 * n_blocks),
                  col_s(0), col_s(n_blocks), col_s(2 * n_blocks)],
        out_specs=col(0),
        out_shape=jax.ShapeDtypeStruct((bsz, s, d), BF16),
        scratch_shapes=[pltpu.VMEM((2 * NEAR, wide), F32),
                        pltpu.VMEM((NEAR, wide), F32),
                        pltpu.VMEM((HEAD_GROUPS, sub, wide), F32),
                        pltpu.VMEM((width + ONES_ROWS, s), BF16),
                        pltpu.VMEM((HEAD_GROUPS, STRIDE, LANES + ONES_ROWS, sub), BF16),
                        pltpu.VMEM((HEAD_GROUPS, s, LANES), F32),
                        pltpu.VMEM((HEAD_GROUPS, s, LANES), F32)],
        compiler_params=_params(2),
        name="dilated_attention",
    )(tv1, tv2, qkv, qkv, qkv, qkv_s, qkv_s, qkv_s)


def kernel(x, mix_norm, ffn_norm, final_norm, conv_w_in, conv_kernel, conv_w_out,
           attn_w_qkv, attn_w_out, rel_bias, ffn_w_gate, ffn_w_up, ffn_w_down):
    bsz, s, d = x.shape
    depth = mix_norm.shape[0]
    assert d == N_HEADS * HEAD_DIM and s % ROW_TILE == 0 and ROW_TILE % ATT_TILE == 0
    near = tuple((w, dil) for w, dil in BRANCHES if dil == 1)
    far = tuple((w, dil) for w, dil in BRANCHES if dil > 1)
    assert all(w <= NEAR for w, _ in near) and all(dil % STRIDE == 0 for _, dil in far)
    assert s % (STRIDE * ATT_TILE) == 0 and ROW_TILE % STRIDE == 0

    tv1 = _bias_rows(rel_bias, NEAR + 1, 1, near, 3 * ATT_TILE)
    tv2 = _bias_rows(rel_bias, s // STRIDE, STRIDE, far, s // STRIDE + ATT_TILE)
    gf = final_norm.reshape(1, d)

    def layer_params(i):
        mix_in, mix_out = (conv_w_in, conv_w_out) if i % 2 == 0 else (attn_w_qkv, attn_w_out)
        return [(mix_in, i // 2), (mix_out, i // 2),
                (ffn_w_gate, i), (ffn_w_up, i), (ffn_w_down, i)]

    mix_in = conv_w_in[0].astype(BF16)
    rest = None
    x2 = x.reshape(bsz * s, d)
    for i in range(depth):
        g_mix = mix_norm[i].reshape(1, d)
        if i % 2 == 0:
            jobs = layer_params(i)[1:] if i == 0 else []
            z, made = _conv_pre(x2, g_mix, mix_in, conv_kernel, i // 2, s, jobs)
            rest = made if i == 0 else rest
        else:
            qkv, qkv_s = _qkv_proj(x2, g_mix, mix_in, s)
            z = _attention(qkv.reshape(bsz, s, 3 * d), qkv_s, tv1, tv2).reshape(bsz * s, d)
        mix_out, w_g, w_u, w_d = rest
        jobs = layer_params(i + 1) if i + 1 < depth else []
        x2, made = _mix_ffn(x2, z, mix_out, ffn_norm[i].reshape(1, d), w_g, w_u, w_d, gf,
                            final=(i == depth - 1), cast_jobs=jobs)
        if made:
            mix_in, rest = made[0], made[1:]
    return x2.reshape(bsz, s, d)
```

```python
import functools
import math

import jax
import jax.numpy as jnp
from jax import lax
from jax.experimental import pallas as pl
from jax.experimental.pallas import tpu as pltpu

N_HEADS = 16
HEAD_DIM = 64
CONV_WIDTH = 3
BRANCHES = ((128, 1), (512, 4), (2048, 16))
NUM_BUCKETS = 32
MAX_DISTANCE = 2048
EPS = 1e-6
NEG_INF = -1e30
LOG2E = math.log2(math.e)

LANES = 128
ROW_TILE = 1024
COL_CHUNK = 512
FF_CHUNK = 256
ATT_TILE = 256
HEAD_GROUPS = 2
STRIDE = 4
NEAR = ATT_TILE // 2
BF16_SUBLANES = 16
ONES_ROWS = 16
VMEM_LIMIT = 56 * 1024 * 1024

F32 = jnp.float32
BF16 = jnp.bfloat16
NT_DIMS = (((1,), (1,)), ((), ()))


def _rms(x, g):
    ms = jnp.mean(x * x, axis=-1, keepdims=True)
    return x * lax.rsqrt(ms + EPS) * g


def _const_spec(shape):
    nd = len(shape)
    return pl.BlockSpec(shape, lambda *_: (0,) * nd, pipeline_mode=pl.Buffered(1))


def _layer_spec(stacked, layer):
    tail = (0,) * (stacked.ndim - 1)
    return pl.BlockSpec((None,) + stacked.shape[1:], lambda *_: (layer,) + tail,
                        pipeline_mode=pl.Buffered(1))


def _cast_io(jobs, steps):
    in_specs, out_specs, out_shapes = [], [], []
    for w, layer in jobs:
        rows, cols = w.shape[1] // steps, w.shape[2]
        assert w.shape[1] % steps == 0 and rows % BF16_SUBLANES == 0
        in_specs.append(pl.BlockSpec((None, rows, cols), lambda i, layer=layer: (layer, i, 0)))
        out_specs.append(pl.BlockSpec((rows, cols), lambda i: (i, 0)))
        out_shapes.append(jax.ShapeDtypeStruct(w.shape[1:], BF16))
    return in_specs, out_specs, out_shapes


def _split_refs(refs, n_in, n_out, n_cast):
    a, b, c = n_in + n_cast, n_in + n_cast + n_out, n_in + 2 * n_cast + n_out
    return refs[:n_in], refs[n_in:a], refs[a:b], refs[b:c], refs[c:]


def _run_casts(srcs, dsts):
    for src, dst in zip(srcs, dsts):
        dst[...] = src[...].astype(BF16)


def _params(n_axes):
    return pltpu.CompilerParams(
        dimension_semantics=("arbitrary",) * n_axes, vmem_limit_bytes=VMEM_LIMIT)


def _qkv_body(x_ref, g_ref, w_ref, o_ref, os_ref, slab0_ref, slab1_ref):
    tm, d = x_ref.shape
    h = _rms(x_ref[...], g_ref[...]).astype(BF16)
    q_scale = LOG2E * HEAD_DIM ** -0.5
    for j in range(0, w_ref.shape[1], COL_CHUNK):
        slab_ref = (slab0_ref, slab1_ref)[(j // COL_CHUNK) % 2]
        y = jnp.dot(h, w_ref[:, j:j + COL_CHUNK], preferred_element_type=F32)
        if j < d:
            y = y * q_scale
        o_ref[:, j:j + COL_CHUNK] = y.astype(BF16)
        for c in range(COL_CHUNK // LANES):
            slab_ref[c] = y[:, c * LANES:(c + 1) * LANES]
        for r in range(STRIDE):
            for c in range(COL_CHUNK // LANES):
                part = slab_ref[c, pl.ds(r, tm // STRIDE, stride=STRIDE), :]
                os_ref[r, :, j + c * LANES:j + (c + 1) * LANES] = part.astype(BF16)


def _qkv_proj(x2, g, w, seq):
    t, d = x2.shape
    n = w.shape[1]
    tm = ROW_TILE
    per_seq = seq // tm
    return pl.pallas_call(
        _qkv_body,
        grid=(t // tm,),
        in_specs=[pl.BlockSpec((tm, d), lambda i: (i, 0)),
                  _const_spec((1, d)), _const_spec(w.shape)],
        out_specs=[pl.BlockSpec((tm, n), lambda i: (i, 0)),
                   pl.BlockSpec((None, STRIDE, tm // STRIDE, n),
                                lambda i: (i // per_seq, 0, i % per_seq, 0))],
        out_shape=[jax.ShapeDtypeStruct((t, n), BF16),
                   jax.ShapeDtypeStruct((t // seq, STRIDE, seq // STRIDE, n), BF16)],
        scratch_shapes=[pltpu.VMEM((COL_CHUNK // LANES, tm, LANES), F32)] * 2,
        compiler_params=_params(1),
        name="qkv_proj",
    )(x2, g, w)


def _conv_body(*refs, n_cast, per_seq):
    (x_ref, g_ref, w_ref, k_ref), cast_src, (z_ref,), cast_dst, (cu_ref,) = _split_refs(
        refs, 4, 1, n_cast)
    _run_casts(cast_src, cast_dst)
    tm, d = x_ref.shape
    first = pl.program_id(0) % per_seq == 0

    @pl.when(first)
    def _():
        cu_ref[0:8, :] = jnp.zeros((8, d), F32)

    @pl.when(jnp.logical_not(first))
    def _():
        cu_ref[0:8, :] = cu_ref[tm:tm + 8, :]

    h = _rms(x_ref[...], g_ref[...]).astype(BF16)
    for j in range(0, d, COL_CHUNK):
        cols = slice(j, j + COL_CHUNK)
        c = jnp.dot(h, w_ref[:, d + j:d + j + COL_CHUNK], preferred_element_type=F32)
        u = jnp.dot(h, w_ref[:, 2 * d + j:2 * d + j + COL_CHUNK], preferred_element_type=F32)
        cu_ref[8:8 + tm, cols] = c * u
        y = (k_ref[2:3, cols] * cu_ref[8:8 + tm, cols]
             + k_ref[1:2, cols] * cu_ref[7:7 + tm, cols]
             + k_ref[0:1, cols] * cu_ref[6:6 + tm, cols])
        b = jnp.dot(h, w_ref[:, cols], preferred_element_type=F32)
        z_ref[:, cols] = (b * y).astype(BF16)


def _conv_pre(x2, g, w_in, taps, layer, seq, cast_jobs):
    t, d = x2.shape
    tm = ROW_TILE
    steps = t // tm
    cast_in, cast_out, cast_shapes = _cast_io(cast_jobs, steps)
    row = pl.BlockSpec((tm, d), lambda i: (i, 0))
    outs = pl.pallas_call(
        functools.partial(_conv_body, n_cast=len(cast_jobs), per_seq=seq // tm),
        grid=(steps,),
        in_specs=[row, _const_spec((1, d)), _const_spec(w_in.shape), _layer_spec(taps, layer)]
        + cast_in,
        out_specs=[row] + cast_out,
        out_shape=[jax.ShapeDtypeStruct((t, d), BF16)] + cast_shapes,
        scratch_shapes=[pltpu.VMEM((tm + 8, d), F32)],
        compiler_params=_params(1),
        name="conv_pre",
    )(x2, g, w_in, taps, *[w for w, _ in cast_jobs])
    return outs[0], outs[1:]


def _ffn_body(*refs, final, n_cast):
    ins, cast_src, (o_ref,), cast_dst, _ = _split_refs(refs, 8, 1, n_cast)
    x_ref, z_ref, wo_ref, g_ref, wg_ref, wu_ref, wd_ref, gf_ref = ins
    _run_casts(cast_src, cast_dst)
    x1 = x_ref[...] + jnp.dot(z_ref[...], wo_ref[...], preferred_element_type=F32)
    h = _rms(x1, g_ref[...]).astype(BF16)
    o_ref[...] = x1
    for c in range(0, wg_ref.shape[1], FF_CHUNK):
        gate = jnp.dot(h, wg_ref[:, c:c + FF_CHUNK], preferred_element_type=F32)
        up = jnp.dot(h, wu_ref[:, c:c + FF_CHUNK], preferred_element_type=F32)
        a = (gate * jax.nn.sigmoid(gate) * up).astype(BF16)
        o_ref[...] += jnp.dot(a, wd_ref[c:c + FF_CHUNK, :], preferred_element_type=F32)
    if final:
        o_ref[...] = _rms(o_ref[...], gf_ref[...])


def _mix_ffn(x2, z2, wo, g, wg, wu, wd, gf, final, cast_jobs):
    t, d = x2.shape
    steps = t // ROW_TILE
    cast_in, cast_out, cast_shapes = _cast_io(cast_jobs, steps)
    row = pl.BlockSpec((ROW_TILE, d), lambda i: (i, 0))
    outs = pl.pallas_call(
        functools.partial(_ffn_body, final=final, n_cast=len(cast_jobs)),
        grid=(steps,),
        in_specs=[row, row, _const_spec(wo.shape), _const_spec((1, d)), _const_spec(wg.shape),
                  _const_spec(wu.shape), _const_spec(wd.shape), _const_spec((1, d))] + cast_in,
        out_specs=[row] + cast_out,
        out_shape=[jax.ShapeDtypeStruct((t, d), F32)] + cast_shapes,
        compiler_params=_params(1),
        name="mix_ffn_final" if final else "mix_ffn",
    )(x2, z2, wo, g, wg, wu, wd, gf, *[w for w, _ in cast_jobs])
    return outs[0], outs[1:]


def _t5_bucket(dist):
    exact = NUM_BUCKETS // 2
    df = jnp.maximum(dist, 1).astype(F32)
    large = exact + (jnp.log(df / exact) / math.log(MAX_DISTANCE / exact)
                     * (NUM_BUCKETS - exact)).astype(jnp.int32)
    large = jnp.minimum(large, NUM_BUCKETS - 1)
    return jnp.where(dist < exact, dist, large)


def _bias_rows(rel_bias, n_dist, stride, branches, period):
    dist = jnp.arange(n_dist, dtype=jnp.int32) * stride
    mult = jnp.zeros((n_dist,), F32)
    for window, dilation in branches:
        mult = mult + ((dist % dilation == 0) & (dist <= window)).astype(F32)
    table = rel_bias[_t5_bucket(dist)].astype(F32)
    total = jnp.where(mult[:, None] > 0,
                      (table + jnp.log(jnp.maximum(mult, 1.0))[:, None]) * LOG2E, NEG_INF)
    pad = jnp.full((period - n_dist, total.shape[1]), NEG_INF, F32)
    return jnp.concatenate([total, pad], axis=0).T[:, None, :]


def _attn_body(tv1_ref, tv2_ref, q_ref, k_ref, v_ref, qs_ref, ks_ref, vs_ref, o_ref,
               bias1_ref, bias10_ref, bias2_ref, lhs_ref, lhss_ref, o2_ref, lse2_ref):
    tq = ATT_TILE
    seq, width = k_ref.shape
    sub = ks_ref.shape[1]
    n_sub = sub // tq
    groups = width // LANES
    heads = width // HEAD_DIM

    @pl.when(pl.program_id(1) == 0)
    def _():
        for h in range(heads):
            row = jnp.broadcast_to(tv1_ref[h], (2 * NEAR, tv1_ref.shape[2]))
            toep = pltpu.roll(row, 0, 1, stride=1, stride_axis=0)
            bias10_ref[:, h * NEAR:(h + 1) * NEAR] = toep[:NEAR, :NEAR]
            bias1_ref[:, h * NEAR:(h + 1) * NEAR] = toep[:, NEAR:2 * NEAR]
            g, cols = h // 2, slice((h % 2) * tq, (h % 2 + 1) * tq)
            row = jnp.broadcast_to(tv2_ref[h], (tq, tv2_ref.shape[2]))
            toep = pltpu.roll(row, 0, 1, stride=1, stride_axis=0)
            for dist in range(n_sub):
                start = (n_sub - 1 - dist) * tq
                bias2_ref[g, start:start + tq, cols] = toep[:, dist * tq:(dist + 1) * tq]

    for kb in range(seq // tq):
        lhs_ref[:width, kb * tq:(kb + 1) * tq] = v_ref[kb * tq:(kb + 1) * tq, :].T
    lhs_ref[width:, :] = jnp.ones((ONES_ROWS, seq), BF16)
    for g in range(groups):
        lanes = slice(g * LANES, (g + 1) * LANES)
        for r in range(STRIDE):
            for kb in range(n_sub):
                lhss_ref[g, r, :LANES, kb * tq:(kb + 1) * tq] = (
                    vs_ref[r, kb * tq:(kb + 1) * tq, lanes].T)
            lhss_ref[g, r, LANES:, :] = jnp.ones((ONES_ROWS, sub), BF16)

    lane_head = lax.broadcasted_iota(jnp.int32, (1, width), 1) // HEAD_DIM
    items = [(g, r, j, 0, (j + 1) * tq)
             for g in range(groups) for r in range(STRIDE) for j in range(n_sub)]
    items += [(None, None, qb, max(0, (qb - 1) * NEAR), (qb + 1) * NEAR)
              for qb in range(seq // NEAR)]
    col_max, scores, probs = {}, {}, {}

    def per_head_rows(q):
        zero = jnp.zeros_like(q)
        lanes = lane_head[:, :q.shape[1]]
        return jnp.concatenate(
            [jnp.where(lanes == h, q, zero) for h in range(q.shape[1] // HEAD_DIM)], axis=0)

    def stage_a(t):
        g, r, qb, lo, hi = items[t]
        if g is None:
            q = per_head_rows(q_ref[qb * NEAR:(qb + 1) * NEAR, :])
            k = k_ref[lo:hi, :]
            bias = bias10_ref[...] if lo == qb * NEAR else bias1_ref[...]
        else:
            lanes = slice(g * LANES, (g + 1) * LANES)
            q = per_head_rows(qs_ref[r, qb * tq:(qb + 1) * tq, lanes])
            k = ks_ref[r, lo:hi, lanes]
            bias = bias2_ref[g, sub - hi:, :]
        s = (lax.dot_general(k, q, NT_DIMS, preferred_element_type=F32) + bias).astype(BF16)
        scores[t] = s
        col_max[t] = jnp.max(s, axis=0, keepdims=True)

    def stage_b(t):
        probs[t] = jnp.exp2(scores.pop(t) - col_max[t])

    def stage_c(t):
        g, r, qb, lo, hi = items[t]
        lhs = lhs_ref[:, lo:hi] if g is None else lhss_ref[g, r, :, lo:hi]
        acc = jnp.dot(lhs, probs.pop(t), preferred_element_type=F32)
        n_heads = (lhs.shape[0] - ONES_ROWS) // HEAD_DIM
        nq = acc.shape[1] // n_heads
        den = acc[n_heads * HEAD_DIM:n_heads * HEAD_DIM + 1]
        lse = col_max.pop(t).astype(F32) + jnp.log2(den)
        out_t = jnp.concatenate(
            [acc[h * HEAD_DIM:(h + 1) * HEAD_DIM, h * nq:(h + 1) * nq] / den[:, h * nq:(h + 1) * nq]
             for h in range(n_heads)], axis=0)
        lse_t = jnp.concatenate(
            [jnp.broadcast_to(lse[:, h * nq:(h + 1) * nq], (HEAD_DIM, nq)) for h in range(n_heads)],
            axis=0)
        out, lse = out_t.T, lse_t.T
        if g is not None:
            token_rows = pl.ds(qb * tq * STRIDE + r, tq, stride=STRIDE)
            o2_ref[g, token_rows, :] = out
            lse2_ref[g, token_rows, :] = lse
        else:
            rows = slice(qb * NEAR, (qb + 1) * NEAR)
            out2 = jnp.concatenate([o2_ref[i, rows, :] for i in range(groups)], axis=1)
            lse2 = jnp.concatenate([lse2_ref[i, rows, :] for i in range(groups)], axis=1)
            top = jnp.maximum(lse, lse2)
            w1, w2 = jnp.exp2(lse - top), jnp.exp2(lse2 - top)
            o_ref[rows, :] = ((w1 * out + w2 * out2) / (w1 + w2)).astype(BF16)

    for t in range(len(items) + 2):
        if t < len(items):
            stage_a(t)
        if 1 <= t <= len(items):
            stage_b(t - 1)
        if t >= 2:
            stage_c(t - 2)


def _attention(qkv, qkv_s, tv1, tv2):
    bsz, s, d3 = qkv.shape
    d = d3 // 3
    width = HEAD_GROUPS * LANES
    heads = width // HEAD_DIM
    n_blocks = d // width
    sub = s // STRIDE
    col = lambda off: pl.BlockSpec((None, s, width), lambda hb, b: (b, 0, off + hb))
    col_s = lambda off: pl.BlockSpec((None, STRIDE, sub, width),
                                     lambda hb, b: (b, 0, 0, off + hb))
    bias_row = lambda tv: pl.BlockSpec((heads, 1, tv.shape[2]), lambda hb, b: (hb, 0, 0))
    wide = 2 * ATT_TILE
    assert heads * NEAR == wide
    return pl.pallas_call(
        _attn_body,
        grid=(n_blocks, bsz),
        in_specs=[bias_row(tv1), bias_row(tv2),
                  col(0), col(n_blocks), col(2 * n_blocks),
                  col_s(0), col_s(n_blocks), col_s(2 * n_blocks)],
        out_specs=col(0),
        out_shape=jax.ShapeDtypeStruct((bsz, s, d), BF16),
        scratch_shapes=[pltpu.VMEM((2 * NEAR, wide), F32),
                        pltpu.VMEM((NEAR, wide), F32),
                        pltpu.VMEM((HEAD_GROUPS, sub, wide), F32),
                        pltpu.VMEM((width + ONES_ROWS, s), BF16),
                        pltpu.VMEM((HEAD_GROUPS, STRIDE, LANES + ONES_ROWS, sub), BF16),
                        pltpu.VMEM((HEAD_GROUPS, s, LANES), F32),
                        pltpu.VMEM((HEAD_GROUPS, s, LANES), F32)],
        compiler_params=_params(2),
        name="dilated_attention",
    )(tv1, tv2, qkv, qkv, qkv, qkv_s, qkv_s, qkv_s)


def kernel(x, mix_norm, ffn_norm, final_norm, conv_w_in, conv_kernel, conv_w_out,
           attn_w_qkv, attn_w_out, rel_bias, ffn_w_gate, ffn_w_up, ffn_w_down):
    bsz, s, d = x.shape
    depth = mix_norm.shape[0]
    assert d == N_HEADS * HEAD_DIM and s % ROW_TILE == 0 and ROW_TILE % ATT_TILE == 0
    near = tuple((w, dil) for w, dil in BRANCHES if dil == 1)
    far = tuple((w, dil) for w, dil in BRANCHES if dil > 1)
    assert all(w <= NEAR for w, _ in near) and all(dil % STRIDE == 0 for _, dil in far)
    assert s % (STRIDE * ATT_TILE) == 0 and ROW_TILE % STRIDE == 0

    tv1 = _bias_rows(rel_bias, NEAR + 1, 1, near, 3 * ATT_TILE)
    tv2 = _bias_rows(rel_bias, s // STRIDE, STRIDE, far, s // STRIDE + ATT_TILE)
    gf = final_norm.reshape(1, d)

    def layer_params(i):
        mix_in, mix_out = (conv_w_in, conv_w_out) if i % 2 == 0 else (attn_w_qkv, attn_w_out)
        return [(mix_in, i // 2), (mix_out, i // 2),
                (ffn_w_gate, i), (ffn_w_up, i), (ffn_w_down, i)]

    mix_in = conv_w_in[0].astype(BF16)
    rest = None
    x2 = x.reshape(bsz * s, d)
    for i in range(depth):
        g_mix = mix_norm[i].reshape(1, d)
        if i % 2 == 0:
            jobs = layer_params(i)[1:] if i == 0 else []
            z, made = _conv_pre(x2, g_mix, mix_in, conv_kernel, i // 2, s, jobs)
            rest = made if i == 0 else rest
        else:
            qkv, qkv_s = _qkv_proj(x2, g_mix, mix_in, s)
            z = _attention(qkv.reshape(bsz, s, 3 * d), qkv_s, tv1, tv2).reshape(bsz * s, d)
        mix_out, w_g, w_u, w_d = rest
        jobs = layer_params(i + 1) if i + 1 < depth else []
        x2, made = _mix_ffn(x2, z, mix_out, ffn_norm[i].reshape(1, d), w_g, w_u, w_d, gf,
                            final=(i == depth - 1), cast_jobs=jobs)
        if made:
            mix_in, rest = made[0], made[1:]
    return x2.reshape(bsz, s, d)
```

```python
import functools
import math

import jax
import jax.numpy as jnp
from jax import lax
from jax.experimental import pallas as pl
from jax.experimental.pallas import tpu as pltpu

N_HEADS = 16
HEAD_DIM = 64
CONV_WIDTH = 3
BRANCHES = ((128, 1), (512, 4), (2048, 16))
NUM_BUCKETS = 32
MAX_DISTANCE = 2048
EPS = 1e-6
NEG_INF = -1e30
LOG2E = math.log2(math.e)

LANES = 128
ROW_TILE = 1024
COL_CHUNK = 512
FF_CHUNK = 256
ATT_TILE = 256
HEAD_GROUPS = 2
STRIDE = 4
NEAR = ATT_TILE // 2
BF16_SUBLANES = 16
ONES_ROWS = 16
VMEM_LIMIT = 56 * 1024 * 1024

F32 = jnp.float32
BF16 = jnp.bfloat16
NT_DIMS = (((1,), (1,)), ((), ()))


def _rms(x, g):
    ms = jnp.mean(x * x, axis=-1, keepdims=True)
    return x * lax.rsqrt(ms + EPS) * g


def _const_spec(shape):
    nd = len(shape)
    return pl.BlockSpec(shape, lambda *_: (0,) * nd, pipeline_mode=pl.Buffered(1))


def _layer_spec(stacked, layer):
    tail = (0,) * (stacked.ndim - 1)
    return pl.BlockSpec((None,) + stacked.shape[1:], lambda *_: (layer,) + tail,
                        pipeline_mode=pl.Buffered(1))


def _cast_io(jobs, steps):
    in_specs, out_specs, out_shapes = [], [], []
    for w, layer in jobs:
        rows, cols = w.shape[1] // steps, w.shape[2]
        assert w.shape[1] % steps == 0 and rows % BF16_SUBLANES == 0
        in_specs.append(pl.BlockSpec((None, rows, cols), lambda i, layer=layer: (layer, i, 0)))
        out_specs.append(pl.BlockSpec((rows, cols), lambda i: (i, 0)))
        out_shapes.append(jax.ShapeDtypeStruct(w.shape[1:], BF16))
    return in_specs, out_specs, out_shapes


def _split_refs(refs, n_in, n_out, n_cast):
    a, b, c = n_in + n_cast, n_in + n_cast + n_out, n_in + 2 * n_cast + n_out
    return refs[:n_in], refs[n_in:a], refs[a:b], refs[b:c], refs[c:]


def _run_casts(srcs, dsts):
    for src, dst in zip(srcs, dsts):
        dst[...] = src[...].astype(BF16)


def _params(n_axes):
    return pltpu.CompilerParams(
        dimension_semantics=("arbitrary",) * n_axes, vmem_limit_bytes=VMEM_LIMIT)


def _qkv_body(x_ref, g_ref, w_ref, o_ref, os_ref, slab0_ref, slab1_ref):
    tm, d = x_ref.shape
    h = _rms(x_ref[...], g_ref[...]).astype(BF16)
    q_scale = LOG2E * HEAD_DIM ** -0.5
    for j in range(0, w_ref.shape[1], COL_CHUNK):
        slab_ref = (slab0_ref, slab1_ref)[(j // COL_CHUNK) % 2]
        y = jnp.dot(h, w_ref[:, j:j + COL_CHUNK], preferred_element_type=F32)
        if j < d:
            y = y * q_scale
        o_ref[:, j:j + COL_CHUNK] = y.astype(BF16)
        for c in range(COL_CHUNK // LANES):
            slab_ref[c] = y[:, c * LANES:(c + 1) * LANES]
        for r in range(STRIDE):
            for c in range(COL_CHUNK // LANES):
                part = slab_ref[c, pl.ds(r, tm // STRIDE, stride=STRIDE), :]
                os_ref[r, :, j + c * LANES:j + (c + 1) * LANES] = part.astype(BF16)


def _qkv_proj(x2, g, w, seq):
    t, d = x2.shape
    n = w.shape[1]
    tm = ROW_TILE
    per_seq = seq // tm
    return pl.pallas_call(
        _qkv_body,
        grid=(t // tm,),
        in_specs=[pl.BlockSpec((tm, d), lambda i: (i, 0)),
                  _const_spec((1, d)), _const_spec(w.shape)],
        out_specs=[pl.BlockSpec((tm, n), lambda i: (i, 0)),
                   pl.BlockSpec((None, STRIDE, tm // STRIDE, n),
                                lambda i: (i // per_seq, 0, i % per_seq, 0))],
        out_shape=[jax.ShapeDtypeStruct((t, n), BF16),
                   jax.ShapeDtypeStruct((t // seq, STRIDE, seq // STRIDE, n), BF16)],
        scratch_shapes=[pltpu.VMEM((COL_CHUNK // LANES, tm, LANES), F32)] * 2,
        compiler_params=_params(1),
        name="qkv_proj",
    )(x2, g, w)


def _conv_body(*refs, n_cast, per_seq):
    (x_ref, g_ref, w_ref, k_ref), cast_src, (z_ref,), cast_dst, (cu_ref,) = _split_refs(
        refs, 4, 1, n_cast)
    _run_casts(cast_src, cast_dst)
    tm, d = x_ref.shape
    first = pl.program_id(0) % per_seq == 0

    @pl.when(first)
    def _():
        cu_ref[0:8, :] = jnp.zeros((8, d), F32)

    @pl.when(jnp.logical_not(first))
    def _():
        cu_ref[0:8, :] = cu_ref[tm:tm + 8, :]

    h = _rms(x_ref[...], g_ref[...]).astype(BF16)
    for j in range(0, d, COL_CHUNK):
        cols = slice(j, j + COL_CHUNK)
        c = jnp.dot(h, w_ref[:, d + j:d + j + COL_CHUNK], preferred_element_type=F32)
        u = jnp.dot(h, w_ref[:, 2 * d + j:2 * d + j + COL_CHUNK], preferred_element_type=F32)
        cu_ref[8:8 + tm, cols] = c * u
        y = (k_ref[2:3, cols] * cu_ref[8:8 + tm, cols]
             + k_ref[1:2, cols] * cu_ref[7:7 + tm, cols]
             + k_ref[0:1, cols] * cu_ref[6:6 + tm, cols])
        b = jnp.dot(h, w_ref[:, cols], preferred_element_type=F32)
        z_ref[:, cols] = (b * y).astype(BF16)


def _conv_pre(x2, g, w_in, taps, layer, seq, cast_jobs):
    t, d = x2.shape
    tm = ROW_TILE
    steps = t // tm
    cast_in, cast_out, cast_shapes = _cast_io(cast_jobs, steps)
    row = pl.BlockSpec((tm, d), lambda i: (i, 0))
    outs = pl.pallas_call(
        functools.partial(_conv_body, n_cast=len(cast_jobs), per_seq=seq // tm),
        grid=(steps,),
        in_specs=[row, _const_spec((1, d)), _const_spec(w_in.shape), _layer_spec(taps, layer)]
        + cast_in,
        out_specs=[row] + cast_out,
        out_shape=[jax.ShapeDtypeStruct((t, d), BF16)] + cast_shapes,
        scratch_shapes=[pltpu.VMEM((tm + 8, d), F32)],
        compiler_params=_params(1),
        name="conv_pre",
    )(x2, g, w_in, taps, *[w for w, _ in cast_jobs])
    return outs[0], outs[1:]


def _ffn_body(*refs, final, n_cast):
    ins, cast_src, (o_ref,), cast_dst, _ = _split_refs(refs, 8, 1, n_cast)
    x_ref, z_ref, wo_ref, g_ref, wg_ref, wu_ref, wd_ref, gf_ref = ins
    _run_casts(cast_src, cast_dst)
    x1 = x_ref[...] + jnp.dot(z_ref[...], wo_ref[...], preferred_element_type=F32)
    h = _rms(x1, g_ref[...]).astype(BF16)
    o_ref[...] = x1
    for c in range(0, wg_ref.shape[1], FF_CHUNK):
        gate = jnp.dot(h, wg_ref[:, c:c + FF_CHUNK], preferred_element_type=F32)
        up = jnp.dot(h, wu_ref[:, c:c + FF_CHUNK], preferred_element_type=F32)
        a = (gate * jax.nn.sigmoid(gate) * up).astype(BF16)
        o_ref[...] += jnp.dot(a, wd_ref[c:c + FF_CHUNK, :], preferred_element_type=F32)
    if final:
        o_ref[...] = _rms(o_ref[...], gf_ref[...])


def _mix_ffn(x2, z2, wo, g, wg, wu, wd, gf, final, cast_jobs):
    t, d = x2.shape
    steps = t // ROW_TILE
    cast_in, cast_out, cast_shapes = _cast_io(cast_jobs, steps)
    row = pl.BlockSpec((ROW_TILE, d), lambda i: (i, 0))
    outs = pl.pallas_call(
        functools.partial(_ffn_body, final=final, n_cast=len(cast_jobs)),
        grid=(steps,),
        in_specs=[row, row, _const_spec(wo.shape), _const_spec((1, d)), _const_spec(wg.shape),
                  _const_spec(wu.shape), _const_spec(wd.shape), _const_spec((1, d))] + cast_in,
        out_specs=[row] + cast_out,
        out_shape=[jax.ShapeDtypeStruct((t, d), F32)] + cast_shapes,
        compiler_params=_params(1),
        name="mix_ffn_final" if final else "mix_ffn",
    )(x2, z2, wo, g, wg, wu, wd, gf, *[w for w, _ in cast_jobs])
    return outs[0], outs[1:]


def _t5_bucket(dist):
    exact = NUM_BUCKETS // 2
    df = jnp.maximum(dist, 1).astype(F32)
    large = exact + (jnp.log(df / exact) / math.log(MAX_DISTANCE / exact)
                     * (NUM_BUCKETS - exact)).astype(jnp.int32)
    large = jnp.minimum(large, NUM_BUCKETS - 1)
    return jnp.where(dist < exact, dist, large)


def _bias_rows(rel_bias, n_dist, stride, branches, period):
    dist = jnp.arange(n_dist, dtype=jnp.int32) * stride
    mult = jnp.zeros((n_dist,), F32)
    for window, dilation in branches:
        mult = mult + ((dist % dilation == 0) & (dist <= window)).astype(F32)
    table = rel_bias[_t5_bucket(dist)].astype(F32)
    total = jnp.where(mult[:, None] > 0,
                      (table + jnp.log(jnp.maximum(mult, 1.0))[:, None]) * LOG2E, NEG_INF)
    pad = jnp.full((period - n_dist, total.shape[1]), NEG_INF, F32)
    return jnp.concatenate([total, pad], axis=0).T[:, None, :]


def _attn_body(tv1_ref, tv2_ref, q_ref, k_ref, v_ref, qs_ref, ks_ref, vs_ref, o_ref,
               bias1_ref, bias10_ref, bias2_ref, lhs_ref, lhss_ref, o2_ref, lse2_ref):
    tq = ATT_TILE
    seq, width = k_ref.shape
    sub = ks_ref.shape[1]
    n_sub = sub // tq
    groups = width // LANES
    heads = width // HEAD_DIM

    @pl.when(pl.program_id(1) == 0)
    def _():
        for h in range(heads):
            row = jnp.broadcast_to(tv1_ref[h], (2 * NEAR, tv1_ref.shape[2]))
            toep = pltpu.roll(row, 0, 1, stride=1, stride_axis=0)
            bias10_ref[:, h * NEAR:(h + 1) * NEAR] = toep[:NEAR, :NEAR]
            bias1_ref[:, h * NEAR:(h + 1) * NEAR] = toep[:, NEAR:2 * NEAR]
            g, cols = h // 2, slice((h % 2) * tq, (h % 2 + 1) * tq)
            row = jnp.broadcast_to(tv2_ref[h], (tq, tv2_ref.shape[2]))
            toep = pltpu.roll(row, 0, 1, stride=1, stride_axis=0)
            for dist in range(n_sub):
                start = (n_sub - 1 - dist) * tq
                bias2_ref[g, start:start + tq, cols] = toep[:, dist * tq:(dist + 1) * tq]

    for g in range(groups):
        lanes = slice(g * LANES, (g + 1) * LANES)
        for kb in range(seq // tq):
            lhs_ref[g, :LANES, kb * tq:(kb + 1) * tq] = v_ref[kb * tq:(kb + 1) * tq, lanes].T
        lhs_ref[g, LANES:, :] = jnp.ones((ONES_ROWS, seq), BF16)
        for r in range(STRIDE):
            for kb in range(n_sub):
                lhss_ref[g, r, :LANES, kb * tq:(kb + 1) * tq] = (
                    vs_ref[r, kb * tq:(kb + 1) * tq, lanes].T)
            lhss_ref[g, r, LANES:, :] = jnp.ones((ONES_ROWS, sub), BF16)

    lane_head = lax.broadcasted_iota(jnp.int32, (1, width), 1) // HEAD_DIM
    items = [(g, r, j, 0, (j + 1) * tq)
             for g in range(groups) for r in range(STRIDE) for j in range(n_sub)]
    items += [(None, None, qb, max(0, (qb - 1) * NEAR), (qb + 1) * NEAR)
              for qb in range(seq // NEAR)]
    col_max, scores, probs = {}, {}, {}
    half = tq // 2
    late_cols = (half, tq + half)

    def per_head_rows(q):
        zero = jnp.zeros_like(q)
        lanes = lane_head[:, :q.shape[1]]
        return jnp.concatenate(
            [jnp.where(lanes == h, q, zero) for h in range(q.shape[1] // HEAD_DIM)], axis=0)

    def stage_a(t):
        g, r, qb, lo, hi = items[t]
        if g is None:
            q = per_head_rows(q_ref[qb * NEAR:(qb + 1) * NEAR, :])
            k = k_ref[lo:hi, :]
            bias = bias10_ref[...] if lo == qb * NEAR else bias1_ref[...]
        else:
            lanes = slice(g * LANES, (g + 1) * LANES)
            q = per_head_rows(qs_ref[r, qb * tq:(qb + 1) * tq, lanes])
            k = ks_ref[r, lo:hi, lanes]
            bias = bias2_ref[g, sub - hi:, :]
        s = lax.dot_general(k, q, NT_DIMS, preferred_element_type=F32) + bias
        if g is None:
            s = s.astype(BF16)
            scores[t] = (s, None)
            col_max[t] = jnp.max(s, axis=0, keepdims=True)
        else:
            old = s[:hi - lo - half].astype(BF16)
            new = [s[hi - lo - half:, c:c + half].astype(BF16) for c in late_cols]
            scores[t] = (old, new)
            m_old = jnp.max(old, axis=0, keepdims=True)
            m_new = [jnp.max(x, axis=0, keepdims=True) for x in new]
            col_max[t] = jnp.concatenate(
                [m_old[:, :half], jnp.maximum(m_old[:, half:tq], m_new[0]),
                 m_old[:, tq:tq + half], jnp.maximum(m_old[:, tq + half:], m_new[1])], axis=1)

    def stage_b(t):
        old, new = scores.pop(t)
        m = col_max[t]
        p = jnp.exp2(old - m)
        if new is not None:
            zero = jnp.zeros((half, half), BF16)
            p_new = [jnp.exp2(x - m[:, c:c + half]) for x, c in zip(new, late_cols)]
            p = jnp.concatenate(
                [p, jnp.concatenate([zero, p_new[0], zero, p_new[1]], axis=1)], axis=0)
        probs[t] = p

    def stage_c(t):
        g, r, qb, lo, hi = items[t]
        p, m = probs.pop(t), col_max.pop(t).astype(F32)
        if g is None:
            parts = [(jnp.dot(lhs_ref[i, :, lo:hi], p[:, i * tq:(i + 1) * tq],
                              preferred_element_type=F32), m[:, i * tq:(i + 1) * tq])
                     for i in range(groups)]
        else:
            parts = [(jnp.dot(lhss_ref[g, r, :, lo:hi], p, preferred_element_type=F32), m)]
        outs, lses = [], []
        for acc, m_part in parts:
            nq = acc.shape[1] // 2
            den = acc[LANES:LANES + 1]
            lse = m_part + jnp.log2(den)
            for h in range(2):
                cols = slice(h * nq, (h + 1) * nq)
                outs.append(acc[h * HEAD_DIM:(h + 1) * HEAD_DIM, cols] / den[:, cols])
                lses.append(jnp.broadcast_to(lse[:, cols], (HEAD_DIM, nq)))
        out_t, lse_t = jnp.concatenate(outs, axis=0), jnp.concatenate(lses, axis=0)
        out, lse = out_t.T, lse_t.T
        if g is not None:
            token_rows = pl.ds(qb * tq * STRIDE + r, tq, stride=STRIDE)
            o2_ref[g, token_rows, :] = out
            lse2_ref[g, token_rows, :] = lse
        else:
            rows = slice(qb * NEAR, (qb + 1) * NEAR)
            out2 = jnp.concatenate([o2_ref[i, rows, :] for i in range(groups)], axis=1)
            lse2 = jnp.concatenate([lse2_ref[i, rows, :] for i in range(groups)], axis=1)
            top = jnp.maximum(lse, lse2)
            w1, w2 = jnp.exp2(lse - top), jnp.exp2(lse2 - top)
            o_ref[rows, :] = ((w1 * out + w2 * out2) / (w1 + w2)).astype(BF16)

    for t in range(len(items) + 2):
        if t < len(items):
            stage_a(t)
        if 1 <= t <= len(items):
            stage_b(t - 1)
        if t >= 2:
            stage_c(t - 2)


def _attention(qkv, qkv_s, tv1, tv2):
    bsz, s, d3 = qkv.shape
    d = d3 // 3
    width = HEAD_GROUPS * LANES
    heads = width // HEAD_DIM
    n_blocks = d // width
    sub = s // STRIDE
    col = lambda off: pl.BlockSpec((None, s, width), lambda hb, b: (b, 0, off + hb))
    col_s = lambda off: pl.BlockSpec((None, STRIDE, sub, width),
                                     lambda hb, b: (b, 0, 0, off + hb))
    bias_row = lambda tv: pl.BlockSpec((heads, 1, tv.shape[2]), lambda hb, b: (hb, 0, 0))
    wide = 2 * ATT_TILE
    assert heads * NEAR == wide
    return pl.pallas_call(
        _attn_body,
        grid=(n_blocks, bsz),
        in_specs=[bias_row(tv1), bias_row(tv2),
                  col(0), col(n_blocks), col(2 * n_blocks),
                  col_s(0), col_s(n_blocks), col_s(2 * n_blocks)],
        out_specs=col(0),
        out_shape=jax.ShapeDtypeStruct((bsz, s, d), BF16),
        scratch_shapes=[pltpu.VMEM((2 * NEAR, wide), F32),
                        pltpu.VMEM((NEAR, wide), F32),
                        pltpu.VMEM((HEAD_GROUPS, sub, wide), F32),
                        pltpu.VMEM((HEAD_GROUPS, LANES + ONES_ROWS, s), BF16),
                        pltpu.VMEM((HEAD_GROUPS, STRIDE, LANES + ONES_ROWS, sub), BF16),
                        pltpu.VMEM((HEAD_GROUPS, s, LANES), F32),
                        pltpu.VMEM((HEAD_GROUPS, s, LANES), F32)],
        compiler_params=_params(2),
        name="dilated_attention",
    )(tv1, tv2, qkv, qkv, qkv, qkv_s, qkv_s, qkv_s)


def kernel(x, mix_norm, ffn_norm, final_norm, conv_w_in, conv_kernel, conv_w_out,
           attn_w_qkv, attn_w_out, rel_bias, ffn_w_gate, ffn_w_up, ffn_w_down):
    bsz, s, d = x.shape
    depth = mix_norm.shape[0]
    assert d == N_HEADS * HEAD_DIM and s % ROW_TILE == 0 and ROW_TILE % ATT_TILE == 0
    near = tuple((w, dil) for w, dil in BRANCHES if dil == 1)
    far = tuple((w, dil) for w, dil in BRANCHES if dil > 1)
    assert all(w <= NEAR for w, _ in near) and all(dil % STRIDE == 0 for _, dil in far)
    assert s % (STRIDE * ATT_TILE) == 0 and ROW_TILE % STRIDE == 0

    tv1 = _bias_rows(rel_bias, NEAR + 1, 1, near, 3 * ATT_TILE)
    tv2 = _bias_rows(rel_bias, s // STRIDE, STRIDE, far, s // STRIDE + ATT_TILE)
    gf = final_norm.reshape(1, d)

    def layer_params(i):
        mix_in, mix_out = (conv_w_in, conv_w_out) if i % 2 == 0 else (attn_w_qkv, attn_w_out)
        return [(mix_in, i // 2), (mix_out, i // 2),
                (ffn_w_gate, i), (ffn_w_up, i), (ffn_w_down, i)]

    mix_in = conv_w_in[0].astype(BF16)
    rest = None
    x2 = x.reshape(bsz * s, d)
    for i in range(depth):
        g_mix = mix_norm[i].reshape(1, d)
        if i % 2 == 0:
            jobs = layer_params(i)[1:] if i == 0 else []
            z, made = _conv_pre(x2, g_mix, mix_in, conv_kernel, i // 2, s, jobs)
            rest = made if i == 0 else rest
        else:
            qkv, qkv_s = _qkv_proj(x2, g_mix, mix_in, s)
            z = _attention(qkv.reshape(bsz, s, 3 * d), qkv_s, tv1, tv2).reshape(bsz * s, d)
        mix_out, w_g, w_u, w_d = rest
        jobs = layer_params(i + 1) if i + 1 < depth else []
        x2, made = _mix_ffn(x2, z, mix_out, ffn_norm[i].reshape(1, d), w_g, w_u, w_d, gf,
                            final=(i == depth - 1), cast_jobs=jobs)
        if made:
            mix_in, rest = made[0], made[1:]
    return x2.reshape(bsz, s, d)
```

```python
import functools
import math

import jax
import jax.numpy as jnp
from jax import lax
from jax.experimental import pallas as pl
from jax.experimental.pallas import tpu as pltpu

N_HEADS = 16
HEAD_DIM = 64
CONV_WIDTH = 3
BRANCHES = ((128, 1), (512, 4), (2048, 16))
NUM_BUCKETS = 32
MAX_DISTANCE = 2048
EPS = 1e-6
NEG_INF = -1e30
LOG2E = math.log2(math.e)

LANES = 128
ROW_TILE = 1024
COL_CHUNK = 512
FF_CHUNK = 256
ATT_TILE = 256
HEAD_GROUPS = 2
STRIDE = 4
NEAR = ATT_TILE // 2
BF16_SUBLANES = 16
ONES_ROWS = 16
VMEM_LIMIT = 56 * 1024 * 1024

F32 = jnp.float32
BF16 = jnp.bfloat16
NT_DIMS = (((1,), (1,)), ((), ()))


def _rms(x, g):
    ms = jnp.mean(x * x, axis=-1, keepdims=True)
    return x * lax.rsqrt(ms + EPS) * g


def _const_spec(shape):
    nd = len(shape)
    return pl.BlockSpec(shape, lambda *_: (0,) * nd, pipeline_mode=pl.Buffered(1))


def _layer_spec(stacked, layer):
    tail = (0,) * (stacked.ndim - 1)
    return pl.BlockSpec((None,) + stacked.shape[1:], lambda *_: (layer,) + tail,
                        pipeline_mode=pl.Buffered(1))


def _cast_io(jobs, steps):
    in_specs, out_specs, out_shapes = [], [], []
    for w, layer in jobs:
        rows, cols = w.shape[1] // steps, w.shape[2]
        assert w.shape[1] % steps == 0 and rows % BF16_SUBLANES == 0
        in_specs.append(pl.BlockSpec((None, rows, cols), lambda i, layer=layer: (layer, i, 0)))
        out_specs.append(pl.BlockSpec((rows, cols), lambda i: (i, 0)))
        out_shapes.append(jax.ShapeDtypeStruct(w.shape[1:], BF16))
    return in_specs, out_specs, out_shapes


def _split_refs(refs, n_in, n_out, n_cast):
    a, b, c = n_in + n_cast, n_in + n_cast + n_out, n_in + 2 * n_cast + n_out
    return refs[:n_in], refs[n_in:a], refs[a:b], refs[b:c], refs[c:]


def _run_casts(srcs, dsts):
    for src, dst in zip(srcs, dsts):
        dst[...] = src[...].astype(BF16)


def _params(n_axes):
    return pltpu.CompilerParams(
        dimension_semantics=("arbitrary",) * n_axes, vmem_limit_bytes=VMEM_LIMIT)


def _qkv_body(x_ref, g_ref, w_ref, o_ref, os_ref, slab0_ref, slab1_ref):
    tm, d = x_ref.shape
    h = _rms(x_ref[...], g_ref[...]).astype(BF16)
    q_scale = LOG2E * HEAD_DIM ** -0.5
    for j in range(0, w_ref.shape[1], COL_CHUNK):
        slab_ref = (slab0_ref, slab1_ref)[(j // COL_CHUNK) % 2]
        y = jnp.dot(h, w_ref[:, j:j + COL_CHUNK], preferred_element_type=F32)
        if j < d:
            y = y * q_scale
        o_ref[:, j:j + COL_CHUNK] = y.astype(BF16)
        for c in range(COL_CHUNK // LANES):
            slab_ref[c] = y[:, c * LANES:(c + 1) * LANES]
        for r in range(STRIDE):
            for c in range(COL_CHUNK // LANES):
                part = slab_ref[c, pl.ds(r, tm // STRIDE, stride=STRIDE), :]
                os_ref[r, :, j + c * LANES:j + (c + 1) * LANES] = part.astype(BF16)


def _qkv_proj(x2, g, w, seq):
    t, d = x2.shape
    n = w.shape[1]
    tm = ROW_TILE
    per_seq = seq // tm
    return pl.pallas_call(
        _qkv_body,
        grid=(t // tm,),
        in_specs=[pl.BlockSpec((tm, d), lambda i: (i, 0)),
                  _const_spec((1, d)), _const_spec(w.shape)],
        out_specs=[pl.BlockSpec((tm, n), lambda i: (i, 0)),
                   pl.BlockSpec((None, STRIDE, tm // STRIDE, n),
                                lambda i: (i // per_seq, 0, i % per_seq, 0))],
        out_shape=[jax.ShapeDtypeStruct((t, n), BF16),
                   jax.ShapeDtypeStruct((t // seq, STRIDE, seq // STRIDE, n), BF16)],
        scratch_shapes=[pltpu.VMEM((COL_CHUNK // LANES, tm, LANES), F32)] * 2,
        compiler_params=_params(1),
        name="qkv_proj",
    )(x2, g, w)


def _conv_body(*refs, n_cast, per_seq):
    (x_ref, g_ref, w_ref, k_ref), cast_src, (z_ref,), cast_dst, (cu_ref,) = _split_refs(
        refs, 4, 1, n_cast)
    _run_casts(cast_src, cast_dst)
    tm, d = x_ref.shape
    first = pl.program_id(0) % per_seq == 0

    @pl.when(first)
    def _():
        cu_ref[0:8, :] = jnp.zeros((8, d), F32)

    @pl.when(jnp.logical_not(first))
    def _():
        cu_ref[0:8, :] = cu_ref[tm:tm + 8, :]

    h = _rms(x_ref[...], g_ref[...]).astype(BF16)
    for j in range(0, d, COL_CHUNK):
        cols = slice(j, j + COL_CHUNK)
        c = jnp.dot(h, w_ref[:, d + j:d + j + COL_CHUNK], preferred_element_type=F32)
        u = jnp.dot(h, w_ref[:, 2 * d + j:2 * d + j + COL_CHUNK], preferred_element_type=F32)
        cu_ref[8:8 + tm, cols] = c * u
        y = (k_ref[2:3, cols] * cu_ref[8:8 + tm, cols]
             + k_ref[1:2, cols] * cu_ref[7:7 + tm, cols]
             + k_ref[0:1, cols] * cu_ref[6:6 + tm, cols])
        b = jnp.dot(h, w_ref[:, cols], preferred_element_type=F32)
        z_ref[:, cols] = (b * y).astype(BF16)


def _conv_pre(x2, g, w_in, taps, layer, seq, cast_jobs):
    t, d = x2.shape
    tm = ROW_TILE
    steps = t // tm
    cast_in, cast_out, cast_shapes = _cast_io(cast_jobs, steps)
    row = pl.BlockSpec((tm, d), lambda i: (i, 0))
    outs = pl.pallas_call(
        functools.partial(_conv_body, n_cast=len(cast_jobs), per_seq=seq // tm),
        grid=(steps,),
        in_specs=[row, _const_spec((1, d)), _const_spec(w_in.shape), _layer_spec(taps, layer)]
        + cast_in,
        out_specs=[row] + cast_out,
        out_shape=[jax.ShapeDtypeStruct((t, d), BF16)] + cast_shapes,
        scratch_shapes=[pltpu.VMEM((tm + 8, d), F32)],
        compiler_params=_params(1),
        name="conv_pre",
    )(x2, g, w_in, taps, *[w for w, _ in cast_jobs])
    return outs[0], outs[1:]


def _ffn_body(*refs, final, n_cast):
    ins, cast_src, (o_ref,), cast_dst, _ = _split_refs(refs, 8, 1, n_cast)
    x_ref, z_ref, wo_ref, g_ref, wg_ref, wu_ref, wd_ref, gf_ref = ins
    _run_casts(cast_src, cast_dst)
    x1 = x_ref[...] + jnp.dot(z_ref[...], wo_ref[...], preferred_element_type=F32)
    h = _rms(x1, g_ref[...]).astype(BF16)
    o_ref[...] = x1
    for c in range(0, wg_ref.shape[1], FF_CHUNK):
        gate = jnp.dot(h, wg_ref[:, c:c + FF_CHUNK], preferred_element_type=F32)
        up = jnp.dot(h, wu_ref[:, c:c + FF_CHUNK], preferred_element_type=F32)
        a = (gate * jax.nn.sigmoid(gate) * up).astype(BF16)
        o_ref[...] += jnp.dot(a, wd_ref[c:c + FF_CHUNK, :], preferred_element_type=F32)
    if final:
        o_ref[...] = _rms(o_ref[...], gf_ref[...])


def _mix_ffn(x2, z2, wo, g, wg, wu, wd, gf, final, cast_jobs):
    t, d = x2.shape
    steps = t // ROW_TILE
    cast_in, cast_out, cast_shapes = _cast_io(cast_jobs, steps)
    row = pl.BlockSpec((ROW_TILE, d), lambda i: (i, 0))
    outs = pl.pallas_call(
        functools.partial(_ffn_body, final=final, n_cast=len(cast_jobs)),
        grid=(steps,),
        in_specs=[row, row, _const_spec(wo.shape), _const_spec((1, d)), _const_spec(wg.shape),
                  _const_spec(wu.shape), _const_spec(wd.shape), _const_spec((1, d))] + cast_in,
        out_specs=[row] + cast_out,
        out_shape=[jax.ShapeDtypeStruct((t, d), F32)] + cast_shapes,
        compiler_params=_params(1),
        name="mix_ffn_final" if final else "mix_ffn",
    )(x2, z2, wo, g, wg, wu, wd, gf, *[w for w, _ in cast_jobs])
    return outs[0], outs[1:]


def _t5_bucket(dist):
    exact = NUM_BUCKETS // 2
    df = jnp.maximum(dist, 1).astype(F32)
    large = exact + (jnp.log(df / exact) / math.log(MAX_DISTANCE / exact)
                     * (NUM_BUCKETS - exact)).astype(jnp.int32)
    large = jnp.minimum(large, NUM_BUCKETS - 1)
    return jnp.where(dist < exact, dist, large)


def _bias_rows(rel_bias, n_dist, stride, branches, period):
    dist = jnp.arange(n_dist, dtype=jnp.int32) * stride
    mult = jnp.zeros((n_dist,), F32)
    for window, dilation in branches:
        mult = mult + ((dist % dilation == 0) & (dist <= window)).astype(F32)
    table = rel_bias[_t5_bucket(dist)].astype(F32)
    total = jnp.where(mult[:, None] > 0,
                      (table + jnp.log(jnp.maximum(mult, 1.0))[:, None]) * LOG2E, NEG_INF)
    pad = jnp.full((period - n_dist, total.shape[1]), NEG_INF, F32)
    return jnp.concatenate([total, pad], axis=0).T[:, None, :]


def _attn_body(tv1_ref, tv2_ref, q_ref, k_ref, v_ref, qs_ref, ks_ref, vs_ref, o_ref,
               bias1_ref, bias10_ref, bias2_ref, lhs_ref, lhss_ref, o2_ref, lse2_ref):
    tq = ATT_TILE
    seq, width = k_ref.shape
    sub = ks_ref.shape[1]
    n_sub = sub // tq
    groups = width // LANES
    heads = width // HEAD_DIM

    @pl.when(pl.program_id(1) == 0)
    def _():
        for h in range(heads):
            row = jnp.broadcast_to(tv1_ref[h], (2 * NEAR, tv1_ref.shape[2]))
            toep = pltpu.roll(row, 0, 1, stride=1, stride_axis=0)
            bias10_ref[:, h * NEAR:(h + 1) * NEAR] = toep[:NEAR, :NEAR]
            bias1_ref[:, h * NEAR:(h + 1) * NEAR] = toep[:, NEAR:2 * NEAR]
            g, cols = h // 2, slice((h % 2) * tq, (h % 2 + 1) * tq)
            row = jnp.broadcast_to(tv2_ref[h], (tq, tv2_ref.shape[2]))
            toep = pltpu.roll(row, 0, 1, stride=1, stride_axis=0)
            for dist in range(n_sub):
                start = (n_sub - 1 - dist) * tq
                bias2_ref[g, start:start + tq, cols] = toep[:, dist * tq:(dist + 1) * tq]

    for g in range(groups):
        lanes = slice(g * LANES, (g + 1) * LANES)
        for kb in range(seq // tq):
            lhs_ref[g, :LANES, kb * tq:(kb + 1) * tq] = v_ref[kb * tq:(kb + 1) * tq, lanes].T
        lhs_ref[g, LANES:, :] = jnp.ones((ONES_ROWS, seq), BF16)
        for r in range(STRIDE):
            for kb in range(n_sub):
                lhss_ref[g, r, :LANES, kb * tq:(kb + 1) * tq] = (
                    vs_ref[r, kb * tq:(kb + 1) * tq, lanes].T)
            lhss_ref[g, r, LANES:, :] = jnp.ones((ONES_ROWS, sub), BF16)

    lane_head = lax.broadcasted_iota(jnp.int32, (1, width), 1) // HEAD_DIM
    items = [(g, r, j, 0, (j + 1) * tq)
             for g in range(groups) for r in range(STRIDE) for j in range(n_sub)]
    items += [(None, None, qb, max(0, (qb - 1) * NEAR), (qb + 1) * NEAR)
              for qb in range(seq // NEAR)]
    col_max, scores, probs = {}, {}, {}
    half = tq // 2
    late_cols = (half, tq + half)

    def per_head_rows(q):
        zero = jnp.zeros_like(q)
        lanes = lane_head[:, :q.shape[1]]
        return jnp.concatenate(
            [jnp.where(lanes == h, q, zero) for h in range(q.shape[1] // HEAD_DIM)], axis=0)

    def stage_a(t):
        g, r, qb, lo, hi = items[t]
        if g is None:
            q = per_head_rows(q_ref[qb * NEAR:(qb + 1) * NEAR, :])
            k = k_ref[lo:hi, :]
            bias = bias10_ref[...] if lo == qb * NEAR else bias1_ref[...]
        else:
            lanes = slice(g * LANES, (g + 1) * LANES)
            q = per_head_rows(qs_ref[r, qb * tq:(qb + 1) * tq, lanes])
            k = ks_ref[r, lo:hi, lanes]
            bias = bias2_ref[g, sub - hi:, :]
        s = lax.dot_general(k, q, NT_DIMS, preferred_element_type=F32) + bias
        if g is None:
            s = s.astype(BF16)
            scores[t] = (s, None)
            col_max[t] = jnp.max(s, axis=0, keepdims=True)
        else:
            old = s[:hi - lo - half].astype(BF16)
            new = [s[hi - lo - half:, c:c + half].astype(BF16) for c in late_cols]
            scores[t] = (old, new)
            m_old = jnp.max(old, axis=0, keepdims=True)
            m_new = [jnp.max(x, axis=0, keepdims=True) for x in new]
            col_max[t] = jnp.concatenate(
                [m_old[:, :half], jnp.maximum(m_old[:, half:tq], m_new[0]),
                 m_old[:, tq:tq + half], jnp.maximum(m_old[:, tq + half:], m_new[1])], axis=1)

    def stage_b(t):
        old, new = scores.pop(t)
        m = col_max[t]
        p = jnp.exp2((old - m).astype(F32)).astype(BF16)
        if new is not None:
            zero = jnp.zeros((half, half), BF16)
            p_new = [jnp.exp2((x - m[:, c:c + half]).astype(F32)).astype(BF16)
                     for x, c in zip(new, late_cols)]
            p = jnp.concatenate(
                [p, jnp.concatenate([zero, p_new[0], zero, p_new[1]], axis=1)], axis=0)
        probs[t] = p

    def stage_c(t):
        g, r, qb, lo, hi = items[t]
        p, m = probs.pop(t), col_max.pop(t).astype(F32)
        if g is None:
            parts = [(jnp.dot(lhs_ref[i, :, lo:hi], p[:, i * tq:(i + 1) * tq],
                              preferred_element_type=F32), m[:, i * tq:(i + 1) * tq])
                     for i in range(groups)]
        else:
            parts = [(jnp.dot(lhss_ref[g, r, :, lo:hi], p, preferred_element_type=F32), m)]
        outs, lses = [], []
        for acc, m_part in parts:
            nq = acc.shape[1] // 2
            den = acc[LANES:LANES + 1]
            lse = m_part + jnp.log2(den)
            for h in range(2):
                cols = slice(h * nq, (h + 1) * nq)
                outs.append(acc[h * HEAD_DIM:(h + 1) * HEAD_DIM, cols] / den[:, cols])
                lses.append(jnp.broadcast_to(lse[:, cols], (HEAD_DIM, nq)))
        out_t, lse_t = jnp.concatenate(outs, axis=0), jnp.concatenate(lses, axis=0)
        out, lse = out_t.T, lse_t.T
        if g is not None:
            token_rows = pl.ds(qb * tq * STRIDE + r, tq, stride=STRIDE)
            o2_ref[g, token_rows, :] = out
            lse2_ref[g, token_rows, :] = lse
        else:
            rows = slice(qb * NEAR, (qb + 1) * NEAR)
            out2 = jnp.concatenate([o2_ref[i, rows, :] for i in range(groups)], axis=1)
            lse2 = jnp.concatenate([lse2_ref[i, rows, :] for i in range(groups)], axis=1)
            top = jnp.maximum(lse, lse2)
            w1, w2 = jnp.exp2(lse - top), jnp.exp2(lse2 - top)
            o_ref[rows, :] = ((w1 * out + w2 * out2) / (w1 + w2)).astype(BF16)

    for t in range(len(items) + 2):
        if t < len(items):
            stage_a(t)
        if 1 <= t <= len(items):
            stage_b(t - 1)
        if t >= 2:
            stage_c(t - 2)


def _attention(qkv, qkv_s, tv1, tv2):
    bsz, s, d3 = qkv.shape
    d = d3 // 3
    width = HEAD_GROUPS * LANES
    heads = width // HEAD_DIM
    n_blocks = d // width
    sub = s // STRIDE
    col = lambda off: pl.BlockSpec((None, s, width), lambda hb, b: (b, 0, off + hb))
    col_s = lambda off: pl.BlockSpec((None, STRIDE, sub, width),
                                     lambda hb, b: (b, 0, 0, off + hb))
    bias_row = lambda tv: pl.BlockSpec((heads, 1, tv.shape[2]), lambda hb, b: (hb, 0, 0))
    wide = 2 * ATT_TILE
    assert heads * NEAR == wide
    return pl.pallas_call(
        _attn_body,
        grid=(n_blocks, bsz),
        in_specs=[bias_row(tv1), bias_row(tv2),
                  col(0), col(n_blocks), col(2 * n_blocks),
                  col_s(0), col_s(n_blocks), col_s(2 * n_blocks)],
        out_specs=col(0),
        out_shape=jax.ShapeDtypeStruct((bsz, s, d), BF16),
        scratch_shapes=[pltpu.VMEM((2 * NEAR, wide), F32),
                        pltpu.VMEM((NEAR, wide), F32),
                        pltpu.VMEM((HEAD_GROUPS, sub, wide), F32),
                        pltpu.VMEM((HEAD_GROUPS, LANES + ONES_ROWS, s), BF16),
                        pltpu.VMEM((HEAD_GROUPS, STRIDE, LANES + ONES_ROWS, sub), BF16),
                        pltpu.VMEM((HEAD_GROUPS, s, LANES), F32),
                        pltpu.VMEM((HEAD_GROUPS, s, LANES), F32)],
        compiler_params=_params(2),
        name="dilated_attention",
    )(tv1, tv2, qkv, qkv, qkv, qkv_s, qkv_s, qkv_s)


def kernel(x, mix_norm, ffn_norm, final_norm, conv_w_in, conv_kernel, conv_w_out,
           attn_w_qkv, attn_w_out, rel_bias, ffn_w_gate, ffn_w_up, ffn_w_down):
    bsz, s, d = x.shape
    depth = mix_norm.shape[0]
    assert d == N_HEADS * HEAD_DIM and s % ROW_TILE == 0 and ROW_TILE % ATT_TILE == 0
    near = tuple((w, dil) for w, dil in BRANCHES if dil == 1)
    far = tuple((w, dil) for w, dil in BRANCHES if dil > 1)
    assert all(w <= NEAR for w, _ in near) and all(dil % STRIDE == 0 for _, dil in far)
    assert s % (STRIDE * ATT_TILE) == 0 and ROW_TILE % STRIDE == 0

    tv1 = _bias_rows(rel_bias, NEAR + 1, 1, near, 3 * ATT_TILE)
    tv2 = _bias_rows(rel_bias, s // STRIDE, STRIDE, far, s // STRIDE + ATT_TILE)
    gf = final_norm.reshape(1, d)

    def layer_params(i):
        mix_in, mix_out = (conv_w_in, conv_w_out) if i % 2 == 0 else (attn_w_qkv, attn_w_out)
        return [(mix_in, i // 2), (mix_out, i // 2),
                (ffn_w_gate, i), (ffn_w_up, i), (ffn_w_down, i)]

    mix_in = conv_w_in[0].astype(BF16)
    rest = None
    x2 = x.reshape(bsz * s, d)
    for i in range(depth):
        g_mix = mix_norm[i].reshape(1, d)
        if i % 2 == 0:
            jobs = layer_params(i)[1:] if i == 0 else []
            z, made = _conv_pre(x2, g_mix, mix_in, conv_kernel, i // 2, s, jobs)
            rest = made if i == 0 else rest
        else:
            qkv, qkv_s = _qkv_proj(x2, g_mix, mix_in, s)
            z = _attention(qkv.reshape(bsz, s, 3 * d), qkv_s, tv1, tv2).reshape(bsz * s, d)
        mix_out, w_g, w_u, w_d = rest
        jobs = layer_params(i + 1) if i + 1 < depth else []
        x2, made = _mix_ffn(x2, z, mix_out, ffn_norm[i].reshape(1, d), w_g, w_u, w_d, gf,
                            final=(i == depth - 1), cast_jobs=jobs)
        if made:
            mix_in, rest = made[0], made[1:]
    return x2.reshape(bsz, s, d)
```

```python
import functools
import math

import jax
import jax.numpy as jnp
from jax import lax
from jax.experimental import pallas as pl
from jax.experimental.pallas import tpu as pltpu

N_HEADS = 16
HEAD_DIM = 64
CONV_WIDTH = 3
BRANCHES = ((128, 1), (512, 4), (2048, 16))
NUM_BUCKETS = 32
MAX_DISTANCE = 2048
EPS = 1e-6
NEG_INF = -1e30
LOG2E = math.log2(math.e)

LANES = 128
ROW_TILE = 1024
COL_CHUNK = 512
FF_CHUNK = 256
ATT_TILE = 256
HEAD_GROUPS = 2
STRIDE = 4
NEAR = ATT_TILE // 2
BF16_SUBLANES = 16
ONES_ROWS = 16
VMEM_LIMIT = 60 * 1024 * 1024

F32 = jnp.float32
BF16 = jnp.bfloat16
NT_DIMS = (((1,), (1,)), ((), ()))


def _rms(x, g):
    ms = jnp.mean(x * x, axis=-1, keepdims=True)
    return x * lax.rsqrt(ms + EPS) * g


def _const_spec(shape):
    nd = len(shape)
    return pl.BlockSpec(shape, lambda *_: (0,) * nd, pipeline_mode=pl.Buffered(1))


def _layer_spec(stacked, layer):
    tail = (0,) * (stacked.ndim - 1)
    return pl.BlockSpec((None,) + stacked.shape[1:], lambda *_: (layer,) + tail,
                        pipeline_mode=pl.Buffered(1))


def _cast_io(jobs, steps):
    in_specs, out_specs, out_shapes = [], [], []
    for w, layer in jobs:
        rows, cols = w.shape[1] // steps, w.shape[2]
        assert w.shape[1] % steps == 0 and rows % BF16_SUBLANES == 0
        in_specs.append(pl.BlockSpec((None, rows, cols), lambda i, layer=layer: (layer, i, 0)))
        out_specs.append(pl.BlockSpec((rows, cols), lambda i: (i, 0)))
        out_shapes.append(jax.ShapeDtypeStruct(w.shape[1:], BF16))
    return in_specs, out_specs, out_shapes


def _split_refs(refs, n_in, n_out, n_cast):
    a, b, c = n_in + n_cast, n_in + n_cast + n_out, n_in + 2 * n_cast + n_out
    return refs[:n_in], refs[n_in:a], refs[a:b], refs[b:c], refs[c:]


def _run_casts(srcs, dsts):
    for src, dst in zip(srcs, dsts):
        dst[...] = src[...].astype(BF16)


def _params(n_axes):
    return pltpu.CompilerParams(
        dimension_semantics=("arbitrary",) * n_axes, vmem_limit_bytes=VMEM_LIMIT)


def _qkv_body(x_ref, g_ref, w_ref, o_ref, os_ref, slab0_ref, slab1_ref):
    tm, d = x_ref.shape
    h = _rms(x_ref[...], g_ref[...]).astype(BF16)
    q_scale = LOG2E * HEAD_DIM ** -0.5
    for j in range(0, w_ref.shape[1], COL_CHUNK):
        slab_ref = (slab0_ref, slab1_ref)[(j // COL_CHUNK) % 2]
        y = jnp.dot(h, w_ref[:, j:j + COL_CHUNK], preferred_element_type=F32)
        if j < d:
            y = y * q_scale
        o_ref[:, j:j + COL_CHUNK] = y.astype(BF16)
        for c in range(COL_CHUNK // LANES):
            slab_ref[c] = y[:, c * LANES:(c + 1) * LANES]
        for r in range(STRIDE):
            for c in range(COL_CHUNK // LANES):
                part = slab_ref[c, pl.ds(r, tm // STRIDE, stride=STRIDE), :]
                os_ref[r, :, j + c * LANES:j + (c + 1) * LANES] = part.astype(BF16)


def _qkv_proj(x2, g, w, seq):
    t, d = x2.shape
    n = w.shape[1]
    tm = ROW_TILE
    per_seq = seq // tm
    return pl.pallas_call(
        _qkv_body,
        grid=(t // tm,),
        in_specs=[pl.BlockSpec((tm, d), lambda i: (i, 0)),
                  _const_spec((1, d)), _const_spec(w.shape)],
        out_specs=[pl.BlockSpec((tm, n), lambda i: (i, 0)),
                   pl.BlockSpec((None, STRIDE, tm // STRIDE, n),
                                lambda i: (i // per_seq, 0, i % per_seq, 0))],
        out_shape=[jax.ShapeDtypeStruct((t, n), BF16),
                   jax.ShapeDtypeStruct((t // seq, STRIDE, seq // STRIDE, n), BF16)],
        scratch_shapes=[pltpu.VMEM((COL_CHUNK // LANES, tm, LANES), F32)] * 2,
        compiler_params=_params(1),
        name="qkv_proj",
    )(x2, g, w)


def _conv_body(*refs, n_cast, per_seq):
    (x_ref, g_ref, w_ref, k_ref), cast_src, (z_ref,), cast_dst, (cu_ref,) = _split_refs(
        refs, 4, 1, n_cast)
    _run_casts(cast_src, cast_dst)
    tm, d = x_ref.shape
    first = pl.program_id(0) % per_seq == 0

    @pl.when(first)
    def _():
        cu_ref[0:8, :] = jnp.zeros((8, d), F32)

    @pl.when(jnp.logical_not(first))
    def _():
        cu_ref[0:8, :] = cu_ref[tm:tm + 8, :]

    h = _rms(x_ref[...], g_ref[...]).astype(BF16)
    for j in range(0, d, COL_CHUNK):
        cols = slice(j, j + COL_CHUNK)
        c = jnp.dot(h, w_ref[:, d + j:d + j + COL_CHUNK], preferred_element_type=F32)
        u = jnp.dot(h, w_ref[:, 2 * d + j:2 * d + j + COL_CHUNK], preferred_element_type=F32)
        cu_ref[8:8 + tm, cols] = c * u
        y = (k_ref[2:3, cols] * cu_ref[8:8 + tm, cols]
             + k_ref[1:2, cols] * cu_ref[7:7 + tm, cols]
             + k_ref[0:1, cols] * cu_ref[6:6 + tm, cols])
        b = jnp.dot(h, w_ref[:, cols], preferred_element_type=F32)
        z_ref[:, cols] = (b * y).astype(BF16)


def _conv_pre(x2, g, w_in, taps, layer, seq, cast_jobs):
    t, d = x2.shape
    tm = ROW_TILE
    steps = t // tm
    cast_in, cast_out, cast_shapes = _cast_io(cast_jobs, steps)
    row = pl.BlockSpec((tm, d), lambda i: (i, 0))
    outs = pl.pallas_call(
        functools.partial(_conv_body, n_cast=len(cast_jobs), per_seq=seq // tm),
        grid=(steps,),
        in_specs=[row, _const_spec((1, d)), _const_spec(w_in.shape), _layer_spec(taps, layer)]
        + cast_in,
        out_specs=[row] + cast_out,
        out_shape=[jax.ShapeDtypeStruct((t, d), BF16)] + cast_shapes,
        scratch_shapes=[pltpu.VMEM((tm + 8, d), F32)],
        compiler_params=_params(1),
        name="conv_pre",
    )(x2, g, w_in, taps, *[w for w, _ in cast_jobs])
    return outs[0], outs[1:]


def _ffn_body(*refs, final, n_cast):
    ins, cast_src, (o_ref,), cast_dst, (a_ref,) = _split_refs(refs, 8, 1, n_cast)
    x_ref, z_ref, wo_ref, g_ref, wg_ref, wu_ref, wd_ref, gf_ref = ins
    _run_casts(cast_src, cast_dst)
    x1 = x_ref[...] + jnp.dot(z_ref[...], wo_ref[...], preferred_element_type=F32)
    h = _rms(x1, g_ref[...]).astype(BF16)
    o_ref[...] = x1
    for c in range(0, wg_ref.shape[1], FF_CHUNK):
        gate = jnp.dot(h, wg_ref[:, c:c + FF_CHUNK], preferred_element_type=F32)
        up = jnp.dot(h, wu_ref[:, c:c + FF_CHUNK], preferred_element_type=F32)
        a_ref[:, c:c + FF_CHUNK] = (gate * jax.nn.sigmoid(gate) * up).astype(BF16)
    o_ref[...] += jnp.dot(a_ref[...], wd_ref[...], preferred_element_type=F32)
    if final:
        o_ref[...] = _rms(o_ref[...], gf_ref[...])


def _mix_ffn(x2, z2, wo, g, wg, wu, wd, gf, final, cast_jobs):
    t, d = x2.shape
    steps = t // ROW_TILE
    cast_in, cast_out, cast_shapes = _cast_io(cast_jobs, steps)
    row = pl.BlockSpec((ROW_TILE, d), lambda i: (i, 0))
    outs = pl.pallas_call(
        functools.partial(_ffn_body, final=final, n_cast=len(cast_jobs)),
        grid=(steps,),
        in_specs=[row, row, _const_spec(wo.shape), _const_spec((1, d)), _const_spec(wg.shape),
                  _const_spec(wu.shape), _const_spec(wd.shape), _const_spec((1, d))] + cast_in,
        out_specs=[row] + cast_out,
        out_shape=[jax.ShapeDtypeStruct((t, d), F32)] + cast_shapes,
        scratch_shapes=[pltpu.VMEM((ROW_TILE, wg.shape[1]), BF16)],
        compiler_params=_params(1),
        name="mix_ffn_final" if final else "mix_ffn",
    )(x2, z2, wo, g, wg, wu, wd, gf, *[w for w, _ in cast_jobs])
    return outs[0], outs[1:]


def _t5_bucket(dist):
    exact = NUM_BUCKETS // 2
    df = jnp.maximum(dist, 1).astype(F32)
    large = exact + (jnp.log(df / exact) / math.log(MAX_DISTANCE / exact)
                     * (NUM_BUCKETS - exact)).astype(jnp.int32)
    large = jnp.minimum(large, NUM_BUCKETS - 1)
    return jnp.where(dist < exact, dist, large)


def _bias_rows(rel_bias, n_dist, stride, branches, period):
    dist = jnp.arange(n_dist, dtype=jnp.int32) * stride
    mult = jnp.zeros((n_dist,), F32)
    for window, dilation in branches:
        mult = mult + ((dist % dilation == 0) & (dist <= window)).astype(F32)
    table = rel_bias[_t5_bucket(dist)].astype(F32)
    total = jnp.where(mult[:, None] > 0,
                      (table + jnp.log(jnp.maximum(mult, 1.0))[:, None]) * LOG2E, NEG_INF)
    pad = jnp.full((period - n_dist, total.shape[1]), NEG_INF, F32)
    return jnp.concatenate([total, pad], axis=0).T[:, None, :]


def _attn_body(tv1_ref, tv2_ref, q_ref, k_ref, v_ref, qs_ref, ks_ref, vs_ref, o_ref,
               bias1_ref, bias10_ref, bias2_ref, lhs_ref, lhss_ref, o2_ref, lse2_ref):
    tq = ATT_TILE
    seq, width = k_ref.shape
    sub = ks_ref.shape[1]
    n_sub = sub // tq
    groups = width // LANES
    heads = width // HEAD_DIM

    @pl.when(pl.program_id(1) == 0)
    def _():
        for h in range(heads):
            row = jnp.broadcast_to(tv1_ref[h], (2 * NEAR, tv1_ref.shape[2]))
            toep = pltpu.roll(row, 0, 1, stride=1, stride_axis=0)
            bias10_ref[:, h * NEAR:(h + 1) * NEAR] = toep[:NEAR, :NEAR]
            bias1_ref[:, h * NEAR:(h + 1) * NEAR] = toep[:, NEAR:2 * NEAR]
            g, cols = h // 2, slice((h % 2) * tq, (h % 2 + 1) * tq)
            row = jnp.broadcast_to(tv2_ref[h], (tq, tv2_ref.shape[2]))
            toep = pltpu.roll(row, 0, 1, stride=1, stride_axis=0)
            for dist in range(n_sub):
                start = (n_sub - 1 - dist) * tq
                bias2_ref[g, start:start + tq, cols] = toep[:, dist * tq:(dist + 1) * tq]

    for g in range(groups):
        lanes = slice(g * LANES, (g + 1) * LANES)
        for kb in range(seq // tq):
            lhs_ref[g, :LANES, kb * tq:(kb + 1) * tq] = v_ref[kb * tq:(kb + 1) * tq, lanes].T
        lhs_ref[g, LANES:, :] = jnp.ones((ONES_ROWS, seq), BF16)
        for r in range(STRIDE):
            for kb in range(n_sub):
                lhss_ref[g, r, :LANES, kb * tq:(kb + 1) * tq] = (
                    vs_ref[r, kb * tq:(kb + 1) * tq, lanes].T)
            lhss_ref[g, r, LANES:, :] = jnp.ones((ONES_ROWS, sub), BF16)

    lane_head = lax.broadcasted_iota(jnp.int32, (1, width), 1) // HEAD_DIM
    items = [(g, r, j, 0, (j + 1) * tq)
             for g in range(groups) for r in range(STRIDE) for j in range(n_sub)]
    items += [(None, None, qb, max(0, (qb - 1) * NEAR), (qb + 1) * NEAR)
              for qb in range(seq // NEAR)]
    col_max, scores, probs = {}, {}, {}
    half = tq // 2
    late_cols = (half, tq + half)

    def per_head_rows(q):
        zero = jnp.zeros_like(q)
        lanes = lane_head[:, :q.shape[1]]
        return jnp.concatenate(
            [jnp.where(lanes == h, q, zero) for h in range(q.shape[1] // HEAD_DIM)], axis=0)

    def stage_a(t):
        g, r, qb, lo, hi = items[t]
        if g is None:
            q = per_head_rows(q_ref[qb * NEAR:(qb + 1) * NEAR, :])
            k = k_ref[lo:hi, :]
            bias = bias10_ref[...] if lo == qb * NEAR else bias1_ref[...]
        else:
            lanes = slice(g * LANES, (g + 1) * LANES)
            q = per_head_rows(qs_ref[r, qb * tq:(qb + 1) * tq, lanes])
            k = ks_ref[r, lo:hi, lanes]
            bias = bias2_ref[g, sub - hi:, :]
        s = lax.dot_general(k, q, NT_DIMS, preferred_element_type=F32) + bias
        if g is None:
            s = s.astype(BF16)
            scores[t] = (s, None)
            col_max[t] = jnp.max(s, axis=0, keepdims=True)
        else:
            old = s[:hi - lo - half].astype(BF16)
            new = [s[hi - lo - half:, c:c + half].astype(BF16) for c in late_cols]
            scores[t] = (old, new)
            m_old = jnp.max(old, axis=0, keepdims=True)
            m_new = [jnp.max(x, axis=0, keepdims=True) for x in new]
            col_max[t] = jnp.concatenate(
                [m_old[:, :half], jnp.maximum(m_old[:, half:tq], m_new[0]),
                 m_old[:, tq:tq + half], jnp.maximum(m_old[:, tq + half:], m_new[1])], axis=1)

    def stage_b(t):
        old, new = scores.pop(t)
        m = col_max[t]
        p = jnp.exp2(old - m)
        if new is not None:
            zero = jnp.zeros((half, half), BF16)
            p_new = [jnp.exp2(x - m[:, c:c + half]) for x, c in zip(new, late_cols)]
            p = jnp.concatenate(
                [p, jnp.concatenate([zero, p_new[0], zero, p_new[1]], axis=1)], axis=0)
        probs[t] = p

    def stage_c(t):
        g, r, qb, lo, hi = items[t]
        p, m = probs.pop(t), col_max.pop(t).astype(F32)
        if g is None:
            parts = [(jnp.dot(lhs_ref[i, :, lo:hi], p[:, i * tq:(i + 1) * tq],
                              preferred_element_type=F32), m[:, i * tq:(i + 1) * tq])
                     for i in range(groups)]
        else:
            parts = [(jnp.dot(lhss_ref[g, r, :, lo:hi], p, preferred_element_type=F32), m)]
        outs, lses = [], []
        for acc, m_part in parts:
            nq = acc.shape[1] // 2
            den = acc[LANES:LANES + 1]
            lse = m_part + jnp.log2(den)
            for h in range(2):
                cols = slice(h * nq, (h + 1) * nq)
                outs.append(acc[h * HEAD_DIM:(h + 1) * HEAD_DIM, cols] / den[:, cols])
                lses.append(jnp.broadcast_to(lse[:, cols], (HEAD_DIM, nq)))
        out_t, lse_t = jnp.concatenate(outs, axis=0), jnp.concatenate(lses, axis=0)
        out, lse = out_t.T, lse_t.T
        if g is not None:
            token_rows = pl.ds(qb * tq * STRIDE + r, tq, stride=STRIDE)
            o2_ref[g, token_rows, :] = out
            lse2_ref[g, token_rows, :] = lse
        else:
            rows = slice(qb * NEAR, (qb + 1) * NEAR)
            out2 = jnp.concatenate([o2_ref[i, rows, :] for i in range(groups)], axis=1)
            lse2 = jnp.concatenate([lse2_ref[i, rows, :] for i in range(groups)], axis=1)
            top = jnp.maximum(lse, lse2)
            w1, w2 = jnp.exp2(lse - top), jnp.exp2(lse2 - top)
            o_ref[rows, :] = ((w1 * out + w2 * out2) / (w1 + w2)).astype(BF16)

    for t in range(len(items) + 2):
        if t < len(items):
            stage_a(t)
        if 1 <= t <= len(items):
            stage_b(t - 1)
        if t >= 2:
            stage_c(t - 2)


def _attention(qkv, qkv_s, tv1, tv2):
    bsz, s, d3 = qkv.shape
    d = d3 // 3
    width = HEAD_GROUPS * LANES
    heads = width // HEAD_DIM
    n_blocks = d // width
    sub = s // STRIDE
    col = lambda off: pl.BlockSpec((None, s, width), lambda hb, b: (b, 0, off + hb))
    col_s = lambda off: pl.BlockSpec((None, STRIDE, sub, width),
                                     lambda hb, b: (b, 0, 0, off + hb))
    bias_row = lambda tv: pl.BlockSpec((heads, 1, tv.shape[2]), lambda hb, b: (hb, 0, 0))
    wide = 2 * ATT_TILE
    assert heads * NEAR == wide
    return pl.pallas_call(
        _attn_body,
        grid=(n_blocks, bsz),
        in_specs=[bias_row(tv1), bias_row(tv2),
                  col(0), col(n_blocks), col(2 * n_blocks),
                  col_s(0), col_s(n_blocks), col_s(2 * n_blocks)],
        out_specs=col(0),
        out_shape=jax.ShapeDtypeStruct((bsz, s, d), BF16),
        scratch_shapes=[pltpu.VMEM((2 * NEAR, wide), F32),
                        pltpu.VMEM((NEAR, wide), F32),
                        pltpu.VMEM((HEAD_GROUPS, sub, wide), F32),
                        pltpu.VMEM((HEAD_GROUPS, LANES + ONES_ROWS, s), BF16),
                        pltpu.VMEM((HEAD_GROUPS, STRIDE, LANES + ONES_ROWS, sub), BF16),
                        pltpu.VMEM((HEAD_GROUPS, s, LANES), F32),
                        pltpu.VMEM((HEAD_GROUPS, s, LANES), F32)],
        compiler_params=_params(2),
        name="dilated_attention",
    )(tv1, tv2, qkv, qkv, qkv, qkv_s, qkv_s, qkv_s)


def kernel(x, mix_norm, ffn_norm, final_norm, conv_w_in, conv_kernel, conv_w_out,
           attn_w_qkv, attn_w_out, rel_bias, ffn_w_gate, ffn_w_up, ffn_w_down):
    bsz, s, d = x.shape
    depth = mix_norm.shape[0]
    assert d == N_HEADS * HEAD_DIM and s % ROW_TILE == 0 and ROW_TILE % ATT_TILE == 0
    near = tuple((w, dil) for w, dil in BRANCHES if dil == 1)
    far = tuple((w, dil) for w, dil in BRANCHES if dil > 1)
    assert all(w <= NEAR for w, _ in near) and all(dil % STRIDE == 0 for _, dil in far)
    assert s % (STRIDE * ATT_TILE) == 0 and ROW_TILE % STRIDE == 0

    tv1 = _bias_rows(rel_bias, NEAR + 1, 1, near, 3 * ATT_TILE)
    tv2 = _bias_rows(rel_bias, s // STRIDE, STRIDE, far, s // STRIDE + ATT_TILE)
    gf = final_norm.reshape(1, d)

    def layer_params(i):
        mix_in, mix_out = (conv_w_in, conv_w_out) if i % 2 == 0 else (attn_w_qkv, attn_w_out)
        return [(mix_in, i // 2), (mix_out, i // 2),
                (ffn_w_gate, i), (ffn_w_up, i), (ffn_w_down, i)]

    mix_in = conv_w_in[0].astype(BF16)
    rest = None
    x2 = x.reshape(bsz * s, d)
    for i in range(depth):
        g_mix = mix_norm[i].reshape(1, d)
        if i % 2 == 0:
            jobs = layer_params(i)[1:] if i == 0 else []
            z, made = _conv_pre(x2, g_mix, mix_in, conv_kernel, i // 2, s, jobs)
            rest = made if i == 0 else rest
        else:
            qkv, qkv_s = _qkv_proj(x2, g_mix, mix_in, s)
            z = _attention(qkv.reshape(bsz, s, 3 * d), qkv_s, tv1, tv2).reshape(bsz * s, d)
        mix_out, w_g, w_u, w_d = rest
        jobs = layer_params(i + 1) if i + 1 < depth else []
        x2, made = _mix_ffn(x2, z, mix_out, ffn_norm[i].reshape(1, d), w_g, w_u, w_d, gf,
                            final=(i == depth - 1), cast_jobs=jobs)
        if made:
            mix_in, rest = made[0], made[1:]
    return x2.reshape(bsz, s, d)
```

```python
import functools
import math

import jax
import jax.numpy as jnp
from jax import lax
from jax.experimental import pallas as pl
from jax.experimental.pallas import tpu as pltpu

N_HEADS = 16
HEAD_DIM = 64
CONV_WIDTH = 3
BRANCHES = ((128, 1), (512, 4), (2048, 16))
NUM_BUCKETS = 32
MAX_DISTANCE = 2048
EPS = 1e-6
NEG_INF = -1e30
LOG2E = math.log2(math.e)

LANES = 128
ROW_TILE = 1024
COL_CHUNK = 512
FF_CHUNK = 256
PAIR = LANES // HEAD_DIM
ATT_TILE = 256
HEAD_GROUPS = 2
STRIDE = 4
NEAR = ATT_TILE // 2
F32_SUBLANES = 8
BF16_SUBLANES = 16
ONES_ROWS = BF16_SUBLANES
VMEM_LIMIT = 56 * 1024 * 1024

F32 = jnp.float32
BF16 = jnp.bfloat16
NT_DIMS = (((1,), (1,)), ((), ()))


def _rms(x, g):
    ms = jnp.mean(x * x, axis=-1, keepdims=True)
    return x * lax.rsqrt(ms + EPS) * g


def _const_spec(shape):
    nd = len(shape)
    return pl.BlockSpec(shape, lambda *_: (0,) * nd, pipeline_mode=pl.Buffered(1))


def _layer_spec(stacked, layer):
    tail = (0,) * (stacked.ndim - 1)
    return pl.BlockSpec((None,) + stacked.shape[1:], lambda *_: (layer,) + tail,
                        pipeline_mode=pl.Buffered(1))


def _cast_io(jobs, steps):
    in_specs, out_specs, out_shapes = [], [], []
    for w, layer in jobs:
        rows, cols = w.shape[1] // steps, w.shape[2]
        assert w.shape[1] % steps == 0 and rows % BF16_SUBLANES == 0
        in_specs.append(pl.BlockSpec((None, rows, cols), lambda i, layer=layer: (layer, i, 0)))
        out_specs.append(pl.BlockSpec((rows, cols), lambda i: (i, 0)))
        out_shapes.append(jax.ShapeDtypeStruct(w.shape[1:], BF16))
    return in_specs, out_specs, out_shapes


def _split_refs(refs, n_in, n_out, n_cast):
    a, b, c = n_in + n_cast, n_in + n_cast + n_out, n_in + 2 * n_cast + n_out
    return refs[:n_in], refs[n_in:a], refs[a:b], refs[b:c], refs[c:]


def _run_casts(srcs, dsts):
    for src, dst in zip(srcs, dsts):
        dst[...] = src[...].astype(BF16)


def _params(n_axes):
    return pltpu.CompilerParams(
        dimension_semantics=("arbitrary",) * n_axes, vmem_limit_bytes=VMEM_LIMIT)


def _qkv_body(x_ref, g_ref, w_ref, o_ref, os_ref, slab0_ref, slab1_ref):
    tm, d = x_ref.shape
    h = _rms(x_ref[...], g_ref[...]).astype(BF16)
    q_scale = LOG2E * HEAD_DIM ** -0.5
    for j in range(0, w_ref.shape[1], COL_CHUNK):
        slab_ref = (slab0_ref, slab1_ref)[(j // COL_CHUNK) % 2]
        y = jnp.dot(h, w_ref[:, j:j + COL_CHUNK], preferred_element_type=F32)
        if j < d:
            y = y * q_scale
        o_ref[:, j:j + COL_CHUNK] = y.astype(BF16)
        for c in range(COL_CHUNK // LANES):
            slab_ref[c] = y[:, c * LANES:(c + 1) * LANES]
        for r in range(STRIDE):
            for c in range(COL_CHUNK // LANES):
                part = slab_ref[c, pl.ds(r, tm // STRIDE, stride=STRIDE), :]
                os_ref[r, :, j + c * LANES:j + (c + 1) * LANES] = part.astype(BF16)


def _qkv_proj(x2, g, w, seq):
    t, d = x2.shape
    n = w.shape[1]
    tm = ROW_TILE
    per_seq = seq // tm
    return pl.pallas_call(
        _qkv_body,
        grid=(t // tm,),
        in_specs=[pl.BlockSpec((tm, d), lambda i: (i, 0)),
                  _const_spec((1, d)), _const_spec(w.shape)],
        out_specs=[pl.BlockSpec((tm, n), lambda i: (i, 0)),
                   pl.BlockSpec((None, STRIDE, tm // STRIDE, n),
                                lambda i: (i // per_seq, 0, i % per_seq, 0))],
        out_shape=[jax.ShapeDtypeStruct((t, n), BF16),
                   jax.ShapeDtypeStruct((t // seq, STRIDE, seq // STRIDE, n), BF16)],
        scratch_shapes=[pltpu.VMEM((COL_CHUNK // LANES, tm, LANES), F32)] * 2,
        compiler_params=_params(1),
        name="qkv_proj",
    )(x2, g, w)


def _conv_body(*refs, n_cast, per_seq):
    (x_ref, g_ref, w_ref, k_ref), cast_src, (z_ref,), cast_dst, (cu_ref,) = _split_refs(
        refs, 4, 1, n_cast)
    _run_casts(cast_src, cast_dst)
    tm, d = x_ref.shape
    pad = F32_SUBLANES
    first = pl.program_id(0) % per_seq == 0

    @pl.when(first)
    def _():
        cu_ref[0:pad, :] = jnp.zeros((pad, d), F32)

    @pl.when(jnp.logical_not(first))
    def _():
        cu_ref[0:pad, :] = cu_ref[tm:tm + pad, :]

    h = _rms(x_ref[...], g_ref[...]).astype(BF16)
    for j in range(0, d, COL_CHUNK):
        cols = slice(j, j + COL_CHUNK)
        c = jnp.dot(h, w_ref[:, d + j:d + j + COL_CHUNK], preferred_element_type=F32)
        u = jnp.dot(h, w_ref[:, 2 * d + j:2 * d + j + COL_CHUNK], preferred_element_type=F32)
        cu_ref[pad:pad + tm, cols] = c * u
        y = None
        for w in reversed(range(CONV_WIDTH)):
            start = pad - (CONV_WIDTH - 1 - w)
            term = k_ref[w:w + 1, cols] * cu_ref[start:start + tm, cols]
            y = term if y is None else y + term
        b = jnp.dot(h, w_ref[:, cols], preferred_element_type=F32)
        z_ref[:, cols] = (b * y).astype(BF16)


def _conv_pre(x2, g, w_in, taps, layer, seq, cast_jobs):
    t, d = x2.shape
    tm = ROW_TILE
    steps = t // tm
    cast_in, cast_out, cast_shapes = _cast_io(cast_jobs, steps)
    row = pl.BlockSpec((tm, d), lambda i: (i, 0))
    outs = pl.pallas_call(
        functools.partial(_conv_body, n_cast=len(cast_jobs), per_seq=seq // tm),
        grid=(steps,),
        in_specs=[row, _const_spec((1, d)), _const_spec(w_in.shape), _layer_spec(taps, layer)]
        + cast_in,
        out_specs=[row] + cast_out,
        out_shape=[jax.ShapeDtypeStruct((t, d), BF16)] + cast_shapes,
        scratch_shapes=[pltpu.VMEM((tm + F32_SUBLANES, d), F32)],
        compiler_params=_params(1),
        name="conv_pre",
    )(x2, g, w_in, taps, *[w for w, _ in cast_jobs])
    return outs[0], outs[1:]


def _ffn_body(*refs, final, n_cast):
    ins, cast_src, (o_ref,), cast_dst, _ = _split_refs(refs, 8, 1, n_cast)
    x_ref, z_ref, wo_ref, g_ref, wg_ref, wu_ref, wd_ref, gf_ref = ins
    _run_casts(cast_src, cast_dst)
    x1 = x_ref[...] + jnp.dot(z_ref[...], wo_ref[...], preferred_element_type=F32)
    h = _rms(x1, g_ref[...]).astype(BF16)
    o_ref[...] = x1
    for c in range(0, wg_ref.shape[1], FF_CHUNK):
        gate = jnp.dot(h, wg_ref[:, c:c + FF_CHUNK], preferred_element_type=F32)
        up = jnp.dot(h, wu_ref[:, c:c + FF_CHUNK], preferred_element_type=F32)
        a = (gate * jax.nn.sigmoid(gate) * up).astype(BF16)
        o_ref[...] += jnp.dot(a, wd_ref[c:c + FF_CHUNK, :], preferred_element_type=F32)
    if final:
        o_ref[...] = _rms(o_ref[...], gf_ref[...])


def _mix_ffn(x2, z2, wo, g, wg, wu, wd, gf, final, cast_jobs):
    t, d = x2.shape
    steps = t // ROW_TILE
    cast_in, cast_out, cast_shapes = _cast_io(cast_jobs, steps)
    row = pl.BlockSpec((ROW_TILE, d), lambda i: (i, 0))
    outs = pl.pallas_call(
        functools.partial(_ffn_body, final=final, n_cast=len(cast_jobs)),
        grid=(steps,),
        in_specs=[row, row, _const_spec(wo.shape), _const_spec((1, d)), _const_spec(wg.shape),
                  _const_spec(wu.shape), _const_spec(wd.shape), _const_spec((1, d))] + cast_in,
        out_specs=[row] + cast_out,
        out_shape=[jax.ShapeDtypeStruct((t, d), F32)] + cast_shapes,
        compiler_params=_params(1),
        name="mix_ffn_final" if final else "mix_ffn",
    )(x2, z2, wo, g, wg, wu, wd, gf, *[w for w, _ in cast_jobs])
    return outs[0], outs[1:]


def _t5_bucket(dist):
    exact = NUM_BUCKETS // 2
    df = jnp.maximum(dist, 1).astype(F32)
    large = exact + (jnp.log(df / exact) / math.log(MAX_DISTANCE / exact)
                     * (NUM_BUCKETS - exact)).astype(jnp.int32)
    large = jnp.minimum(large, NUM_BUCKETS - 1)
    return jnp.where(dist < exact, dist, large)


def _bias_rows(rel_bias, n_dist, stride, branches, period):
    dist = jnp.arange(n_dist, dtype=jnp.int32) * stride
    mult = jnp.zeros((n_dist,), F32)
    for window, dilation in branches:
        mult = mult + ((dist % dilation == 0) & (dist <= window)).astype(F32)
    table = rel_bias[_t5_bucket(dist)].astype(F32)
    total = jnp.where(mult[:, None] > 0,
                      (table + jnp.log(jnp.maximum(mult, 1.0))[:, None]) * LOG2E, NEG_INF)
    pad = jnp.full((period - n_dist, total.shape[1]), NEG_INF, F32)
    return jnp.concatenate([total, pad], axis=0).T[:, None, :]


def _attn_body(tv1_ref, tv2_ref, q_ref, k_ref, v_ref, qs_ref, ks_ref, vs_ref, o_ref,
               bias1_ref, bias10_ref, bias2_ref, lhs_ref, lhss_ref, o2_ref, lse2_ref):
    tq = ATT_TILE
    seq, width = k_ref.shape
    sub = ks_ref.shape[1]
    n_sub = sub // tq
    groups = width // LANES
    heads = width // HEAD_DIM

    @pl.when(pl.program_id(1) == 0)
    def _():
        for h in range(heads):
            row = jnp.broadcast_to(tv1_ref[h], (2 * NEAR, tv1_ref.shape[2]))
            toep = pltpu.roll(row, 0, 1, stride=1, stride_axis=0)
            bias10_ref[:, h * NEAR:(h + 1) * NEAR] = toep[:NEAR, :NEAR]
            bias1_ref[:, h * NEAR:(h + 1) * NEAR] = toep[:, NEAR:2 * NEAR]
            g, hg = divmod(h, PAIR)
            cols = slice(hg * tq, (hg + 1) * tq)
            row = jnp.broadcast_to(tv2_ref[h], (tq, tv2_ref.shape[2]))
            toep = pltpu.roll(row, 0, 1, stride=1, stride_axis=0)
            for dist in range(n_sub):
                start = (n_sub - 1 - dist) * tq
                bias2_ref[g, start:start + tq, cols] = toep[:, dist * tq:(dist + 1) * tq]

    for g in range(groups):
        lanes = slice(g * LANES, (g + 1) * LANES)
        for kb in range(seq // tq):
            lhs_ref[g, :LANES, kb * tq:(kb + 1) * tq] = v_ref[kb * tq:(kb + 1) * tq, lanes].T
        lhs_ref[g, LANES:, :] = jnp.ones((ONES_ROWS, seq), BF16)
        for r in range(STRIDE):
            for kb in range(n_sub):
                lhss_ref[g, r, :LANES, kb * tq:(kb + 1) * tq] = (
                    vs_ref[r, kb * tq:(kb + 1) * tq, lanes].T)
            lhss_ref[g, r, LANES:, :] = jnp.ones((ONES_ROWS, sub), BF16)

    lane_head = lax.broadcasted_iota(jnp.int32, (1, width), 1) // HEAD_DIM
    items = [(g, r, j, 0, (j + 1) * tq)
             for g in range(groups) for r in range(STRIDE) for j in range(n_sub)]
    items += [(None, None, qb, max(0, (qb - 1) * NEAR), (qb + 1) * NEAR)
              for qb in range(seq // NEAR)]
    col_max, scores, probs = {}, {}, {}
    half = tq // 2
    late_cols = (half, tq + half)

    def per_head_rows(q):
        zero = jnp.zeros_like(q)
        lanes = lane_head[:, :q.shape[1]]
        return jnp.concatenate(
            [jnp.where(lanes == h, q, zero) for h in range(q.shape[1] // HEAD_DIM)], axis=0)

    def stage_a(t):
        g, r, qb, lo, hi = items[t]
        if g is None:
            q = per_head_rows(q_ref[qb * NEAR:(qb + 1) * NEAR, :])
            k = k_ref[lo:hi, :]
            bias = bias10_ref[...] if lo == qb * NEAR else bias1_ref[...]
        else:
            lanes = slice(g * LANES, (g + 1) * LANES)
            q = per_head_rows(qs_ref[r, qb * tq:(qb + 1) * tq, lanes])
            k = ks_ref[r, lo:hi, lanes]
            bias = bias2_ref[g, sub - hi:, :]
        s = lax.dot_general(k, q, NT_DIMS, preferred_element_type=F32) + bias
        if g is None:
            s = s.astype(BF16)
            scores[t] = (s, None)
            col_max[t] = jnp.max(s, axis=0, keepdims=True)
        else:
            old = s[:hi - lo - half].astype(BF16)
            new = [s[hi - lo - half:, c:c + half].astype(BF16) for c in late_cols]
            scores[t] = (old, new)
            m_old = jnp.max(old, axis=0, keepdims=True)
            m_new = [jnp.max(x, axis=0, keepdims=True) for x in new]
            col_max[t] = jnp.concatenate(
                [m_old[:, :half], jnp.maximum(m_old[:, half:tq], m_new[0]),
                 m_old[:, tq:tq + half], jnp.maximum(m_old[:, tq + half:], m_new[1])], axis=1)

    def stage_b(t):
        old, new = scores.pop(t)
        m = col_max[t]
        p = jnp.exp2(old - m)
        if new is not None:
            zero = jnp.zeros((half, half), BF16)
            p_new = [jnp.exp2(x - m[:, c:c + half]) for x, c in zip(new, late_cols)]
            p = jnp.concatenate(
                [p, jnp.concatenate([zero, p_new[0], zero, p_new[1]], axis=1)], axis=0)
        probs[t] = p

    def stage_c(t):
        g, r, qb, lo, hi = items[t]
        p, m = probs.pop(t), col_max.pop(t).astype(F32)
        if g is None:
            parts = [(jnp.dot(lhs_ref[i, :, lo:hi], p[:, i * tq:(i + 1) * tq],
                              preferred_element_type=F32), m[:, i * tq:(i + 1) * tq])
                     for i in range(groups)]
        else:
            parts = [(jnp.dot(lhss_ref[g, r, :, lo:hi], p, preferred_element_type=F32), m)]
        outs, lses = [], []
        for acc, m_part in parts:
            nq = acc.shape[1] // PAIR
            den = acc[LANES:LANES + 1]
            lse = m_part + jnp.log2(den)
            for h in range(PAIR):
                cols = slice(h * nq, (h + 1) * nq)
                outs.append(acc[h * HEAD_DIM:(h + 1) * HEAD_DIM, cols] / den[:, cols])
                lses.append(jnp.broadcast_to(lse[:, cols], (HEAD_DIM, nq)))
        out_t, lse_t = jnp.concatenate(outs, axis=0), jnp.concatenate(lses, axis=0)
        out, lse = out_t.T, lse_t.T
        if g is not None:
            token_rows = pl.ds(qb * tq * STRIDE + r, tq, stride=STRIDE)
            o2_ref[g, token_rows, :] = out
            lse2_ref[g, token_rows, :] = lse
        else:
            rows = slice(qb * NEAR, (qb + 1) * NEAR)
            out2 = jnp.concatenate([o2_ref[i, rows, :] for i in range(groups)], axis=1)
            lse2 = jnp.concatenate([lse2_ref[i, rows, :] for i in range(groups)], axis=1)
            top = jnp.maximum(lse, lse2)
            w1, w2 = jnp.exp2(lse - top), jnp.exp2(lse2 - top)
            o_ref[rows, :] = ((w1 * out + w2 * out2) / (w1 + w2)).astype(BF16)

    for t in range(len(items) + 2):
        if t < len(items):
            stage_a(t)
        if 1 <= t <= len(items):
            stage_b(t - 1)
        if t >= 2:
            stage_c(t - 2)


def _attention(qkv, qkv_s, tv1, tv2):
    bsz, s, d3 = qkv.shape
    d = d3 // 3
    width = HEAD_GROUPS * LANES
    heads = width // HEAD_DIM
    n_blocks = d // width
    sub = s // STRIDE
    col = lambda off: pl.BlockSpec((None, s, width), lambda hb, b: (b, 0, off + hb))
    col_s = lambda off: pl.BlockSpec((None, STRIDE, sub, width),
                                     lambda hb, b: (b, 0, 0, off + hb))
    bias_row = lambda tv: pl.BlockSpec((heads, 1, tv.shape[2]), lambda hb, b: (hb, 0, 0))
    wide = 2 * ATT_TILE
    assert heads * NEAR == wide
    return pl.pallas_call(
        _attn_body,
        grid=(n_blocks, bsz),
        in_specs=[bias_row(tv1), bias_row(tv2),
                  col(0), col(n_blocks), col(2 * n_blocks),
                  col_s(0), col_s(n_blocks), col_s(2 * n_blocks)],
        out_specs=col(0),
        out_shape=jax.ShapeDtypeStruct((bsz, s, d), BF16),
        scratch_shapes=[pltpu.VMEM((2 * NEAR, wide), F32),
                        pltpu.VMEM((NEAR, wide), F32),
                        pltpu.VMEM((HEAD_GROUPS, sub, wide), F32),
                        pltpu.VMEM((HEAD_GROUPS, LANES + ONES_ROWS, s), BF16),
                        pltpu.VMEM((HEAD_GROUPS, STRIDE, LANES + ONES_ROWS, sub), BF16),
                        pltpu.VMEM((HEAD_GROUPS, s, LANES), F32),
                        pltpu.VMEM((HEAD_GROUPS, s, LANES), F32)],
        compiler_params=_params(2),
        name="dilated_attention",
    )(tv1, tv2, qkv, qkv, qkv, qkv_s, qkv_s, qkv_s)


def kernel(x, mix_norm, ffn_norm, final_norm, conv_w_in, conv_kernel, conv_w_out,
           attn_w_qkv, attn_w_out, rel_bias, ffn_w_gate, ffn_w_up, ffn_w_down):
    bsz, s, d = x.shape
    depth = mix_norm.shape[0]
    assert d == N_HEADS * HEAD_DIM and s % ROW_TILE == 0 and ROW_TILE % ATT_TILE == 0
    near = tuple((w, dil) for w, dil in BRANCHES if dil == 1)
    far = tuple((w, dil) for w, dil in BRANCHES if dil > 1)
    assert all(w <= NEAR for w, _ in near) and all(dil % STRIDE == 0 for _, dil in far)
    assert s % (STRIDE * ATT_TILE) == 0 and ROW_TILE % STRIDE == 0

    tv1 = _bias_rows(rel_bias, NEAR + 1, 1, near, 3 * ATT_TILE)
    tv2 = _bias_rows(rel_bias, s // STRIDE, STRIDE, far, s // STRIDE + ATT_TILE)
    gf = final_norm.reshape(1, d)

    def layer_params(i):
        mix_in, mix_out = (conv_w_in, conv_w_out) if i % 2 == 0 else (attn_w_qkv, attn_w_out)
        return [(mix_in, i // 2), (mix_out, i // 2),
                (ffn_w_gate, i), (ffn_w_up, i), (ffn_w_down, i)]

    mix_in = conv_w_in[0].astype(BF16)
    rest = None
    x2 = x.reshape(bsz * s, d)
    for i in range(depth):
        g_mix = mix_norm[i].reshape(1, d)
        if i % 2 == 0:
            jobs = layer_params(i)[1:] if i == 0 else []
            z, made = _conv_pre(x2, g_mix, mix_in, conv_kernel, i // 2, s, jobs)
            rest = made if i == 0 else rest
        else:
            qkv, qkv_s = _qkv_proj(x2, g_mix, mix_in, s)
            z = _attention(qkv.reshape(bsz, s, 3 * d), qkv_s, tv1, tv2).reshape(bsz * s, d)
        mix_out, w_g, w_u, w_d = rest
        jobs = layer_params(i + 1) if i + 1 < depth else []
        x2, made = _mix_ffn(x2, z, mix_out, ffn_norm[i].reshape(1, d), w_g, w_u, w_d, gf,
                            final=(i == depth - 1), cast_jobs=jobs)
        if made:
            mix_in, rest = made[0], made[1:]
    return x2.reshape(bsz, s, d)
```

```python
import functools
import math

import jax
import jax.numpy as jnp
from jax import lax
from jax.experimental import pallas as pl
from jax.experimental.pallas import tpu as pltpu

N_HEADS = 16
HEAD_DIM = 64
CONV_WIDTH = 3
BRANCHES = ((128, 1), (512, 4), (2048, 16))
NUM_BUCKETS = 32
MAX_DISTANCE = 2048
EPS = 1e-6
NEG_INF = -1e30
LOG2E = math.log2(math.e)

LANES = 128
ROW_TILE = 1024
COL_CHUNK = 512
FF_CHUNK = 256
PAIR = LANES // HEAD_DIM
ATT_TILE = 256
HEAD_GROUPS = 2
STRIDE = 4
NEAR = ATT_TILE // 2
F32_SUBLANES = 8
BF16_SUBLANES = 16
ONES_ROWS = BF16_SUBLANES
VMEM_LIMIT = 56 * 1024 * 1024

F32 = jnp.float32
BF16 = jnp.bfloat16
NT_DIMS = (((1,), (1,)), ((), ()))


def _rms(x, g):
    ms = jnp.mean(x * x, axis=-1, keepdims=True)
    return x * lax.rsqrt(ms + EPS) * g


def _const_spec(shape):
    nd = len(shape)
    return pl.BlockSpec(shape, lambda *_: (0,) * nd, pipeline_mode=pl.Buffered(1))


def _layer_spec(stacked, layer):
    tail = (0,) * (stacked.ndim - 1)
    return pl.BlockSpec((None,) + stacked.shape[1:], lambda *_: (layer,) + tail,
                        pipeline_mode=pl.Buffered(1))


def _cast_io(jobs, steps):
    in_specs, out_specs, out_shapes = [], [], []
    for w, layer in jobs:
        rows, cols = w.shape[1] // steps, w.shape[2]
        assert w.shape[1] % steps == 0 and rows % BF16_SUBLANES == 0
        in_specs.append(pl.BlockSpec((None, rows, cols), lambda i, layer=layer: (layer, i, 0)))
        out_specs.append(pl.BlockSpec((rows, cols), lambda i: (i, 0)))
        out_shapes.append(jax.ShapeDtypeStruct(w.shape[1:], BF16))
    return in_specs, out_specs, out_shapes


def _split_refs(refs, n_in, n_out, n_cast):
    a, b, c = n_in + n_cast, n_in + n_cast + n_out, n_in + 2 * n_cast + n_out
    return refs[:n_in], refs[n_in:a], refs[a:b], refs[b:c], refs[c:]


def _run_casts(srcs, dsts):
    for src, dst in zip(srcs, dsts):
        dst[...] = src[...].astype(BF16)


def _params(n_axes):
    return pltpu.CompilerParams(
        dimension_semantics=("arbitrary",) * n_axes, vmem_limit_bytes=VMEM_LIMIT)


def _qkv_body(x_ref, g_ref, w_ref, o_ref, os_ref, slab0_ref, slab1_ref):
    tm, d = x_ref.shape
    h = _rms(x_ref[...], g_ref[...]).astype(BF16)
    q_scale = LOG2E * HEAD_DIM ** -0.5
    for j in range(0, w_ref.shape[1], COL_CHUNK):
        slab_ref = (slab0_ref, slab1_ref)[(j // COL_CHUNK) % 2]
        y = jnp.dot(h, w_ref[:, j:j + COL_CHUNK], preferred_element_type=F32)
        if j < d:
            y = y * q_scale
        o_ref[:, j:j + COL_CHUNK] = y.astype(BF16)
        for c in range(COL_CHUNK // LANES):
            slab_ref[c] = y[:, c * LANES:(c + 1) * LANES]
        for r in range(STRIDE):
            for c in range(COL_CHUNK // LANES):
                part = slab_ref[c, pl.ds(r, tm // STRIDE, stride=STRIDE), :]
                os_ref[r, :, j + c * LANES:j + (c + 1) * LANES] = part.astype(BF16)


def _qkv_proj(x2, g, w, seq):
    t, d = x2.shape
    n = w.shape[1]
    tm = ROW_TILE
    per_seq = seq // tm
    return pl.pallas_call(
        _qkv_body,
        grid=(t // tm,),
        in_specs=[pl.BlockSpec((tm, d), lambda i: (i, 0)),
                  _const_spec((1, d)), _const_spec(w.shape)],
        out_specs=[pl.BlockSpec((tm, n), lambda i: (i, 0)),
                   pl.BlockSpec((None, STRIDE, tm // STRIDE, n),
                                lambda i: (i // per_seq, 0, i % per_seq, 0))],
        out_shape=[jax.ShapeDtypeStruct((t, n), BF16),
                   jax.ShapeDtypeStruct((t // seq, STRIDE, seq // STRIDE, n), BF16)],
        scratch_shapes=[pltpu.VMEM((COL_CHUNK // LANES, tm, LANES), F32)] * 2,
        compiler_params=_params(1),
        name="qkv_proj",
    )(x2, g, w)


def _conv_body(*refs, n_cast, per_seq):
    (x_ref, g_ref, w_ref, k_ref), cast_src, (z_ref,), cast_dst, (cu_ref,) = _split_refs(
        refs, 4, 1, n_cast)
    _run_casts(cast_src, cast_dst)
    tm, d = x_ref.shape
    pad = F32_SUBLANES
    first = pl.program_id(0) % per_seq == 0

    @pl.when(first)
    def _():
        cu_ref[0:pad, :] = jnp.zeros((pad, d), F32)

    @pl.when(jnp.logical_not(first))
    def _():
        cu_ref[0:pad, :] = cu_ref[tm:tm + pad, :]

    h = _rms(x_ref[...], g_ref[...]).astype(BF16)
    for j in range(0, d, COL_CHUNK):
        cols = slice(j, j + COL_CHUNK)
        c = jnp.dot(h, w_ref[:, d + j:d + j + COL_CHUNK], preferred_element_type=F32)
        u = jnp.dot(h, w_ref[:, 2 * d + j:2 * d + j + COL_CHUNK], preferred_element_type=F32)
        cu_ref[pad:pad + tm, cols] = c * u
        y = None
        for w in reversed(range(CONV_WIDTH)):
            start = pad - (CONV_WIDTH - 1 - w)
            term = k_ref[w:w + 1, cols] * cu_ref[start:start + tm, cols]
            y = term if y is None else y + term
        b = jnp.dot(h, w_ref[:, cols], preferred_element_type=F32)
        z_ref[:, cols] = (b * y).astype(BF16)


def _conv_pre(x2, g, w_in, taps, layer, seq, cast_jobs):
    t, d = x2.shape
    tm = ROW_TILE
    steps = t // tm
    cast_in, cast_out, cast_shapes = _cast_io(cast_jobs, steps)
    row = pl.BlockSpec((tm, d), lambda i: (i, 0))
    outs = pl.pallas_call(
        functools.partial(_conv_body, n_cast=len(cast_jobs), per_seq=seq // tm),
        grid=(steps,),
        in_specs=[row, _const_spec((1, d)), _const_spec(w_in.shape), _layer_spec(taps, layer)]
        + cast_in,
        out_specs=[row] + cast_out,
        out_shape=[jax.ShapeDtypeStruct((t, d), BF16)] + cast_shapes,
        scratch_shapes=[pltpu.VMEM((tm + F32_SUBLANES, d), F32)],
        compiler_params=_params(1),
        name="conv_pre",
    )(x2, g, w_in, taps, *[w for w, _ in cast_jobs])
    return outs[0], outs[1:]


def _ffn_body(*refs, final, n_cast):
    ins, cast_src, (o_ref,), cast_dst, _ = _split_refs(refs, 8, 1, n_cast)
    x_ref, z_ref, wo_ref, g_ref, wg_ref, wu_ref, wd_ref, gf_ref = ins
    _run_casts(cast_src, cast_dst)
    x1 = x_ref[...] + jnp.dot(z_ref[...], wo_ref[...], preferred_element_type=F32)
    h = _rms(x1, g_ref[...]).astype(BF16)
    o_ref[...] = x1
    for c in range(0, wg_ref.shape[1], FF_CHUNK):
        gate = jnp.dot(h, wg_ref[:, c:c + FF_CHUNK], preferred_element_type=F32)
        up = jnp.dot(h, wu_ref[:, c:c + FF_CHUNK], preferred_element_type=F32)
        a = (gate * jax.nn.sigmoid(gate) * up).astype(BF16)
        o_ref[...] += jnp.dot(a, wd_ref[c:c + FF_CHUNK, :], preferred_element_type=F32)
    if final:
        o_ref[...] = _rms(o_ref[...], gf_ref[...])


def _mix_ffn(x2, z2, wo, g, wg, wu, wd, gf, final, cast_jobs):
    t, d = x2.shape
    steps = t // ROW_TILE
    cast_in, cast_out, cast_shapes = _cast_io(cast_jobs, steps)
    row = pl.BlockSpec((ROW_TILE, d), lambda i: (i, 0))
    outs = pl.pallas_call(
        functools.partial(_ffn_body, final=final, n_cast=len(cast_jobs)),
        grid=(steps,),
        in_specs=[row, row, _const_spec(wo.shape), _const_spec((1, d)), _const_spec(wg.shape),
                  _const_spec(wu.shape), _const_spec(wd.shape), _const_spec((1, d))] + cast_in,
        out_specs=[row] + cast_out,
        out_shape=[jax.ShapeDtypeStruct((t, d), F32)] + cast_shapes,
        compiler_params=_params(1),
        name="mix_ffn_final" if final else "mix_ffn",
    )(x2, z2, wo, g, wg, wu, wd, gf, *[w for w, _ in cast_jobs])
    return outs[0], outs[1:]


def _t5_bucket(dist):
    exact = NUM_BUCKETS // 2
    df = jnp.maximum(dist, 1).astype(F32)
    large = exact + (jnp.log(df / exact) / math.log(MAX_DISTANCE / exact)
                     * (NUM_BUCKETS - exact)).astype(jnp.int32)
    large = jnp.minimum(large, NUM_BUCKETS - 1)
    return jnp.where(dist < exact, dist, large)


def _bias_rows(rel_bias, n_dist, stride, branches, period):
    dist = jnp.arange(n_dist, dtype=jnp.int32) * stride
    mult = jnp.zeros((n_dist,), F32)
    for window, dilation in branches:
        mult = mult + ((dist % dilation == 0) & (dist <= window)).astype(F32)
    table = rel_bias[_t5_bucket(dist)].astype(F32)
    total = jnp.where(mult[:, None] > 0,
                      (table + jnp.log(jnp.maximum(mult, 1.0))[:, None]) * LOG2E, NEG_INF)
    pad = jnp.full((period - n_dist, total.shape[1]), NEG_INF, F32)
    return jnp.concatenate([total, pad], axis=0).T[:, None, :]


def _attn_body(tv1_ref, tv2_ref, q_ref, k_ref, v_ref, qs_ref, ks_ref, vs_ref, o_ref,
               bias1_ref, bias10_ref, bias2_ref, lhs_ref, lhss_ref, o2_ref, lse2_ref):
    tq = ATT_TILE
    seq, width = k_ref.shape
    sub = ks_ref.shape[1]
    n_sub = sub // tq
    groups = width // LANES
    heads = width // HEAD_DIM

    @pl.when(pl.program_id(1) == 0)
    def _():
        for h in range(heads):
            row = jnp.broadcast_to(tv1_ref[h], (2 * NEAR, tv1_ref.shape[2]))
            toep = pltpu.roll(row, 0, 1, stride=1, stride_axis=0)
            bias10_ref[:, h * NEAR:(h + 1) * NEAR] = toep[:NEAR, :NEAR]
            bias1_ref[:, h * NEAR:(h + 1) * NEAR] = toep[:, NEAR:2 * NEAR]
            g, hg = divmod(h, PAIR)
            cols = slice(hg * tq, (hg + 1) * tq)
            row = jnp.broadcast_to(tv2_ref[h], (tq, tv2_ref.shape[2]))
            toep = pltpu.roll(row, 0, 1, stride=1, stride_axis=0)
            for dist in range(n_sub):
                start = (n_sub - 1 - dist) * tq
                bias2_ref[g, start:start + tq, cols] = toep[:, dist * tq:(dist + 1) * tq]

    for g in range(groups):
        lanes = slice(g * LANES, (g + 1) * LANES)
        for kb in range(seq // tq):
            lhs_ref[g, :LANES, kb * tq:(kb + 1) * tq] = v_ref[kb * tq:(kb + 1) * tq, lanes].T
        lhs_ref[g, LANES:, :] = jnp.ones((ONES_ROWS, seq), BF16)
        for r in range(STRIDE):
            for kb in range(n_sub):
                lhss_ref[g, r, :LANES, kb * tq:(kb + 1) * tq] = (
                    vs_ref[r, kb * tq:(kb + 1) * tq, lanes].T)
            lhss_ref[g, r, LANES:, :] = jnp.ones((ONES_ROWS, sub), BF16)

    lane_head = lax.broadcasted_iota(jnp.int32, (1, width), 1) // HEAD_DIM
    items = [(g, r, j, 0, (j + 1) * tq)
             for g in range(groups) for r in range(STRIDE) for j in range(n_sub)]
    items += [(None, None, qb, max(0, (qb - 1) * NEAR), (qb + 1) * NEAR)
              for qb in range(seq // NEAR)]
    col_max, scores, probs = {}, {}, {}
    half = tq // 2
    late_cols = (half, tq + half)

    def per_head_rows(q):
        zero = jnp.zeros_like(q)
        lanes = lane_head[:, :q.shape[1]]
        return jnp.concatenate(
            [jnp.where(lanes == h, q, zero) for h in range(q.shape[1] // HEAD_DIM)], axis=0)

    def stage_a(t):
        g, r, qb, lo, hi = items[t]
        if g is None:
            q = per_head_rows(q_ref[qb * NEAR:(qb + 1) * NEAR, :])
            k = k_ref[lo:hi, :]
            bias = bias10_ref[...] if lo == qb * NEAR else bias1_ref[...]
        else:
            lanes = slice(g * LANES, (g + 1) * LANES)
            q = per_head_rows(qs_ref[r, qb * tq:(qb + 1) * tq, lanes])
            k = ks_ref[r, lo:hi, lanes]
            bias = bias2_ref[g, sub - hi:, :]
        s = lax.dot_general(k, q, NT_DIMS, preferred_element_type=F32) + bias
        if g is None:
            s = s.astype(BF16)
            scores[t] = (s, None)
            col_max[t] = jnp.max(s, axis=0, keepdims=True)
        else:
            old = s[:hi - lo - half].astype(BF16)
            new = [s[hi - lo - half:, c:c + half].astype(BF16) for c in late_cols]
            scores[t] = (old, new)
            m_old = jnp.max(old, axis=0, keepdims=True)
            m_new = [jnp.max(x, axis=0, keepdims=True) for x in new]
            col_max[t] = jnp.concatenate(
                [m_old[:, :half], jnp.maximum(m_old[:, half:tq], m_new[0]),
                 m_old[:, tq:tq + half], jnp.maximum(m_old[:, tq + half:], m_new[1])], axis=1)

    def stage_b(t):
        old, new = scores.pop(t)
        m = col_max[t]
        p = jnp.exp2(old - m)
        if new is not None:
            zero = jnp.zeros((half, half), BF16)
            p_new = [jnp.exp2(x - m[:, c:c + half]) for x, c in zip(new, late_cols)]
            p = jnp.concatenate(
                [p, jnp.concatenate([zero, p_new[0], zero, p_new[1]], axis=1)], axis=0)
        probs[t] = p

    def stage_c(t):
        g, r, qb, lo, hi = items[t]
        p, m = probs.pop(t), col_max.pop(t).astype(F32)
        if g is None:
            parts = [(jnp.dot(lhs_ref[i, :, lo:hi], p[:, i * tq:(i + 1) * tq],
                              preferred_element_type=F32), m[:, i * tq:(i + 1) * tq])
                     for i in range(groups)]
        else:
            parts = [(jnp.dot(lhss_ref[g, r, :, lo:hi], p, preferred_element_type=F32), m)]
        outs, lses = [], []
        for acc, m_part in parts:
            nq = acc.shape[1] // PAIR
            den = acc[LANES:LANES + 1]
            lse = m_part + jnp.log2(den)
            for h in range(PAIR):
                cols = slice(h * nq, (h + 1) * nq)
                outs.append(acc[h * HEAD_DIM:(h + 1) * HEAD_DIM, cols] / den[:, cols])
                lses.append(jnp.broadcast_to(lse[:, cols], (HEAD_DIM, nq)))
        out_t, lse_t = jnp.concatenate(outs, axis=0), jnp.concatenate(lses, axis=0)
        out, lse = out_t.T, lse_t.T
        if g is not None:
            token_rows = pl.ds(qb * tq * STRIDE + r, tq, stride=STRIDE)
            o2_ref[g, token_rows, :] = out
            lse2_ref[g, token_rows, :] = lse
        else:
            rows = slice(qb * NEAR, (qb + 1) * NEAR)
            out2 = jnp.concatenate([o2_ref[i, rows, :] for i in range(groups)], axis=1)
            lse2 = jnp.concatenate([lse2_ref[i, rows, :] for i in range(groups)], axis=1)
            share = 1.0 / (1.0 + jnp.exp2(lse2 - lse))
            o_ref[rows, :] = (out2 + share * (out - out2)).astype(BF16)

    for t in range(len(items) + 2):
        if t < len(items):
            stage_a(t)
        if 1 <= t <= len(items):
            stage_b(t - 1)
        if t >= 2:
            stage_c(t - 2)


def _attention(qkv, qkv_s, tv1, tv2):
    bsz, s, d3 = qkv.shape
    d = d3 // 3
    width = HEAD_GROUPS * LANES
    heads = width // HEAD_DIM
    n_blocks = d // width
    sub = s // STRIDE
    col = lambda off: pl.BlockSpec((None, s, width), lambda hb, b: (b, 0, off + hb))
    col_s = lambda off: pl.BlockSpec((None, STRIDE, sub, width),
                                     lambda hb, b: (b, 0, 0, off + hb))
    bias_row = lambda tv: pl.BlockSpec((heads, 1, tv.shape[2]), lambda hb, b: (hb, 0, 0))
    wide = 2 * ATT_TILE
    assert heads * NEAR == wide
    return pl.pallas_call(
        _attn_body,
        grid=(n_blocks, bsz),
        in_specs=[bias_row(tv1), bias_row(tv2),
                  col(0), col(n_blocks), col(2 * n_blocks),
                  col_s(0), col_s(n_blocks), col_s(2 * n_blocks)],
        out_specs=col(0),
        out_shape=jax.ShapeDtypeStruct((bsz, s, d), BF16),
        scratch_shapes=[pltpu.VMEM((2 * NEAR, wide), F32),
                        pltpu.VMEM((NEAR, wide), F32),
                        pltpu.VMEM((HEAD_GROUPS, sub, wide), F32),
                        pltpu.VMEM((HEAD_GROUPS, LANES + ONES_ROWS, s), BF16),
                        pltpu.VMEM((HEAD_GROUPS, STRIDE, LANES + ONES_ROWS, sub), BF16),
                        pltpu.VMEM((HEAD_GROUPS, s, LANES), F32),
                        pltpu.VMEM((HEAD_GROUPS, s, LANES), F32)],
        compiler_params=_params(2),
        name="dilated_attention",
    )(tv1, tv2, qkv, qkv, qkv, qkv_s, qkv_s, qkv_s)


def kernel(x, mix_norm, ffn_norm, final_norm, conv_w_in, conv_kernel, conv_w_out,
           attn_w_qkv, attn_w_out, rel_bias, ffn_w_gate, ffn_w_up, ffn_w_down):
    bsz, s, d = x.shape
    depth = mix_norm.shape[0]
    assert d == N_HEADS * HEAD_DIM and s % ROW_TILE == 0 and ROW_TILE % ATT_TILE == 0
    near = tuple((w, dil) for w, dil in BRANCHES if dil == 1)
    far = tuple((w, dil) for w, dil in BRANCHES if dil > 1)
    assert all(w <= NEAR for w, _ in near) and all(dil % STRIDE == 0 for _, dil in far)
    assert s % (STRIDE * ATT_TILE) == 0 and ROW_TILE % STRIDE == 0

    tv1 = _bias_rows(rel_bias, NEAR + 1, 1, near, 3 * ATT_TILE)
    tv2 = _bias_rows(rel_bias, s // STRIDE, STRIDE, far, s // STRIDE + ATT_TILE)
    gf = final_norm.reshape(1, d)

    def layer_params(i):
        mix_in, mix_out = (conv_w_in, conv_w_out) if i % 2 == 0 else (attn_w_qkv, attn_w_out)
        return [(mix_in, i // 2), (mix_out, i // 2),
                (ffn_w_gate, i), (ffn_w_up, i), (ffn_w_down, i)]

    mix_in = conv_w_in[0].astype(BF16)
    rest = None
    x2 = x.reshape(bsz * s, d)
    for i in range(depth):
        g_mix = mix_norm[i].reshape(1, d)
        if i % 2 == 0:
            jobs = layer_params(i)[1:] if i == 0 else []
            z, made = _conv_pre(x2, g_mix, mix_in, conv_kernel, i // 2, s, jobs)
            rest = made if i == 0 else rest
        else:
            qkv, qkv_s = _qkv_proj(x2, g_mix, mix_in, s)
            z = _attention(qkv.reshape(bsz, s, 3 * d), qkv_s, tv1, tv2).reshape(bsz * s, d)
        mix_out, w_g, w_u, w_d = rest
        jobs = layer_params(i + 1) if i + 1 < depth else []
        x2, made = _mix_ffn(x2, z, mix_out, ffn_norm[i].reshape(1, d), w_g, w_u, w_d, gf,
                            final=(i == depth - 1), cast_jobs=jobs)
        if made:
            mix_in, rest = made[0], made[1:]
    return x2.reshape(bsz, s, d)
```

```python
import functools
import math

import jax
import jax.numpy as jnp
from jax import lax
from jax.experimental import pallas as pl
from jax.experimental.pallas import tpu as pltpu

N_HEADS = 16
HEAD_DIM = 64
CONV_WIDTH = 3
BRANCHES = ((128, 1), (512, 4), (2048, 16))
NUM_BUCKETS = 32
MAX_DISTANCE = 2048
EPS = 1e-6
NEG_INF = -1e30
LOG2E = math.log2(math.e)

LANES = 128
ROW_TILE = 1024
COL_CHUNK = 512
FF_CHUNK = 256
PAIR = LANES // HEAD_DIM
ATT_TILE = 256
HEAD_GROUPS = 2
STRIDE = 4
NEAR = ATT_TILE // 2
F32_SUBLANES = 8
BF16_SUBLANES = 16
ONES_ROWS = BF16_SUBLANES
VMEM_LIMIT = 56 * 1024 * 1024

F32 = jnp.float32
BF16 = jnp.bfloat16
NT_DIMS = (((1,), (1,)), ((), ()))


def _rms(x, g):
    ms = jnp.mean(x * x, axis=-1, keepdims=True)
    return x * lax.rsqrt(ms + EPS) * g


def _const_spec(shape):
    nd = len(shape)
    return pl.BlockSpec(shape, lambda *_: (0,) * nd, pipeline_mode=pl.Buffered(1))


def _layer_spec(stacked, layer):
    tail = (0,) * (stacked.ndim - 1)
    return pl.BlockSpec((None,) + stacked.shape[1:], lambda *_: (layer,) + tail,
                        pipeline_mode=pl.Buffered(1))


def _cast_io(jobs, steps):
    in_specs, out_specs, out_shapes = [], [], []
    for w, layer in jobs:
        rows, cols = w.shape[1] // steps, w.shape[2]
        assert w.shape[1] % steps == 0 and rows % BF16_SUBLANES == 0
        in_specs.append(pl.BlockSpec((None, rows, cols), lambda i, layer=layer: (layer, i, 0)))
        out_specs.append(pl.BlockSpec((rows, cols), lambda i: (i, 0)))
        out_shapes.append(jax.ShapeDtypeStruct(w.shape[1:], BF16))
    return in_specs, out_specs, out_shapes


def _split_refs(refs, n_in, n_out, n_cast):
    a, b, c = n_in + n_cast, n_in + n_cast + n_out, n_in + 2 * n_cast + n_out
    return refs[:n_in], refs[n_in:a], refs[a:b], refs[b:c], refs[c:]


def _run_casts(srcs, dsts):
    for src, dst in zip(srcs, dsts):
        dst[...] = src[...].astype(BF16)


def _params(n_axes):
    return pltpu.CompilerParams(
        dimension_semantics=("arbitrary",) * n_axes, vmem_limit_bytes=VMEM_LIMIT)


def _qkv_body(x_ref, g_ref, w_ref, o_ref, os_ref, slab0_ref, slab1_ref):
    tm, d = x_ref.shape
    h = _rms(x_ref[...], g_ref[...]).astype(BF16)
    q_scale = LOG2E * HEAD_DIM ** -0.5
    for j in range(0, w_ref.shape[1], COL_CHUNK):
        slab_ref = (slab0_ref, slab1_ref)[(j // COL_CHUNK) % 2]
        y = jnp.dot(h, w_ref[:, j:j + COL_CHUNK], preferred_element_type=F32)
        if j < d:
            y = y * q_scale
        o_ref[:, j:j + COL_CHUNK] = y.astype(BF16)
        for c in range(COL_CHUNK // LANES):
            slab_ref[c] = y[:, c * LANES:(c + 1) * LANES]
        for r in range(STRIDE):
            for c in range(COL_CHUNK // LANES):
                part = slab_ref[c, pl.ds(r, tm // STRIDE, stride=STRIDE), :]
                os_ref[r, :, j + c * LANES:j + (c + 1) * LANES] = part.astype(BF16)


def _qkv_proj(x2, g, w, seq):
    t, d = x2.shape
    n = w.shape[1]
    tm = ROW_TILE
    per_seq = seq // tm
    return pl.pallas_call(
        _qkv_body,
        grid=(t // tm,),
        in_specs=[pl.BlockSpec((tm, d), lambda i: (i, 0)),
                  _const_spec((1, d)), _const_spec(w.shape)],
        out_specs=[pl.BlockSpec((tm, n), lambda i: (i, 0)),
                   pl.BlockSpec((None, STRIDE, tm // STRIDE, n),
                                lambda i: (i // per_seq, 0, i % per_seq, 0))],
        out_shape=[jax.ShapeDtypeStruct((t, n), BF16),
                   jax.ShapeDtypeStruct((t // seq, STRIDE, seq // STRIDE, n), BF16)],
        scratch_shapes=[pltpu.VMEM((COL_CHUNK // LANES, tm, LANES), F32)] * 2,
        compiler_params=_params(1),
        name="qkv_proj",
    )(x2, g, w)


def _conv_body(*refs, n_cast, per_seq):
    (x_ref, g_ref, w_ref, k_ref), cast_src, (z_ref,), cast_dst, (cu_ref,) = _split_refs(
        refs, 4, 1, n_cast)
    _run_casts(cast_src, cast_dst)
    tm, d = x_ref.shape
    pad = F32_SUBLANES
    first = pl.program_id(0) % per_seq == 0

    @pl.when(first)
    def _():
        cu_ref[0:pad, :] = jnp.zeros((pad, d), F32)

    @pl.when(jnp.logical_not(first))
    def _():
        cu_ref[0:pad, :] = cu_ref[tm:tm + pad, :]

    h = _rms(x_ref[...], g_ref[...]).astype(BF16)
    for j in range(0, d, COL_CHUNK):
        cols = slice(j, j + COL_CHUNK)
        c = jnp.dot(h, w_ref[:, d + j:d + j + COL_CHUNK], preferred_element_type=F32)
        u = jnp.dot(h, w_ref[:, 2 * d + j:2 * d + j + COL_CHUNK], preferred_element_type=F32)
        cu_ref[pad:pad + tm, cols] = c * u
        y = None
        for w in reversed(range(CONV_WIDTH)):
            start = pad - (CONV_WIDTH - 1 - w)
            term = k_ref[w:w + 1, cols] * cu_ref[start:start + tm, cols]
            y = term if y is None else y + term
        b = jnp.dot(h, w_ref[:, cols], preferred_element_type=F32)
        z_ref[:, cols] = (b * y).astype(BF16)


def _conv_pre(x2, g, w_in, taps, layer, seq, cast_jobs):
    t, d = x2.shape
    tm = ROW_TILE
    steps = t // tm
    cast_in, cast_out, cast_shapes = _cast_io(cast_jobs, steps)
    row = pl.BlockSpec((tm, d), lambda i: (i, 0))
    outs = pl.pallas_call(
        functools.partial(_conv_body, n_cast=len(cast_jobs), per_seq=seq // tm),
        grid=(steps,),
        in_specs=[row, _const_spec((1, d)), _const_spec(w_in.shape), _layer_spec(taps, layer)]
        + cast_in,
        out_specs=[row] + cast_out,
        out_shape=[jax.ShapeDtypeStruct((t, d), BF16)] + cast_shapes,
        scratch_shapes=[pltpu.VMEM((tm + F32_SUBLANES, d), F32)],
        compiler_params=_params(1),
        name="conv_pre",
    )(x2, g, w_in, taps, *[w for w, _ in cast_jobs])
    return outs[0], outs[1:]


def _ffn_body(*refs, final, n_cast):
    ins, cast_src, (o_ref,), cast_dst, _ = _split_refs(refs, 8, 1, n_cast)
    x_ref, z_ref, wo_ref, g_ref, wg_ref, wu_ref, wd_ref, gf_ref = ins
    _run_casts(cast_src, cast_dst)
    x1 = x_ref[...] + jnp.dot(z_ref[...], wo_ref[...], preferred_element_type=F32)
    h = _rms(x1, g_ref[...]).astype(BF16)
    o_ref[...] = x1
    for c in range(0, wg_ref.shape[1], FF_CHUNK):
        gate = jnp.dot(h, wg_ref[:, c:c + FF_CHUNK], preferred_element_type=F32)
        up = jnp.dot(h, wu_ref[:, c:c + FF_CHUNK], preferred_element_type=F32)
        a = (gate * jax.nn.sigmoid(gate) * up).astype(BF16)
        o_ref[...] += jnp.dot(a, wd_ref[c:c + FF_CHUNK, :], preferred_element_type=F32)
    if final:
        o_ref[...] = _rms(o_ref[...], gf_ref[...])


def _mix_ffn(x2, z2, wo, g, wg, wu, wd, gf, final, cast_jobs):
    t, d = x2.shape
    steps = t // ROW_TILE
    cast_in, cast_out, cast_shapes = _cast_io(cast_jobs, steps)
    row = pl.BlockSpec((ROW_TILE, d), lambda i: (i, 0))
    outs = pl.pallas_call(
        functools.partial(_ffn_body, final=final, n_cast=len(cast_jobs)),
        grid=(steps,),
        in_specs=[row, row, _const_spec(wo.shape), _const_spec((1, d)), _const_spec(wg.shape),
                  _const_spec(wu.shape), _const_spec(wd.shape), _const_spec((1, d))] + cast_in,
        out_specs=[row] + cast_out,
        out_shape=[jax.ShapeDtypeStruct((t, d), F32)] + cast_shapes,
        compiler_params=_params(1),
        name="mix_ffn_final" if final else "mix_ffn",
    )(x2, z2, wo, g, wg, wu, wd, gf, *[w for w, _ in cast_jobs])
    return outs[0], outs[1:]


def _t5_bucket(dist):
    exact = NUM_BUCKETS // 2
    df = jnp.maximum(dist, 1).astype(F32)
    large = exact + (jnp.log(df / exact) / math.log(MAX_DISTANCE / exact)
                     * (NUM_BUCKETS - exact)).astype(jnp.int32)
    large = jnp.minimum(large, NUM_BUCKETS - 1)
    return jnp.where(dist < exact, dist, large)


def _bias_rows(rel_bias, n_dist, stride, branches, period):
    dist = jnp.arange(n_dist, dtype=jnp.int32) * stride
    mult = jnp.zeros((n_dist,), F32)
    for window, dilation in branches:
        mult = mult + ((dist % dilation == 0) & (dist <= window)).astype(F32)
    table = rel_bias[_t5_bucket(dist)].astype(F32)
    total = jnp.where(mult[:, None] > 0,
                      (table + jnp.log(jnp.maximum(mult, 1.0))[:, None]) * LOG2E, NEG_INF)
    pad = jnp.full((period - n_dist, total.shape[1]), NEG_INF, F32)
    return jnp.concatenate([total, pad], axis=0).T[:, None, :]


def _attn_body(tv1_ref, tv2_ref, q_ref, k_ref, v_ref, qs_ref, ks_ref, vs_ref, o_ref,
               bias1_ref, bias10_ref, bias2_ref, lhs_ref, lhss_ref, o2_ref, lse2_ref):
    tq = ATT_TILE
    seq, width = k_ref.shape
    sub = ks_ref.shape[1]
    n_sub = sub // tq
    groups = width // LANES
    heads = width // HEAD_DIM

    @pl.when(pl.program_id(1) == 0)
    def _():
        for h in range(heads):
            row = jnp.broadcast_to(tv1_ref[h], (2 * NEAR, tv1_ref.shape[2]))
            toep = pltpu.roll(row, 0, 1, stride=1, stride_axis=0)
            bias10_ref[:, h * NEAR:(h + 1) * NEAR] = toep[:NEAR, :NEAR]
            bias1_ref[:, h * NEAR:(h + 1) * NEAR] = toep[:, NEAR:2 * NEAR]
            g, hg = divmod(h, PAIR)
            cols = slice(hg * tq, (hg + 1) * tq)
            row = jnp.broadcast_to(tv2_ref[h], (tq, tv2_ref.shape[2]))
            toep = pltpu.roll(row, 0, 1, stride=1, stride_axis=0)
            for dist in range(n_sub):
                start = (n_sub - 1 - dist) * tq
                bias2_ref[g, start:start + tq, cols] = toep[:, dist * tq:(dist + 1) * tq]

    for g in range(groups):
        lanes = slice(g * LANES, (g + 1) * LANES)
        for kb in range(seq // tq):
            lhs_ref[g, :LANES, kb * tq:(kb + 1) * tq] = v_ref[kb * tq:(kb + 1) * tq, lanes].T
        lhs_ref[g, LANES:, :] = jnp.ones((ONES_ROWS, seq), BF16)
        for r in range(STRIDE):
            for kb in range(n_sub):
                lhss_ref[g, r, :LANES, kb * tq:(kb + 1) * tq] = (
                    vs_ref[r, kb * tq:(kb + 1) * tq, lanes].T)
            lhss_ref[g, r, LANES:, :] = jnp.ones((ONES_ROWS, sub), BF16)

    lane_head = lax.broadcasted_iota(jnp.int32, (1, width), 1) // HEAD_DIM
    items = [(g, r, j, 0, (j + 1) * tq)
             for g in range(groups) for r in range(STRIDE) for j in range(n_sub)]
    items += [(None, None, qb, max(0, (qb - 1) * NEAR), (qb + 1) * NEAR)
              for qb in range(seq // NEAR)]
    col_max, scores, probs = {}, {}, {}
    half = tq // 2
    late_cols = (half, tq + half)

    def per_head_rows(q):
        zero = jnp.zeros_like(q)
        lanes = lane_head[:, :q.shape[1]]
        return jnp.concatenate(
            [jnp.where(lanes == h, q, zero) for h in range(q.shape[1] // HEAD_DIM)], axis=0)

    def stage_a(t):
        g, r, qb, lo, hi = items[t]
        if g is None:
            q = per_head_rows(q_ref[qb * NEAR:(qb + 1) * NEAR, :])
            k = k_ref[lo:hi, :]
            bias = bias10_ref[...] if lo == qb * NEAR else bias1_ref[...]
        else:
            lanes = slice(g * LANES, (g + 1) * LANES)
            q = per_head_rows(qs_ref[r, qb * tq:(qb + 1) * tq, lanes])
            k = ks_ref[r, lo:hi, lanes]
            bias = bias2_ref[g, sub - hi:, :]
        s = lax.dot_general(k, q, NT_DIMS, preferred_element_type=F32) + bias
        if g is None:
            s = s.astype(BF16)
            scores[t] = (s, None)
            col_max[t] = jnp.max(s, axis=0, keepdims=True)
        else:
            old = s[:hi - lo - half].astype(BF16)
            new = [s[hi - lo - half:, c:c + half].astype(BF16) for c in late_cols]
            scores[t] = (old, new)
            m_old = jnp.max(old, axis=0, keepdims=True)
            m_new = [jnp.max(x, axis=0, keepdims=True) for x in new]
            col_max[t] = jnp.concatenate(
                [m_old[:, :half], jnp.maximum(m_old[:, half:tq], m_new[0]),
                 m_old[:, tq:tq + half], jnp.maximum(m_old[:, tq + half:], m_new[1])], axis=1)

    def stage_b(t):
        old, new = scores.pop(t)
        m = col_max[t]
        p = jnp.exp2(old - m)
        if new is not None:
            zero = jnp.zeros((half, half), BF16)
            p_new = [jnp.exp2(x - m[:, c:c + half]) for x, c in zip(new, late_cols)]
            p = jnp.concatenate(
                [p, jnp.concatenate([zero, p_new[0], zero, p_new[1]], axis=1)], axis=0)
        probs[t] = p

    def stage_c(t):
        g, r, qb, lo, hi = items[t]
        p, m = probs.pop(t), col_max.pop(t).astype(F32)
        pieces = []
        if g is None:
            for i in range(groups):
                acc = jnp.dot(lhs_ref[i, :, lo:hi], p[:, i * tq:(i + 1) * tq],
                              preferred_element_type=F32)
                for h in range(PAIR):
                    cols = slice(h * NEAR, (h + 1) * NEAR)
                    pieces.append((acc[h * HEAD_DIM:(h + 1) * HEAD_DIM, cols],
                                   acc[LANES:LANES + 1, cols],
                                   m[:, i * tq + h * NEAR:i * tq + (h + 1) * NEAR]))
        else:
            for h in range(PAIR):
                lhs = jnp.concatenate([lhss_ref[g, r, h * HEAD_DIM:(h + 1) * HEAD_DIM, lo:hi],
                                       lhss_ref[g, r, LANES:, lo:hi]], axis=0)
                acc = jnp.dot(lhs, p[:, h * tq:(h + 1) * tq], preferred_element_type=F32)
                pieces.append((acc[:HEAD_DIM], acc[HEAD_DIM:HEAD_DIM + 1],
                               m[:, h * tq:(h + 1) * tq]))
        outs = [num / den for num, den, _ in pieces]
        lses = [jnp.broadcast_to(mx + jnp.log2(den), num.shape) for num, den, mx in pieces]
        out_t, lse_t = jnp.concatenate(outs, axis=0), jnp.concatenate(lses, axis=0)
        out, lse = out_t.T, lse_t.T
        if g is not None:
            token_rows = pl.ds(qb * tq * STRIDE + r, tq, stride=STRIDE)
            o2_ref[g, token_rows, :] = out
            lse2_ref[g, token_rows, :] = lse
        else:
            rows = slice(qb * NEAR, (qb + 1) * NEAR)
            out2 = jnp.concatenate([o2_ref[i, rows, :] for i in range(groups)], axis=1)
            lse2 = jnp.concatenate([lse2_ref[i, rows, :] for i in range(groups)], axis=1)
            share = 1.0 / (1.0 + jnp.exp2(lse2 - lse))
            o_ref[rows, :] = (out2 + share * (out - out2)).astype(BF16)

    for t in range(len(items) + 2):
        if t < len(items):
            stage_a(t)
        if 1 <= t <= len(items):
            stage_b(t - 1)
        if t >= 2:
            stage_c(t - 2)


def _attention(qkv, qkv_s, tv1, tv2):
    bsz, s, d3 = qkv.shape
    d = d3 // 3
    width = HEAD_GROUPS * LANES
    heads = width // HEAD_DIM
    n_blocks = d // width
    sub = s // STRIDE
    col = lambda off: pl.BlockSpec((None, s, width), lambda hb, b: (b, 0, off + hb))
    col_s = lambda off: pl.BlockSpec((None, STRIDE, sub, width),
                                     lambda hb, b: (b, 0, 0, off + hb))
    bias_row = lambda tv: pl.BlockSpec((heads, 1, tv.shape[2]), lambda hb, b: (hb, 0, 0))
    wide = 2 * ATT_TILE
    assert heads * NEAR == wide
    return pl.pallas_call(
        _attn_body,
        grid=(n_blocks, bsz),
        in_specs=[bias_row(tv1), bias_row(tv2),
                  col(0), col(n_blocks), col(2 * n_blocks),
                  col_s(0), col_s(n_blocks), col_s(2 * n_blocks)],
        out_specs=col(0),
        out_shape=jax.ShapeDtypeStruct((bsz, s, d), BF16),
        scratch_shapes=[pltpu.VMEM((2 * NEAR, wide), F32),
                        pltpu.VMEM((NEAR, wide), F32),
                        pltpu.VMEM((HEAD_GROUPS, sub, wide), F32),
                        pltpu.VMEM((HEAD_GROUPS, LANES + ONES_ROWS, s), BF16),
                        pltpu.VMEM((HEAD_GROUPS, STRIDE, LANES + ONES_ROWS, sub), BF16),
                        pltpu.VMEM((HEAD_GROUPS, s, LANES), F32),
                        pltpu.VMEM((HEAD_GROUPS, s, LANES), F32)],
        compiler_params=_params(2),
        name="dilated_attention",
    )(tv1, tv2, qkv, qkv, qkv, qkv_s, qkv_s, qkv_s)


def kernel(x, mix_norm, ffn_norm, final_norm, conv_w_in, conv_kernel, conv_w_out,
           attn_w_qkv, attn_w_out, rel_bias, ffn_w_gate, ffn_w_up, ffn_w_down):
    bsz, s, d = x.shape
    depth = mix_norm.shape[0]
    assert d == N_HEADS * HEAD_DIM and s % ROW_TILE == 0 and ROW_TILE % ATT_TILE == 0
    near = tuple((w, dil) for w, dil in BRANCHES if dil == 1)
    far = tuple((w, dil) for w, dil in BRANCHES if dil > 1)
    assert all(w <= NEAR for w, _ in near) and all(dil % STRIDE == 0 for _, dil in far)
    assert s % (STRIDE * ATT_TILE) == 0 and ROW_TILE % STRIDE == 0

    tv1 = _bias_rows(rel_bias, NEAR + 1, 1, near, 3 * ATT_TILE)
    tv2 = _bias_rows(rel_bias, s // STRIDE, STRIDE, far, s // STRIDE + ATT_TILE)
    gf = final_norm.reshape(1, d)

    def layer_params(i):
        mix_in, mix_out = (conv_w_in, conv_w_out) if i % 2 == 0 else (attn_w_qkv, attn_w_out)
        return [(mix_in, i // 2), (mix_out, i // 2),
                (ffn_w_gate, i), (ffn_w_up, i), (ffn_w_down, i)]

    mix_in = conv_w_in[0].astype(BF16)
    rest = None
    x2 = x.reshape(bsz * s, d)
    for i in range(depth):
        g_mix = mix_norm[i].reshape(1, d)
        if i % 2 == 0:
            jobs = layer_params(i)[1:] if i == 0 else []
            z, made = _conv_pre(x2, g_mix, mix_in, conv_kernel, i // 2, s, jobs)
            rest = made if i == 0 else rest
        else:
            qkv, qkv_s = _qkv_proj(x2, g_mix, mix_in, s)
            z = _attention(qkv.reshape(bsz, s, 3 * d), qkv_s, tv1, tv2).reshape(bsz * s, d)
        mix_out, w_g, w_u, w_d = rest
        jobs = layer_params(i + 1) if i + 1 < depth else []
        x2, made = _mix_ffn(x2, z, mix_out, ffn_norm[i].reshape(1, d), w_g, w_u, w_d, gf,
                            final=(i == depth - 1), cast_jobs=jobs)
        if made:
            mix_in, rest = made[0], made[1:]
    return x2.reshape(bsz, s, d)
```

```python
import functools
import math

import jax
import jax.numpy as jnp
from jax import lax
from jax.experimental import pallas as pl
from jax.experimental.pallas import tpu as pltpu

N_HEADS = 16
HEAD_DIM = 64
CONV_WIDTH = 3
BRANCHES = ((128, 1), (512, 4), (2048, 16))
NUM_BUCKETS = 32
MAX_DISTANCE = 2048
EPS = 1e-6
NEG_INF = -1e30
LOG2E = math.log2(math.e)

LANES = 128
ROW_TILE = 1024
COL_CHUNK = 512
FF_CHUNK = 256
PAIR = LANES // HEAD_DIM
ATT_TILE = 256
HEAD_GROUPS = 2
STRIDE = 4
NEAR = ATT_TILE // 2
F32_SUBLANES = 8
BF16_SUBLANES = 16
ONES_ROWS = BF16_SUBLANES
VMEM_LIMIT = 56 * 1024 * 1024

F32 = jnp.float32
BF16 = jnp.bfloat16
NT_DIMS = (((1,), (1,)), ((), ()))


def _rms(x, g):
    ms = jnp.mean(x * x, axis=-1, keepdims=True)
    return x * lax.rsqrt(ms + EPS) * g


def _const_spec(shape):
    nd = len(shape)
    return pl.BlockSpec(shape, lambda *_: (0,) * nd, pipeline_mode=pl.Buffered(1))


def _layer_spec(stacked, layer):
    tail = (0,) * (stacked.ndim - 1)
    return pl.BlockSpec((None,) + stacked.shape[1:], lambda *_: (layer,) + tail,
                        pipeline_mode=pl.Buffered(1))


def _cast_io(jobs, steps):
    in_specs, out_specs, out_shapes = [], [], []
    for w, layer in jobs:
        rows, cols = w.shape[1] // steps, w.shape[2]
        assert w.shape[1] % steps == 0 and rows % BF16_SUBLANES == 0
        in_specs.append(pl.BlockSpec((None, rows, cols), lambda i, layer=layer: (layer, i, 0)))
        out_specs.append(pl.BlockSpec((rows, cols), lambda i: (i, 0)))
        out_shapes.append(jax.ShapeDtypeStruct(w.shape[1:], BF16))
    return in_specs, out_specs, out_shapes


def _split_refs(refs, n_in, n_out, n_cast):
    a, b, c = n_in + n_cast, n_in + n_cast + n_out, n_in + 2 * n_cast + n_out
    return refs[:n_in], refs[n_in:a], refs[a:b], refs[b:c], refs[c:]


def _run_casts(srcs, dsts):
    for src, dst in zip(srcs, dsts):
        dst[...] = src[...].astype(BF16)


def _params(n_axes):
    return pltpu.CompilerParams(
        dimension_semantics=("arbitrary",) * n_axes, vmem_limit_bytes=VMEM_LIMIT)


def _qkv_body(x_ref, g_ref, w_ref, o_ref, os_ref, slab0_ref, slab1_ref):
    tm, d = x_ref.shape
    h = _rms(x_ref[...], g_ref[...]).astype(BF16)
    q_scale = LOG2E * HEAD_DIM ** -0.5
    for j in range(0, w_ref.shape[1], COL_CHUNK):
        slab_ref = (slab0_ref, slab1_ref)[(j // COL_CHUNK) % 2]
        y = jnp.dot(h, w_ref[:, j:j + COL_CHUNK], preferred_element_type=F32)
        if j < d:
            y = y * q_scale
        o_ref[:, j:j + COL_CHUNK] = y.astype(BF16)
        for c in range(COL_CHUNK // LANES):
            slab_ref[c] = y[:, c * LANES:(c + 1) * LANES]
        for r in range(STRIDE):
            for c in range(COL_CHUNK // LANES):
                part = slab_ref[c, pl.ds(r, tm // STRIDE, stride=STRIDE), :]
                os_ref[r, :, j + c * LANES:j + (c + 1) * LANES] = part.astype(BF16)


def _qkv_proj(x2, g, w, seq):
    t, d = x2.shape
    n = w.shape[1]
    tm = ROW_TILE
    per_seq = seq // tm
    return pl.pallas_call(
        _qkv_body,
        grid=(t // tm,),
        in_specs=[pl.BlockSpec((tm, d), lambda i: (i, 0)),
                  _const_spec((1, d)), _const_spec(w.shape)],
        out_specs=[pl.BlockSpec((tm, n), lambda i: (i, 0)),
                   pl.BlockSpec((None, STRIDE, tm // STRIDE, n),
                                lambda i: (i // per_seq, 0, i % per_seq, 0))],
        out_shape=[jax.ShapeDtypeStruct((t, n), BF16),
                   jax.ShapeDtypeStruct((t // seq, STRIDE, seq // STRIDE, n), BF16)],
        scratch_shapes=[pltpu.VMEM((COL_CHUNK // LANES, tm, LANES), F32)] * 2,
        compiler_params=_params(1),
        name="qkv_proj",
    )(x2, g, w)


def _conv_body(*refs, n_cast, per_seq):
    (x_ref, g_ref, w_ref, k_ref), cast_src, (z_ref,), cast_dst, (cu_ref,) = _split_refs(
        refs, 4, 1, n_cast)
    _run_casts(cast_src, cast_dst)
    tm, d = x_ref.shape
    pad = F32_SUBLANES
    first = pl.program_id(0) % per_seq == 0

    @pl.when(first)
    def _():
        cu_ref[0:pad, :] = jnp.zeros((pad, d), F32)

    @pl.when(jnp.logical_not(first))
    def _():
        cu_ref[0:pad, :] = cu_ref[tm:tm + pad, :]

    h = _rms(x_ref[...], g_ref[...]).astype(BF16)
    for j in range(0, d, COL_CHUNK):
        cols = slice(j, j + COL_CHUNK)
        c = jnp.dot(h, w_ref[:, d + j:d + j + COL_CHUNK], preferred_element_type=F32)
        u = jnp.dot(h, w_ref[:, 2 * d + j:2 * d + j + COL_CHUNK], preferred_element_type=F32)
        cu_ref[pad:pad + tm, cols] = c * u
        y = None
        for w in reversed(range(CONV_WIDTH)):
            start = pad - (CONV_WIDTH - 1 - w)
            term = k_ref[w:w + 1, cols] * cu_ref[start:start + tm, cols]
            y = term if y is None else y + term
        b = jnp.dot(h, w_ref[:, cols], preferred_element_type=F32)
        z_ref[:, cols] = (b * y).astype(BF16)


def _conv_pre(x2, g, w_in, taps, layer, seq, cast_jobs):
    t, d = x2.shape
    tm = ROW_TILE
    steps = t // tm
    cast_in, cast_out, cast_shapes = _cast_io(cast_jobs, steps)
    row = pl.BlockSpec((tm, d), lambda i: (i, 0))
    outs = pl.pallas_call(
        functools.partial(_conv_body, n_cast=len(cast_jobs), per_seq=seq // tm),
        grid=(steps,),
        in_specs=[row, _const_spec((1, d)), _const_spec(w_in.shape), _layer_spec(taps, layer)]
        + cast_in,
        out_specs=[row] + cast_out,
        out_shape=[jax.ShapeDtypeStruct((t, d), BF16)] + cast_shapes,
        scratch_shapes=[pltpu.VMEM((tm + F32_SUBLANES, d), F32)],
        compiler_params=_params(1),
        name="conv_pre",
    )(x2, g, w_in, taps, *[w for w, _ in cast_jobs])
    return outs[0], outs[1:]


def _ffn_body(*refs, final, n_cast):
    ins, cast_src, (o_ref,), cast_dst, _ = _split_refs(refs, 8, 1, n_cast)
    x_ref, z_ref, wo_ref, g_ref, wg_ref, wu_ref, wd_ref, gf_ref = ins
    _run_casts(cast_src, cast_dst)
    x1 = x_ref[...] + jnp.dot(z_ref[...], wo_ref[...], preferred_element_type=F32)
    h = _rms(x1, g_ref[...]).astype(BF16)
    o_ref[...] = x1
    for c in range(0, wg_ref.shape[1], FF_CHUNK):
        gate = jnp.dot(h, wg_ref[:, c:c + FF_CHUNK], preferred_element_type=F32)
        up = jnp.dot(h, wu_ref[:, c:c + FF_CHUNK], preferred_element_type=F32)
        a = (gate * jax.nn.sigmoid(gate) * up).astype(BF16)
        o_ref[...] += jnp.dot(a, wd_ref[c:c + FF_CHUNK, :], preferred_element_type=F32)
    if final:
        o_ref[...] = _rms(o_ref[...], gf_ref[...])


def _mix_ffn(x2, z2, wo, g, wg, wu, wd, gf, final, cast_jobs):
    t, d = x2.shape
    steps = t // ROW_TILE
    cast_in, cast_out, cast_shapes = _cast_io(cast_jobs, steps)
    row = pl.BlockSpec((ROW_TILE, d), lambda i: (i, 0))
    outs = pl.pallas_call(
        functools.partial(_ffn_body, final=final, n_cast=len(cast_jobs)),
        grid=(steps,),
        in_specs=[row, row, _const_spec(wo.shape), _const_spec((1, d)), _const_spec(wg.shape),
                  _const_spec(wu.shape), _const_spec(wd.shape), _const_spec((1, d))] + cast_in,
        out_specs=[row] + cast_out,
        out_shape=[jax.ShapeDtypeStruct((t, d), F32)] + cast_shapes,
        compiler_params=_params(1),
        name="mix_ffn_final" if final else "mix_ffn",
    )(x2, z2, wo, g, wg, wu, wd, gf, *[w for w, _ in cast_jobs])
    return outs[0], outs[1:]


def _t5_bucket(dist):
    exact = NUM_BUCKETS // 2
    df = jnp.maximum(dist, 1).astype(F32)
    large = exact + (jnp.log(df / exact) / math.log(MAX_DISTANCE / exact)
                     * (NUM_BUCKETS - exact)).astype(jnp.int32)
    large = jnp.minimum(large, NUM_BUCKETS - 1)
    return jnp.where(dist < exact, dist, large)


def _bias_rows(rel_bias, n_dist, stride, branches, period):
    dist = jnp.arange(n_dist, dtype=jnp.int32) * stride
    mult = jnp.zeros((n_dist,), F32)
    for window, dilation in branches:
        mult = mult + ((dist % dilation == 0) & (dist <= window)).astype(F32)
    table = rel_bias[_t5_bucket(dist)].astype(F32)
    total = jnp.where(mult[:, None] > 0,
                      (table + jnp.log(jnp.maximum(mult, 1.0))[:, None]) * LOG2E, NEG_INF)
    pad = jnp.full((period - n_dist, total.shape[1]), NEG_INF, F32)
    return jnp.concatenate([total, pad], axis=0).T[:, None, :]


def _attn_body(tv1_ref, tv2_ref, q_ref, k_ref, v_ref, qs_ref, ks_ref, vs_ref, o_ref,
               bias1_ref, bias10_ref, bias2_ref, lhs_ref, lhss_ref, o2_ref, lse2_ref):
    tq = ATT_TILE
    seq, width = k_ref.shape
    sub = ks_ref.shape[1]
    n_sub = sub // tq
    groups = width // LANES
    heads = width // HEAD_DIM

    @pl.when(pl.program_id(1) == 0)
    def _():
        for h in range(heads):
            row = jnp.broadcast_to(tv1_ref[h], (2 * NEAR, tv1_ref.shape[2]))
            toep = pltpu.roll(row, 0, 1, stride=1, stride_axis=0)
            bias10_ref[:, h * NEAR:(h + 1) * NEAR] = toep[:NEAR, :NEAR]
            bias1_ref[:, h * NEAR:(h + 1) * NEAR] = toep[:, NEAR:2 * NEAR]
            g, hg = divmod(h, PAIR)
            cols = slice(hg * tq, (hg + 1) * tq)
            row = jnp.broadcast_to(tv2_ref[h], (tq, tv2_ref.shape[2]))
            toep = pltpu.roll(row, 0, 1, stride=1, stride_axis=0)
            for dist in range(n_sub):
                start = (n_sub - 1 - dist) * tq
                bias2_ref[g, start:start + tq, cols] = toep[:, dist * tq:(dist + 1) * tq]

    for kb in range(seq // tq):
        lhs_ref[:width, kb * tq:(kb + 1) * tq] = v_ref[kb * tq:(kb + 1) * tq, :].T
    lhs_ref[width:, :] = jnp.ones((ONES_ROWS, seq), BF16)
    for g in range(groups):
        lanes = slice(g * LANES, (g + 1) * LANES)
        for r in range(STRIDE):
            for kb in range(n_sub):
                lhss_ref[g, r, :LANES, kb * tq:(kb + 1) * tq] = (
                    vs_ref[r, kb * tq:(kb + 1) * tq, lanes].T)
            lhss_ref[g, r, LANES:, :] = jnp.ones((ONES_ROWS, sub), BF16)

    lane_head = lax.broadcasted_iota(jnp.int32, (1, width), 1) // HEAD_DIM
    items = [(g, r, j, 0, (j + 1) * tq)
             for g in range(groups) for r in range(STRIDE) for j in range(n_sub)]
    items += [(None, None, qb, max(0, (qb - 1) * NEAR), (qb + 1) * NEAR)
              for qb in range(seq // NEAR)]
    col_max, scores, probs = {}, {}, {}
    half = tq // 2
    late_cols = (half, tq + half)

    def per_head_rows(q):
        zero = jnp.zeros_like(q)
        lanes = lane_head[:, :q.shape[1]]
        return jnp.concatenate(
            [jnp.where(lanes == h, q, zero) for h in range(q.shape[1] // HEAD_DIM)], axis=0)

    def stage_a(t):
        g, r, qb, lo, hi = items[t]
        if g is None:
            q = per_head_rows(q_ref[qb * NEAR:(qb + 1) * NEAR, :])
            k = k_ref[lo:hi, :]
            bias = bias10_ref[...] if lo == qb * NEAR else bias1_ref[...]
        else:
            lanes = slice(g * LANES, (g + 1) * LANES)
            q = per_head_rows(qs_ref[r, qb * tq:(qb + 1) * tq, lanes])
            k = ks_ref[r, lo:hi, lanes]
            bias = bias2_ref[g, sub - hi:, :]
        s = lax.dot_general(k, q, NT_DIMS, preferred_element_type=F32) + bias
        if g is None:
            s = s.astype(BF16)
            scores[t] = (s, None)
            col_max[t] = jnp.max(s, axis=0, keepdims=True)
        else:
            old = s[:hi - lo - half].astype(BF16)
            new = [s[hi - lo - half:, c:c + half].astype(BF16) for c in late_cols]
            scores[t] = (old, new)
            m_old = jnp.max(old, axis=0, keepdims=True)
            m_new = [jnp.max(x, axis=0, keepdims=True) for x in new]
            col_max[t] = jnp.concatenate(
                [m_old[:, :half], jnp.maximum(m_old[:, half:tq], m_new[0]),
                 m_old[:, tq:tq + half], jnp.maximum(m_old[:, tq + half:], m_new[1])], axis=1)

    def stage_b(t):
        old, new = scores.pop(t)
        m = col_max[t]
        p = jnp.exp2(old - m)
        if new is not None:
            zero = jnp.zeros((half, half), BF16)
            p_new = [jnp.exp2(x - m[:, c:c + half]) for x, c in zip(new, late_cols)]
            p = jnp.concatenate(
                [p, jnp.concatenate([zero, p_new[0], zero, p_new[1]], axis=1)], axis=0)
        probs[t] = p

    def stage_c(t):
        g, r, qb, lo, hi = items[t]
        p, m = probs.pop(t), col_max.pop(t).astype(F32)
        lhs = lhs_ref[:, lo:hi] if g is None else lhss_ref[g, r, :, lo:hi]
        acc = jnp.dot(lhs, p, preferred_element_type=F32)
        n_heads = (lhs.shape[0] - ONES_ROWS) // HEAD_DIM
        nq = acc.shape[1] // n_heads
        den = acc[n_heads * HEAD_DIM:n_heads * HEAD_DIM + 1]
        lse = m + jnp.log2(den)
        outs, lses = [], []
        for h in range(n_heads):
            cols = slice(h * nq, (h + 1) * nq)
            outs.append(acc[h * HEAD_DIM:(h + 1) * HEAD_DIM, cols] / den[:, cols])
            lses.append(jnp.broadcast_to(lse[:, cols], (HEAD_DIM, nq)))
        out_t, lse_t = jnp.concatenate(outs, axis=0), jnp.concatenate(lses, axis=0)
        out, lse = out_t.T, lse_t.T
        if g is not None:
            token_rows = pl.ds(qb * tq * STRIDE + r, tq, stride=STRIDE)
            o2_ref[g, token_rows, :] = out
            lse2_ref[g, token_rows, :] = lse
        else:
            rows = slice(qb * NEAR, (qb + 1) * NEAR)
            out2 = jnp.concatenate([o2_ref[i, rows, :] for i in range(groups)], axis=1)
            lse2 = jnp.concatenate([lse2_ref[i, rows, :] for i in range(groups)], axis=1)
            share = 1.0 / (1.0 + jnp.exp2(lse2 - lse))
            o_ref[rows, :] = (out2 + share * (out - out2)).astype(BF16)

    for t in range(len(items) + 2):
        if t < len(items):
            stage_a(t)
        if 1 <= t <= len(items):
            stage_b(t - 1)
        if t >= 2:
            stage_c(t - 2)


def _attention(qkv, qkv_s, tv1, tv2):
    bsz, s, d3 = qkv.shape
    d = d3 // 3
    width = HEAD_GROUPS * LANES
    heads = width // HEAD_DIM
    n_blocks = d // width
    sub = s // STRIDE
    col = lambda off: pl.BlockSpec((None, s, width), lambda hb, b: (b, 0, off + hb))
    col_s = lambda off: pl.BlockSpec((None, STRIDE, sub, width),
                                     lambda hb, b: (b, 0, 0, off + hb))
    bias_row = lambda tv: pl.BlockSpec((heads, 1, tv.shape[2]), lambda hb, b: (hb, 0, 0))
    wide = 2 * ATT_TILE
    assert heads * NEAR == wide
    return pl.pallas_call(
        _attn_body,
        grid=(n_blocks, bsz),
        in_specs=[bias_row(tv1), bias_row(tv2),
                  col(0), col(n_blocks), col(2 * n_blocks),
                  col_s(0), col_s(n_blocks), col_s(2 * n_blocks)],
        out_specs=col(0),
        out_shape=jax.ShapeDtypeStruct((bsz, s, d), BF16),
        scratch_shapes=[pltpu.VMEM((2 * NEAR, wide), F32),
                        pltpu.VMEM((NEAR, wide), F32),
                        pltpu.VMEM((HEAD_GROUPS, sub, wide), F32),
                        pltpu.VMEM((width + ONES_ROWS, s), BF16),
                        pltpu.VMEM((HEAD_GROUPS, STRIDE, LANES + ONES_ROWS, sub), BF16),
                        pltpu.VMEM((HEAD_GROUPS, s, LANES), F32),
                        pltpu.VMEM((HEAD_GROUPS, s, LANES), F32)],
        compiler_params=_params(2),
        name="dilated_attention",
    )(tv1, tv2, qkv, qkv, qkv, qkv_s, qkv_s, qkv_s)


def kernel(x, mix_norm, ffn_norm, final_norm, conv_w_in, conv_kernel, conv_w_out,
           attn_w_qkv, attn_w_out, rel_bias, ffn_w_gate, ffn_w_up, ffn_w_down):
    bsz, s, d = x.shape
    depth = mix_norm.shape[0]
    assert d == N_HEADS * HEAD_DIM and s % ROW_TILE == 0 and ROW_TILE % ATT_TILE == 0
    near = tuple((w, dil) for w, dil in BRANCHES if dil == 1)
    far = tuple((w, dil) for w, dil in BRANCHES if dil > 1)
    assert all(w <= NEAR for w, _ in near) and all(dil % STRIDE == 0 for _, dil in far)
    assert s % (STRIDE * ATT_TILE) == 0 and ROW_TILE % STRIDE == 0

    tv1 = _bias_rows(rel_bias, NEAR + 1, 1, near, 3 * ATT_TILE)
    tv2 = _bias_rows(rel_bias, s // STRIDE, STRIDE, far, s // STRIDE + ATT_TILE)
    gf = final_norm.reshape(1, d)

    def layer_params(i):
        mix_in, mix_out = (conv_w_in, conv_w_out) if i % 2 == 0 else (attn_w_qkv, attn_w_out)
        return [(mix_in, i // 2), (mix_out, i // 2),
                (ffn_w_gate, i), (ffn_w_up, i), (ffn_w_down, i)]

    mix_in = conv_w_in[0].astype(BF16)
    rest = None
    x2 = x.reshape(bsz * s, d)
    for i in range(depth):
        g_mix = mix_norm[i].reshape(1, d)
        if i % 2 == 0:
            jobs = layer_params(i)[1:] if i == 0 else []
            z, made = _conv_pre(x2, g_mix, mix_in, conv_kernel, i // 2, s, jobs)
            rest = made if i == 0 else rest
        else:
            qkv, qkv_s = _qkv_proj(x2, g_mix, mix_in, s)
            z = _attention(qkv.reshape(bsz, s, 3 * d), qkv_s, tv1, tv2).reshape(bsz * s, d)
        mix_out, w_g, w_u, w_d = rest
        jobs = layer_params(i + 1) if i + 1 < depth else []
        x2, made = _mix_ffn(x2, z, mix_out, ffn_norm[i].reshape(1, d), w_g, w_u, w_d, gf,
                            final=(i == depth - 1), cast_jobs=jobs)
        if made:
            mix_in, rest = made[0], made[1:]
    return x2.reshape(bsz, s, d)
```

```python
import functools
import math

import jax
import jax.numpy as jnp
from jax import lax
from jax.experimental import pallas as pl
from jax.experimental.pallas import tpu as pltpu

N_HEADS = 16
HEAD_DIM = 64
CONV_WIDTH = 3
BRANCHES = ((128, 1), (512, 4), (2048, 16))
NUM_BUCKETS = 32
MAX_DISTANCE = 2048
EPS = 1e-6
NEG_INF = -1e30
LOG2E = math.log2(math.e)

LANES = 128
ROW_TILE = 1024
COL_CHUNK = 512
FF_CHUNK = 512
PAIR = LANES // HEAD_DIM
ATT_TILE = 256
HEAD_GROUPS = 2
STRIDE = 4
NEAR = ATT_TILE // 2
F32_SUBLANES = 8
BF16_SUBLANES = 16
ONES_ROWS = BF16_SUBLANES
VMEM_LIMIT = 56 * 1024 * 1024

F32 = jnp.float32
BF16 = jnp.bfloat16
NT_DIMS = (((1,), (1,)), ((), ()))


def _rms(x, g):
    ms = jnp.mean(x * x, axis=-1, keepdims=True)
    return x * lax.rsqrt(ms + EPS) * g


def _const_spec(shape):
    nd = len(shape)
    return pl.BlockSpec(shape, lambda *_: (0,) * nd, pipeline_mode=pl.Buffered(1))


def _layer_spec(stacked, layer):
    tail = (0,) * (stacked.ndim - 1)
    return pl.BlockSpec((None,) + stacked.shape[1:], lambda *_: (layer,) + tail,
                        pipeline_mode=pl.Buffered(1))


def _cast_io(jobs, steps):
    in_specs, out_specs, out_shapes = [], [], []
    for w, layer in jobs:
        rows, cols = w.shape[1] // steps, w.shape[2]
        assert w.shape[1] % steps == 0 and rows % BF16_SUBLANES == 0
        in_specs.append(pl.BlockSpec((None, rows, cols), lambda i, layer=layer: (layer, i, 0)))
        out_specs.append(pl.BlockSpec((rows, cols), lambda i: (i, 0)))
        out_shapes.append(jax.ShapeDtypeStruct(w.shape[1:], BF16))
    return in_specs, out_specs, out_shapes


def _split_refs(refs, n_in, n_out, n_cast):
    a, b, c = n_in + n_cast, n_in + n_cast + n_out, n_in + 2 * n_cast + n_out
    return refs[:n_in], refs[n_in:a], refs[a:b], refs[b:c], refs[c:]


def _run_casts(srcs, dsts):
    for src, dst in zip(srcs, dsts):
        dst[...] = src[...].astype(BF16)


def _params(n_axes):
    return pltpu.CompilerParams(
        dimension_semantics=("arbitrary",) * n_axes, vmem_limit_bytes=VMEM_LIMIT)


def _qkv_body(x_ref, g_ref, w_ref, o_ref, os_ref, slab0_ref, slab1_ref):
    tm, d = x_ref.shape
    h = _rms(x_ref[...], g_ref[...]).astype(BF16)
    q_scale = LOG2E * HEAD_DIM ** -0.5
    for j in range(0, w_ref.shape[1], COL_CHUNK):
        slab_ref = (slab0_ref, slab1_ref)[(j // COL_CHUNK) % 2]
        y = jnp.dot(h, w_ref[:, j:j + COL_CHUNK], preferred_element_type=F32)
        if j < d:
            y = y * q_scale
        o_ref[:, j:j + COL_CHUNK] = y.astype(BF16)
        for c in range(COL_CHUNK // LANES):
            slab_ref[c] = y[:, c * LANES:(c + 1) * LANES]
        for r in range(STRIDE):
            for c in range(COL_CHUNK // LANES):
                part = slab_ref[c, pl.ds(r, tm // STRIDE, stride=STRIDE), :]
                os_ref[r, :, j + c * LANES:j + (c + 1) * LANES] = part.astype(BF16)


def _qkv_proj(x2, g, w, seq):
    t, d = x2.shape
    n = w.shape[1]
    tm = ROW_TILE
    per_seq = seq // tm
    return pl.pallas_call(
        _qkv_body,
        grid=(t // tm,),
        in_specs=[pl.BlockSpec((tm, d), lambda i: (i, 0)),
                  _const_spec((1, d)), _const_spec(w.shape)],
        out_specs=[pl.BlockSpec((tm, n), lambda i: (i, 0)),
                   pl.BlockSpec((None, STRIDE, tm // STRIDE, n),
                                lambda i: (i // per_seq, 0, i % per_seq, 0))],
        out_shape=[jax.ShapeDtypeStruct((t, n), BF16),
                   jax.ShapeDtypeStruct((t // seq, STRIDE, seq // STRIDE, n), BF16)],
        scratch_shapes=[pltpu.VMEM((COL_CHUNK // LANES, tm, LANES), F32)] * 2,
        compiler_params=_params(1),
        name="qkv_proj",
    )(x2, g, w)


def _conv_body(*refs, n_cast, per_seq):
    (x_ref, g_ref, w_ref, k_ref), cast_src, (z_ref,), cast_dst, (cu_ref,) = _split_refs(
        refs, 4, 1, n_cast)
    _run_casts(cast_src, cast_dst)
    tm, d = x_ref.shape
    pad = F32_SUBLANES
    first = pl.program_id(0) % per_seq == 0

    @pl.when(first)
    def _():
        cu_ref[0:pad, :] = jnp.zeros((pad, d), F32)

    @pl.when(jnp.logical_not(first))
    def _():
        cu_ref[0:pad, :] = cu_ref[tm:tm + pad, :]

    h = _rms(x_ref[...], g_ref[...]).astype(BF16)
    for j in range(0, d, COL_CHUNK):
        cols = slice(j, j + COL_CHUNK)
        c = jnp.dot(h, w_ref[:, d + j:d + j + COL_CHUNK], preferred_element_type=F32)
        u = jnp.dot(h, w_ref[:, 2 * d + j:2 * d + j + COL_CHUNK], preferred_element_type=F32)
        cu_ref[pad:pad + tm, cols] = c * u
        y = None
        for w in reversed(range(CONV_WIDTH)):
            start = pad - (CONV_WIDTH - 1 - w)
            term = k_ref[w:w + 1, cols] * cu_ref[start:start + tm, cols]
            y = term if y is None else y + term
        b = jnp.dot(h, w_ref[:, cols], preferred_element_type=F32)
        z_ref[:, cols] = (b * y).astype(BF16)


def _conv_pre(x2, g, w_in, taps, layer, seq, cast_jobs):
    t, d = x2.shape
    tm = ROW_TILE
    steps = t // tm
    cast_in, cast_out, cast_shapes = _cast_io(cast_jobs, steps)
    row = pl.BlockSpec((tm, d), lambda i: (i, 0))
    outs = pl.pallas_call(
        functools.partial(_conv_body, n_cast=len(cast_jobs), per_seq=seq // tm),
        grid=(steps,),
        in_specs=[row, _const_spec((1, d)), _const_spec(w_in.shape), _layer_spec(taps, layer)]
        + cast_in,
        out_specs=[row] + cast_out,
        out_shape=[jax.ShapeDtypeStruct((t, d), BF16)] + cast_shapes,
        scratch_shapes=[pltpu.VMEM((tm + F32_SUBLANES, d), F32)],
        compiler_params=_params(1),
        name="conv_pre",
    )(x2, g, w_in, taps, *[w for w, _ in cast_jobs])
    return outs[0], outs[1:]


def _ffn_body(*refs, final, n_cast):
    ins, cast_src, (o_ref,), cast_dst, _ = _split_refs(refs, 8, 1, n_cast)
    x_ref, z_ref, wo_ref, g_ref, wg_ref, wu_ref, wd_ref, gf_ref = ins
    _run_casts(cast_src, cast_dst)
    x1 = x_ref[...] + jnp.dot(z_ref[...], wo_ref[...], preferred_element_type=F32)
    h = _rms(x1, g_ref[...]).astype(BF16)
    o_ref[...] = x1
    for c in range(0, wg_ref.shape[1], FF_CHUNK):
        gate = jnp.dot(h, wg_ref[:, c:c + FF_CHUNK], preferred_element_type=F32)
        up = jnp.dot(h, wu_ref[:, c:c + FF_CHUNK], preferred_element_type=F32)
        a = (gate * jax.nn.sigmoid(gate) * up).astype(BF16)
        o_ref[...] += jnp.dot(a, wd_ref[c:c + FF_CHUNK, :], preferred_element_type=F32)
    if final:
        o_ref[...] = _rms(o_ref[...], gf_ref[...])


def _mix_ffn(x2, z2, wo, g, wg, wu, wd, gf, final, cast_jobs):
    t, d = x2.shape
    steps = t // ROW_TILE
    cast_in, cast_out, cast_shapes = _cast_io(cast_jobs, steps)
    row = pl.BlockSpec((ROW_TILE, d), lambda i: (i, 0))
    outs = pl.pallas_call(
        functools.partial(_ffn_body, final=final, n_cast=len(cast_jobs)),
        grid=(steps,),
        in_specs=[row, row, _const_spec(wo.shape), _const_spec((1, d)), _const_spec(wg.shape),
                  _const_spec(wu.shape), _const_spec(wd.shape), _const_spec((1, d))] + cast_in,
        out_specs=[row] + cast_out,
        out_shape=[jax.ShapeDtypeStruct((t, d), F32)] + cast_shapes,
        compiler_params=_params(1),
        name="mix_ffn_final" if final else "mix_ffn",
    )(x2, z2, wo, g, wg, wu, wd, gf, *[w for w, _ in cast_jobs])
    return outs[0], outs[1:]


def _t5_bucket(dist):
    exact = NUM_BUCKETS // 2
    df = jnp.maximum(dist, 1).astype(F32)
    large = exact + (jnp.log(df / exact) / math.log(MAX_DISTANCE / exact)
                     * (NUM_BUCKETS - exact)).astype(jnp.int32)
    large = jnp.minimum(large, NUM_BUCKETS - 1)
    return jnp.where(dist < exact, dist, large)


def _bias_rows(rel_bias, n_dist, stride, branches, period):
    dist = jnp.arange(n_dist, dtype=jnp.int32) * stride
    mult = jnp.zeros((n_dist,), F32)
    for window, dilation in branches:
        mult = mult + ((dist % dilation == 0) & (dist <= window)).astype(F32)
    table = rel_bias[_t5_bucket(dist)].astype(F32)
    total = jnp.where(mult[:, None] > 0,
                      (table + jnp.log(jnp.maximum(mult, 1.0))[:, None]) * LOG2E, NEG_INF)
    pad = jnp.full((period - n_dist, total.shape[1]), NEG_INF, F32)
    return jnp.concatenate([total, pad], axis=0).T[:, None, :]


def _attn_body(tv1_ref, tv2_ref, q_ref, k_ref, v_ref, qs_ref, ks_ref, vs_ref, o_ref,
               bias1_ref, bias10_ref, bias2_ref, lhs_ref, lhss_ref, o2_ref, lse2_ref):
    tq = ATT_TILE
    seq, width = k_ref.shape
    sub = ks_ref.shape[1]
    n_sub = sub // tq
    groups = width // LANES
    heads = width // HEAD_DIM

    @pl.when(pl.program_id(1) == 0)
    def _():
        for h in range(heads):
            row = jnp.broadcast_to(tv1_ref[h], (2 * NEAR, tv1_ref.shape[2]))
            toep = pltpu.roll(row, 0, 1, stride=1, stride_axis=0)
            bias10_ref[:, h * NEAR:(h + 1) * NEAR] = toep[:NEAR, :NEAR]
            bias1_ref[:, h * NEAR:(h + 1) * NEAR] = toep[:, NEAR:2 * NEAR]
            g, hg = divmod(h, PAIR)
            cols = slice(hg * tq, (hg + 1) * tq)
            row = jnp.broadcast_to(tv2_ref[h], (tq, tv2_ref.shape[2]))
            toep = pltpu.roll(row, 0, 1, stride=1, stride_axis=0)
            for dist in range(n_sub):
                start = (n_sub - 1 - dist) * tq
                bias2_ref[g, start:start + tq, cols] = toep[:, dist * tq:(dist + 1) * tq]

    for kb in range(seq // tq):
        lhs_ref[:width, kb * tq:(kb + 1) * tq] = v_ref[kb * tq:(kb + 1) * tq, :].T
    lhs_ref[width:, :] = jnp.ones((ONES_ROWS, seq), BF16)
    for g in range(groups):
        lanes = slice(g * LANES, (g + 1) * LANES)
        for r in range(STRIDE):
            for kb in range(n_sub):
                lhss_ref[g, r, :LANES, kb * tq:(kb + 1) * tq] = (
                    vs_ref[r, kb * tq:(kb + 1) * tq, lanes].T)
            lhss_ref[g, r, LANES:, :] = jnp.ones((ONES_ROWS, sub), BF16)

    lane_head = lax.broadcasted_iota(jnp.int32, (1, width), 1) // HEAD_DIM
    items = [(g, r, j, 0, (j + 1) * tq)
             for g in range(groups) for r in range(STRIDE) for j in range(n_sub)]
    items += [(None, None, qb, max(0, (qb - 1) * NEAR), (qb + 1) * NEAR)
              for qb in range(seq // NEAR)]
    col_max, scores, probs = {}, {}, {}
    half = tq // 2
    late_cols = (half, tq + half)

    def per_head_rows(q):
        zero = jnp.zeros_like(q)
        lanes = lane_head[:, :q.shape[1]]
        return jnp.concatenate(
            [jnp.where(lanes == h, q, zero) for h in range(q.shape[1] // HEAD_DIM)], axis=0)

    def stage_a(t):
        g, r, qb, lo, hi = items[t]
        if g is None:
            q = per_head_rows(q_ref[qb * NEAR:(qb + 1) * NEAR, :])
            k = k_ref[lo:hi, :]
            bias = bias10_ref[...] if lo == qb * NEAR else bias1_ref[...]
        else:
            lanes = slice(g * LANES, (g + 1) * LANES)
            q = per_head_rows(qs_ref[r, qb * tq:(qb + 1) * tq, lanes])
            k = ks_ref[r, lo:hi, lanes]
            bias = bias2_ref[g, sub - hi:, :]
        s = lax.dot_general(k, q, NT_DIMS, preferred_element_type=F32) + bias
        if g is None:
            s = s.astype(BF16)
            scores[t] = (s, None)
            col_max[t] = jnp.max(s, axis=0, keepdims=True)
        else:
            old = s[:hi - lo - half].astype(BF16)
            new = [s[hi - lo - half:, c:c + half].astype(BF16) for c in late_cols]
            scores[t] = (old, new)
            m_old = jnp.max(old, axis=0, keepdims=True)
            m_new = [jnp.max(x, axis=0, keepdims=True) for x in new]
            col_max[t] = jnp.concatenate(
                [m_old[:, :half], jnp.maximum(m_old[:, half:tq], m_new[0]),
                 m_old[:, tq:tq + half], jnp.maximum(m_old[:, tq + half:], m_new[1])], axis=1)

    def stage_b(t):
        old, new = scores.pop(t)
        m = col_max[t]
        p = jnp.exp2(old - m)
        if new is not None:
            zero = jnp.zeros((half, half), BF16)
            p_new = [jnp.exp2(x - m[:, c:c + half]) for x, c in zip(new, late_cols)]
            p = jnp.concatenate(
                [p, jnp.concatenate([zero, p_new[0], zero, p_new[1]], axis=1)], axis=0)
        probs[t] = p

    def stage_c(t):
        g, r, qb, lo, hi = items[t]
        p, m = probs.pop(t), col_max.pop(t).astype(F32)
        lhs = lhs_ref[:, lo:hi] if g is None else lhss_ref[g, r, :, lo:hi]
        acc = jnp.dot(lhs, p, preferred_element_type=F32)
        n_heads = (lhs.shape[0] - ONES_ROWS) // HEAD_DIM
        nq = acc.shape[1] // n_heads
        den = acc[n_heads * HEAD_DIM:n_heads * HEAD_DIM + 1]
        lse = m + jnp.log2(den)
        outs, lses = [], []
        for h in range(n_heads):
            cols = slice(h * nq, (h + 1) * nq)
            outs.append(acc[h * HEAD_DIM:(h + 1) * HEAD_DIM, cols] / den[:, cols])
            lses.append(jnp.broadcast_to(lse[:, cols], (HEAD_DIM, nq)))
        out_t, lse_t = jnp.concatenate(outs, axis=0), jnp.concatenate(lses, axis=0)
        out, lse = out_t.T, lse_t.T
        if g is not None:
            token_rows = pl.ds(qb * tq * STRIDE + r, tq, stride=STRIDE)
            o2_ref[g, token_rows, :] = out
            lse2_ref[g, token_rows, :] = lse
        else:
            rows = slice(qb * NEAR, (qb + 1) * NEAR)
            out2 = jnp.concatenate([o2_ref[i, rows, :] for i in range(groups)], axis=1)
            lse2 = jnp.concatenate([lse2_ref[i, rows, :] for i in range(groups)], axis=1)
            share = 1.0 / (1.0 + jnp.exp2(lse2 - lse))
            o_ref[rows, :] = (out2 + share * (out - out2)).astype(BF16)

    for t in range(len(items) + 2):
        if t < len(items):
            stage_a(t)
        if 1 <= t <= len(items):
            stage_b(t - 1)
        if t >= 2:
            stage_c(t - 2)


def _attention(qkv, qkv_s, tv1, tv2):
    bsz, s, d3 = qkv.shape
    d = d3 // 3
    width = HEAD_GROUPS * LANES
    heads = width // HEAD_DIM
    n_blocks = d // width
    sub = s // STRIDE
    col = lambda off: pl.BlockSpec((None, s, width), lambda hb, b: (b, 0, off + hb))
    col_s = lambda off: pl.BlockSpec((None, STRIDE, sub, width),
                                     lambda hb, b: (b, 0, 0, off + hb))
    bias_row = lambda tv: pl.BlockSpec((heads, 1, tv.shape[2]), lambda hb, b: (hb, 0, 0))
    wide = 2 * ATT_TILE
    assert heads * NEAR == wide
    return pl.pallas_call(
        _attn_body,
        grid=(n_blocks, bsz),
        in_specs=[bias_row(tv1), bias_row(tv2),
                  col(0), col(n_blocks), col(2 * n_blocks),
                  col_s(0), col_s(n_blocks), col_s(2 * n_blocks)],
        out_specs=col(0),
        out_shape=jax.ShapeDtypeStruct((bsz, s, d), BF16),
        scratch_shapes=[pltpu.VMEM((2 * NEAR, wide), F32),
                        pltpu.VMEM((NEAR, wide), F32),
                        pltpu.VMEM((HEAD_GROUPS, sub, wide), F32),
                        pltpu.VMEM((width + ONES_ROWS, s), BF16),
                        pltpu.VMEM((HEAD_GROUPS, STRIDE, LANES + ONES_ROWS, sub), BF16),
                        pltpu.VMEM((HEAD_GROUPS, s, LANES), F32),
                        pltpu.VMEM((HEAD_GROUPS, s, LANES), F32)],
        compiler_params=_params(2),
        name="dilated_attention",
    )(tv1, tv2, qkv, qkv, qkv, qkv_s, qkv_s, qkv_s)


def kernel(x, mix_norm, ffn_norm, final_norm, conv_w_in, conv_kernel, conv_w_out,
           attn_w_qkv, attn_w_out, rel_bias, ffn_w_gate, ffn_w_up, ffn_w_down):
    bsz, s, d = x.shape
    depth = mix_norm.shape[0]
    assert d == N_HEADS * HEAD_DIM and s % ROW_TILE == 0 and ROW_TILE % ATT_TILE == 0
    near = tuple((w, dil) for w, dil in BRANCHES if dil == 1)
    far = tuple((w, dil) for w, dil in BRANCHES if dil > 1)
    assert all(w <= NEAR for w, _ in near) and all(dil % STRIDE == 0 for _, dil in far)
    assert s % (STRIDE * ATT_TILE) == 0 and ROW_TILE % STRIDE == 0

    tv1 = _bias_rows(rel_bias, NEAR + 1, 1, near, 3 * ATT_TILE)
    tv2 = _bias_rows(rel_bias, s // STRIDE, STRIDE, far, s // STRIDE + ATT_TILE)
    gf = final_norm.reshape(1, d)

    def layer_params(i):
        mix_in, mix_out = (conv_w_in, conv_w_out) if i % 2 == 0 else (attn_w_qkv, attn_w_out)
        return [(mix_in, i // 2), (mix_out, i // 2),
                (ffn_w_gate, i), (ffn_w_up, i), (ffn_w_down, i)]

    mix_in = conv_w_in[0].astype(BF16)
    rest = None
    x2 = x.reshape(bsz * s, d)
    for i in range(depth):
        g_mix = mix_norm[i].reshape(1, d)
        if i % 2 == 0:
            jobs = layer_params(i)[1:] if i == 0 else []
            z, made = _conv_pre(x2, g_mix, mix_in, conv_kernel, i // 2, s, jobs)
            rest = made if i == 0 else rest
        else:
            qkv, qkv_s = _qkv_proj(x2, g_mix, mix_in, s)
            z = _attention(qkv.reshape(bsz, s, 3 * d), qkv_s, tv1, tv2).reshape(bsz * s, d)
        mix_out, w_g, w_u, w_d = rest
        jobs = layer_params(i + 1) if i + 1 < depth else []
        x2, made = _mix_ffn(x2, z, mix_out, ffn_norm[i].reshape(1, d), w_g, w_u, w_d, gf,
                            final=(i == depth - 1), cast_jobs=jobs)
        if made:
            mix_in, rest = made[0], made[1:]
    return x2.reshape(bsz, s, d)
```

```python
import functools
import math

import jax
import jax.numpy as jnp
from jax import lax
from jax.experimental import pallas as pl
from jax.experimental.pallas import tpu as pltpu

N_HEADS = 16
HEAD_DIM = 64
CONV_WIDTH = 3
BRANCHES = ((128, 1), (512, 4), (2048, 16))
NUM_BUCKETS = 32
MAX_DISTANCE = 2048
EPS = 1e-6
NEG_INF = -1e30
LOG2E = math.log2(math.e)

LANES = 128
ROW_TILE = 1024
COL_CHUNK = 512
FF_CHUNK = 256
PAIR = LANES // HEAD_DIM
ATT_TILE = 256
HEAD_GROUPS = 2
STRIDE = 4
NEAR = ATT_TILE // 2
F32_SUBLANES = 8
BF16_SUBLANES = 16
ONES_ROWS = BF16_SUBLANES
VMEM_LIMIT = 56 * 1024 * 1024

F32 = jnp.float32
BF16 = jnp.bfloat16
NT_DIMS = (((1,), (1,)), ((), ()))


def _rms(x, g):
    ms = jnp.mean(x * x, axis=-1, keepdims=True)
    return x * lax.rsqrt(ms + EPS) * g


def _const_spec(shape):
    nd = len(shape)
    return pl.BlockSpec(shape, lambda *_: (0,) * nd, pipeline_mode=pl.Buffered(1))


def _layer_spec(stacked, layer):
    tail = (0,) * (stacked.ndim - 1)
    return pl.BlockSpec((None,) + stacked.shape[1:], lambda *_: (layer,) + tail,
                        pipeline_mode=pl.Buffered(1))


def _cast_io(jobs, steps):
    in_specs, out_specs, out_shapes = [], [], []
    for w, layer in jobs:
        rows, cols = w.shape[1] // steps, w.shape[2]
        assert w.shape[1] % steps == 0 and rows % BF16_SUBLANES == 0
        in_specs.append(pl.BlockSpec((None, rows, cols), lambda i, layer=layer: (layer, i, 0)))
        out_specs.append(pl.BlockSpec((rows, cols), lambda i: (i, 0)))
        out_shapes.append(jax.ShapeDtypeStruct(w.shape[1:], BF16))
    return in_specs, out_specs, out_shapes


def _split_refs(refs, n_in, n_out, n_cast):
    a, b, c = n_in + n_cast, n_in + n_cast + n_out, n_in + 2 * n_cast + n_out
    return refs[:n_in], refs[n_in:a], refs[a:b], refs[b:c], refs[c:]


def _run_casts(srcs, dsts):
    for src, dst in zip(srcs, dsts):
        dst[...] = src[...].astype(BF16)


def _params(n_axes):
    return pltpu.CompilerParams(
        dimension_semantics=("arbitrary",) * n_axes, vmem_limit_bytes=VMEM_LIMIT)


def _qkv_body(x_ref, g_ref, w_ref, o_ref, os_ref, slab0_ref, slab1_ref):
    tm, d = x_ref.shape
    h = _rms(x_ref[...], g_ref[...]).astype(BF16)
    q_scale = LOG2E * HEAD_DIM ** -0.5
    for j in range(0, w_ref.shape[1], COL_CHUNK):
        slab_ref = (slab0_ref, slab1_ref)[(j // COL_CHUNK) % 2]
        y = jnp.dot(h, w_ref[:, j:j + COL_CHUNK], preferred_element_type=F32)
        if j < d:
            y = y * q_scale
        o_ref[:, j:j + COL_CHUNK] = y.astype(BF16)
        for c in range(COL_CHUNK // LANES):
            slab_ref[c] = y[:, c * LANES:(c + 1) * LANES]
        for r in range(STRIDE):
            for c in range(COL_CHUNK // LANES):
                part = slab_ref[c, pl.ds(r, tm // STRIDE, stride=STRIDE), :]
                os_ref[r, :, j + c * LANES:j + (c + 1) * LANES] = part.astype(BF16)


def _qkv_proj(x2, g, w, seq):
    t, d = x2.shape
    n = w.shape[1]
    tm = ROW_TILE
    per_seq = seq // tm
    return pl.pallas_call(
        _qkv_body,
        grid=(t // tm,),
        in_specs=[pl.BlockSpec((tm, d), lambda i: (i, 0)),
                  _const_spec((1, d)), _const_spec(w.shape)],
        out_specs=[pl.BlockSpec((tm, n), lambda i: (i, 0)),
                   pl.BlockSpec((None, STRIDE, tm // STRIDE, n),
                                lambda i: (i // per_seq, 0, i % per_seq, 0))],
        out_shape=[jax.ShapeDtypeStruct((t, n), BF16),
                   jax.ShapeDtypeStruct((t // seq, STRIDE, seq // STRIDE, n), BF16)],
        scratch_shapes=[pltpu.VMEM((COL_CHUNK // LANES, tm, LANES), F32)] * 2,
        compiler_params=_params(1),
        name="qkv_proj",
    )(x2, g, w)


def _conv_body(*refs, n_cast, per_seq):
    (x_ref, g_ref, w_ref, k_ref), cast_src, (z_ref,), cast_dst, (cu_ref,) = _split_refs(
        refs, 4, 1, n_cast)
    _run_casts(cast_src, cast_dst)
    tm, d = x_ref.shape
    pad = F32_SUBLANES
    first = pl.program_id(0) % per_seq == 0

    @pl.when(first)
    def _():
        cu_ref[0:pad, :] = jnp.zeros((pad, d), F32)

    @pl.when(jnp.logical_not(first))
    def _():
        cu_ref[0:pad, :] = cu_ref[tm:tm + pad, :]

    h = _rms(x_ref[...], g_ref[...]).astype(BF16)
    for j in range(0, d, COL_CHUNK):
        cols = slice(j, j + COL_CHUNK)
        c = jnp.dot(h, w_ref[:, d + j:d + j + COL_CHUNK], preferred_element_type=F32)
        u = jnp.dot(h, w_ref[:, 2 * d + j:2 * d + j + COL_CHUNK], preferred_element_type=F32)
        cu_ref[pad:pad + tm, cols] = c * u
        y = None
        for w in reversed(range(CONV_WIDTH)):
            start = pad - (CONV_WIDTH - 1 - w)
            term = k_ref[w:w + 1, cols] * cu_ref[start:start + tm, cols]
            y = term if y is None else y + term
        b = jnp.dot(h, w_ref[:, cols], preferred_element_type=F32)
        z_ref[:, cols] = (b * y).astype(BF16)


def _conv_pre(x2, g, w_in, taps, layer, seq, cast_jobs):
    t, d = x2.shape
    tm = ROW_TILE
    steps = t // tm
    cast_in, cast_out, cast_shapes = _cast_io(cast_jobs, steps)
    row = pl.BlockSpec((tm, d), lambda i: (i, 0))
    outs = pl.pallas_call(
        functools.partial(_conv_body, n_cast=len(cast_jobs), per_seq=seq // tm),
        grid=(steps,),
        in_specs=[row, _const_spec((1, d)), _const_spec(w_in.shape), _layer_spec(taps, layer)]
        + cast_in,
        out_specs=[row] + cast_out,
        out_shape=[jax.ShapeDtypeStruct((t, d), BF16)] + cast_shapes,
        scratch_shapes=[pltpu.VMEM((tm + F32_SUBLANES, d), F32)],
        compiler_params=_params(1),
        name="conv_pre",
    )(x2, g, w_in, taps, *[w for w, _ in cast_jobs])
    return outs[0], outs[1:]


def _ffn_body(*refs, final, n_cast):
    ins, cast_src, (o_ref,), cast_dst, _ = _split_refs(refs, 8, 1, n_cast)
    x_ref, z_ref, wo_ref, g_ref, wg_ref, wu_ref, wd_ref, gf_ref = ins
    _run_casts(cast_src, cast_dst)
    x1 = x_ref[...] + jnp.dot(z_ref[...], wo_ref[...], preferred_element_type=F32)
    h = _rms(x1, g_ref[...]).astype(BF16)
    o_ref[...] = x1
    for c in range(0, wg_ref.shape[1], FF_CHUNK):
        gate = jnp.dot(h, wg_ref[:, c:c + FF_CHUNK], preferred_element_type=F32)
        up = jnp.dot(h, wu_ref[:, c:c + FF_CHUNK], preferred_element_type=F32)
        a = (gate * jax.nn.sigmoid(gate) * up).astype(BF16)
        o_ref[...] += jnp.dot(a, wd_ref[c:c + FF_CHUNK, :], preferred_element_type=F32)
    if final:
        o_ref[...] = _rms(o_ref[...], gf_ref[...])


def _mix_ffn(x2, z2, wo, g, wg, wu, wd, gf, final, cast_jobs):
    t, d = x2.shape
    steps = t // ROW_TILE
    cast_in, cast_out, cast_shapes = _cast_io(cast_jobs, steps)
    row = pl.BlockSpec((ROW_TILE, d), lambda i: (i, 0))
    outs = pl.pallas_call(
        functools.partial(_ffn_body, final=final, n_cast=len(cast_jobs)),
        grid=(steps,),
        in_specs=[row, row, _const_spec(wo.shape), _const_spec((1, d)), _const_spec(wg.shape),
                  _const_spec(wu.shape), _const_spec(wd.shape), _const_spec((1, d))] + cast_in,
        out_specs=[row] + cast_out,
        out_shape=[jax.ShapeDtypeStruct((t, d), F32)] + cast_shapes,
        compiler_params=_params(1),
        name="mix_ffn_final" if final else "mix_ffn",
    )(x2, z2, wo, g, wg, wu, wd, gf, *[w for w, _ in cast_jobs])
    return outs[0], outs[1:]


def _t5_bucket(dist):
    exact = NUM_BUCKETS // 2
    df = jnp.maximum(dist, 1).astype(F32)
    large = exact + (jnp.log(df / exact) / math.log(MAX_DISTANCE / exact)
                     * (NUM_BUCKETS - exact)).astype(jnp.int32)
    large = jnp.minimum(large, NUM_BUCKETS - 1)
    return jnp.where(dist < exact, dist, large)


def _bias_rows(rel_bias, n_dist, stride, branches, period):
    dist = jnp.arange(n_dist, dtype=jnp.int32) * stride
    mult = jnp.zeros((n_dist,), F32)
    for window, dilation in branches:
        mult = mult + ((dist % dilation == 0) & (dist <= window)).astype(F32)
    table = rel_bias[_t5_bucket(dist)].astype(F32)
    total = jnp.where(mult[:, None] > 0,
                      (table + jnp.log(jnp.maximum(mult, 1.0))[:, None]) * LOG2E, NEG_INF)
    pad = jnp.full((period - n_dist, total.shape[1]), NEG_INF, F32)
    return jnp.concatenate([total, pad], axis=0).T[:, None, :]


def _attn_body(tv1_ref, tv2_ref, q_ref, k_ref, v_ref, qs_ref, ks_ref, vs_ref, o_ref,
               bias1_ref, bias10_ref, bias2_ref, lhs_ref, lhss_ref, o2_ref, lse2_ref):
    tq = ATT_TILE
    seq, width = k_ref.shape
    sub = ks_ref.shape[1]
    n_sub = sub // tq
    groups = width // LANES
    heads = width // HEAD_DIM

    @pl.when(pl.program_id(1) == 0)
    def _():
        for h in range(heads):
            row = jnp.broadcast_to(tv1_ref[h], (2 * NEAR, tv1_ref.shape[2]))
            toep = pltpu.roll(row, 0, 1, stride=1, stride_axis=0)
            bias10_ref[:, h * NEAR:(h + 1) * NEAR] = toep[:NEAR, :NEAR]
            bias1_ref[:, h * NEAR:(h + 1) * NEAR] = toep[:, NEAR:2 * NEAR]
            g, hg = divmod(h, PAIR)
            cols = slice(hg * tq, (hg + 1) * tq)
            row = jnp.broadcast_to(tv2_ref[h], (tq, tv2_ref.shape[2]))
            toep = pltpu.roll(row, 0, 1, stride=1, stride_axis=0)
            for dist in range(n_sub):
                start = (n_sub - 1 - dist) * tq
                bias2_ref[g, start:start + tq, cols] = toep[:, dist * tq:(dist + 1) * tq]

    for kb in range(seq // tq):
        lhs_ref[:width, kb * tq:(kb + 1) * tq] = v_ref[kb * tq:(kb + 1) * tq, :].T
    lhs_ref[width:, :] = jnp.ones((ONES_ROWS, seq), BF16)
    for g in range(groups):
        lanes = slice(g * LANES, (g + 1) * LANES)
        for r in range(STRIDE):
            for kb in range(n_sub):
                lhss_ref[g, r, :LANES, kb * tq:(kb + 1) * tq] = (
                    vs_ref[r, kb * tq:(kb + 1) * tq, lanes].T)
            lhss_ref[g, r, LANES:, :] = jnp.ones((ONES_ROWS, sub), BF16)

    lane_head = lax.broadcasted_iota(jnp.int32, (1, width), 1) // HEAD_DIM
    per_j = tq * STRIDE // NEAR
    strided = [[(g, r, j, 0, (j + 1) * tq) for g in range(groups) for r in range(STRIDE)]
               for j in range(n_sub)]
    token = [(None, None, qb, max(0, (qb - 1) * NEAR), (qb + 1) * NEAR)
             for qb in range(seq // NEAR)]
    items = list(strided[0])
    for j in range(n_sub):
        ready = token[j * per_j:(j + 1) * per_j]
        later = strided[j + 1] if j + 1 < n_sub else []
        for i in range(max(len(ready), len(later))):
            items += later[i:i + 1] + ready[i:i + 1]
    col_max, scores, probs = {}, {}, {}
    half = tq // 2
    late_cols = (half, tq + half)

    def per_head_rows(q):
        zero = jnp.zeros_like(q)
        lanes = lane_head[:, :q.shape[1]]
        return jnp.concatenate(
            [jnp.where(lanes == h, q, zero) for h in range(q.shape[1] // HEAD_DIM)], axis=0)

    def stage_a(t):
        g, r, qb, lo, hi = items[t]
        if g is None:
            q = per_head_rows(q_ref[qb * NEAR:(qb + 1) * NEAR, :])
            k = k_ref[lo:hi, :]
            bias = bias10_ref[...] if lo == qb * NEAR else bias1_ref[...]
        else:
            lanes = slice(g * LANES, (g + 1) * LANES)
            q = per_head_rows(qs_ref[r, qb * tq:(qb + 1) * tq, lanes])
            k = ks_ref[r, lo:hi, lanes]
            bias = bias2_ref[g, sub - hi:, :]
        s = lax.dot_general(k, q, NT_DIMS, preferred_element_type=F32) + bias
        if g is None:
            s = s.astype(BF16)
            scores[t] = (s, None)
            col_max[t] = jnp.max(s, axis=0, keepdims=True)
        else:
            old = s[:hi - lo - half].astype(BF16)
            new = [s[hi - lo - half:, c:c + half].astype(BF16) for c in late_cols]
            scores[t] = (old, new)
            m_old = jnp.max(old, axis=0, keepdims=True)
            m_new = [jnp.max(x, axis=0, keepdims=True) for x in new]
            col_max[t] = jnp.concatenate(
                [m_old[:, :half], jnp.maximum(m_old[:, half:tq], m_new[0]),
                 m_old[:, tq:tq + half], jnp.maximum(m_old[:, tq + half:], m_new[1])], axis=1)

    def stage_b(t):
        old, new = scores.pop(t)
        m = col_max[t]
        p = jnp.exp2(old - m)
        if new is not None:
            zero = jnp.zeros((half, half), BF16)
            p_new = [jnp.exp2(x - m[:, c:c + half]) for x, c in zip(new, late_cols)]
            p = jnp.concatenate(
                [p, jnp.concatenate([zero, p_new[0], zero, p_new[1]], axis=1)], axis=0)
        probs[t] = p

    def stage_c(t):
        g, r, qb, lo, hi = items[t]
        p, m = probs.pop(t), col_max.pop(t).astype(F32)
        lhs = lhs_ref[:, lo:hi] if g is None else lhss_ref[g, r, :, lo:hi]
        acc = jnp.dot(lhs, p, preferred_element_type=F32)
        n_heads = (lhs.shape[0] - ONES_ROWS) // HEAD_DIM
        nq = acc.shape[1] // n_heads
        den = acc[n_heads * HEAD_DIM:n_heads * HEAD_DIM + 1]
        lse = m + jnp.log2(den)
        outs, lses = [], []
        for h in range(n_heads):
            cols = slice(h * nq, (h + 1) * nq)
            outs.append(acc[h * HEAD_DIM:(h + 1) * HEAD_DIM, cols] / den[:, cols])
            lses.append(jnp.broadcast_to(lse[:, cols], (HEAD_DIM, nq)))
        out_t, lse_t = jnp.concatenate(outs, axis=0), jnp.concatenate(lses, axis=0)
        out, lse = out_t.T, lse_t.T
        if g is not None:
            token_rows = pl.ds(qb * tq * STRIDE + r, tq, stride=STRIDE)
            o2_ref[g, token_rows, :] = out
            lse2_ref[g, token_rows, :] = lse
        else:
            rows = slice(qb * NEAR, (qb + 1) * NEAR)
            out2 = jnp.concatenate([o2_ref[i, rows, :] for i in range(groups)], axis=1)
            lse2 = jnp.concatenate([lse2_ref[i, rows, :] for i in range(groups)], axis=1)
            share = 1.0 / (1.0 + jnp.exp2(lse2 - lse))
            o_ref[rows, :] = (out2 + share * (out - out2)).astype(BF16)

    for t in range(len(items) + 2):
        if t < len(items):
            stage_a(t)
        if 1 <= t <= len(items):
            stage_b(t - 1)
        if t >= 2:
            stage_c(t - 2)


def _attention(qkv, qkv_s, tv1, tv2):
    bsz, s, d3 = qkv.shape
    d = d3 // 3
    width = HEAD_GROUPS * LANES
    heads = width // HEAD_DIM
    n_blocks = d // width
    sub = s // STRIDE
    col = lambda off: pl.BlockSpec((None, s, width), lambda hb, b: (b, 0, off + hb))
    col_s = lambda off: pl.BlockSpec((None, STRIDE, sub, width),
                                     lambda hb, b: (b, 0, 0, off + hb))
    bias_row = lambda tv: pl.BlockSpec((heads, 1, tv.shape[2]), lambda hb, b: (hb, 0, 0))
    wide = 2 * ATT_TILE
    assert heads * NEAR == wide
    return pl.pallas_call(
        _attn_body,
        grid=(n_blocks, bsz),
        in_specs=[bias_row(tv1), bias_row(tv2),
                  col(0), col(n_blocks), col(2 * n_blocks),
                  col_s(0), col_s(n_blocks), col_s(2 * n_blocks)],
        out_specs=col(0),
        out_shape=jax.ShapeDtypeStruct((bsz, s, d), BF16),
        scratch_shapes=[pltpu.VMEM((2 * NEAR, wide), F32),
                        pltpu.VMEM((NEAR, wide), F32),
                        pltpu.VMEM((HEAD_GROUPS, sub, wide), F32),
                        pltpu.VMEM((width + ONES_ROWS, s), BF16),
                        pltpu.VMEM((HEAD_GROUPS, STRIDE, LANES + ONES_ROWS, sub), BF16),
                        pltpu.VMEM((HEAD_GROUPS, s, LANES), F32),
                        pltpu.VMEM((HEAD_GROUPS, s, LANES), F32)],
        compiler_params=_params(2),
        name="dilated_attention",
    )(tv1, tv2, qkv, qkv, qkv, qkv_s, qkv_s, qkv_s)


def kernel(x, mix_norm, ffn_norm, final_norm, conv_w_in, conv_kernel, conv_w_out,
           attn_w_qkv, attn_w_out, rel_bias, ffn_w_gate, ffn_w_up, ffn_w_down):
    bsz, s, d = x.shape
    depth = mix_norm.shape[0]
    assert d == N_HEADS * HEAD_DIM and s % ROW_TILE == 0 and ROW_TILE % ATT_TILE == 0
    near = tuple((w, dil) for w, dil in BRANCHES if dil == 1)
    far = tuple((w, dil) for w, dil in BRANCHES if dil > 1)
    assert all(w <= NEAR for w, _ in near) and all(dil % STRIDE == 0 for _, dil in far)
    assert s % (STRIDE * ATT_TILE) == 0 and ROW_TILE % STRIDE == 0

    tv1 = _bias_rows(rel_bias, NEAR + 1, 1, near, 3 * ATT_TILE)
    tv2 = _bias_rows(rel_bias, s // STRIDE, STRIDE, far, s // STRIDE + ATT_TILE)
    gf = final_norm.reshape(1, d)

    def layer_params(i):
        mix_in, mix_out = (conv_w_in, conv_w_out) if i % 2 == 0 else (attn_w_qkv, attn_w_out)
        return [(mix_in, i // 2), (mix_out, i // 2),
                (ffn_w_gate, i), (ffn_w_up, i), (ffn_w_down, i)]

    mix_in = conv_w_in[0].astype(BF16)
    rest = None
    x2 = x.reshape(bsz * s, d)
    for i in range(depth):
        g_mix = mix_norm[i].reshape(1, d)
        if i % 2 == 0:
            jobs = layer_params(i)[1:] if i == 0 else []
            z, made = _conv_pre(x2, g_mix, mix_in, conv_kernel, i // 2, s, jobs)
            rest = made if i == 0 else rest
        else:
            qkv, qkv_s = _qkv_proj(x2, g_mix, mix_in, s)
            z = _attention(qkv.reshape(bsz, s, 3 * d), qkv_s, tv1, tv2).reshape(bsz * s, d)
        mix_out, w_g, w_u, w_d = rest
        jobs = layer_params(i + 1) if i + 1 < depth else []
        x2, made = _mix_ffn(x2, z, mix_out, ffn_norm[i].reshape(1, d), w_g, w_u, w_d, gf,
                            final=(i == depth - 1), cast_jobs=jobs)
        if made:
            mix_in, rest = made[0], made[1:]
    return x2.reshape(bsz, s, d)
```

```python
import functools
import math

import jax
import jax.numpy as jnp
from jax import lax
from jax.experimental import pallas as pl
from jax.experimental.pallas import tpu as pltpu

N_HEADS = 16
HEAD_DIM = 64
CONV_WIDTH = 3
BRANCHES = ((128, 1), (512, 4), (2048, 16))
NUM_BUCKETS = 32
MAX_DISTANCE = 2048
EPS = 1e-6
NEG_INF = -1e30
LOG2E = math.log2(math.e)

LANES = 128
ROW_TILE = 1024
COL_CHUNK = 512
FF_CHUNK = 256
PAIR = LANES // HEAD_DIM
ATT_TILE = 256
HEAD_GROUPS = 2
STRIDE = 4
NEAR = ATT_TILE // 2
F32_SUBLANES = 8
BF16_SUBLANES = 16
ONES_ROWS = BF16_SUBLANES
VMEM_LIMIT = 56 * 1024 * 1024

F32 = jnp.float32
BF16 = jnp.bfloat16
NT_DIMS = (((1,), (1,)), ((), ()))


def _rms(x, g):
    ms = jnp.mean(x * x, axis=-1, keepdims=True)
    return x * lax.rsqrt(ms + EPS) * g


def _const_spec(shape):
    nd = len(shape)
    return pl.BlockSpec(shape, lambda *_: (0,) * nd, pipeline_mode=pl.Buffered(1))


def _layer_spec(stacked, layer):
    tail = (0,) * (stacked.ndim - 1)
    return pl.BlockSpec((None,) + stacked.shape[1:], lambda *_: (layer,) + tail,
                        pipeline_mode=pl.Buffered(1))


def _cast_io(jobs, steps):
    in_specs, out_specs, out_shapes = [], [], []
    for w, layer in jobs:
        rows, cols = w.shape[1] // steps, w.shape[2]
        assert w.shape[1] % steps == 0 and rows % BF16_SUBLANES == 0
        in_specs.append(pl.BlockSpec((None, rows, cols), lambda i, layer=layer: (layer, i, 0)))
        out_specs.append(pl.BlockSpec((rows, cols), lambda i: (i, 0)))
        out_shapes.append(jax.ShapeDtypeStruct(w.shape[1:], BF16))
    return in_specs, out_specs, out_shapes


def _split_refs(refs, n_in, n_out, n_cast):
    a, b, c = n_in + n_cast, n_in + n_cast + n_out, n_in + 2 * n_cast + n_out
    return refs[:n_in], refs[n_in:a], refs[a:b], refs[b:c], refs[c:]


def _run_casts(srcs, dsts):
    for src, dst in zip(srcs, dsts):
        dst[...] = src[...].astype(BF16)


def _params(n_axes):
    return pltpu.CompilerParams(
        dimension_semantics=("arbitrary",) * n_axes, vmem_limit_bytes=VMEM_LIMIT)


def _qkv_body(x_ref, g_ref, w_ref, o_ref, os_ref, slab0_ref, slab1_ref):
    tm, d = x_ref.shape
    h = _rms(x_ref[...], g_ref[...]).astype(BF16)
    q_scale = LOG2E * HEAD_DIM ** -0.5
    for j in range(0, w_ref.shape[1], COL_CHUNK):
        slab_ref = (slab0_ref, slab1_ref)[(j // COL_CHUNK) % 2]
        y = jnp.dot(h, w_ref[:, j:j + COL_CHUNK], preferred_element_type=F32)
        if j < d:
            y = y * q_scale
        o_ref[:, j:j + COL_CHUNK] = y.astype(BF16)
        for c in range(COL_CHUNK // LANES):
            slab_ref[c] = y[:, c * LANES:(c + 1) * LANES]
        for r in range(STRIDE):
            for c in range(COL_CHUNK // LANES):
                part = slab_ref[c, pl.ds(r, tm // STRIDE, stride=STRIDE), :]
                os_ref[r, :, j + c * LANES:j + (c + 1) * LANES] = part.astype(BF16)


def _qkv_proj(x2, g, w, seq):
    t, d = x2.shape
    n = w.shape[1]
    tm = ROW_TILE
    per_seq = seq // tm
    return pl.pallas_call(
        _qkv_body,
        grid=(t // tm,),
        in_specs=[pl.BlockSpec((tm, d), lambda i: (i, 0)),
                  _const_spec((1, d)), _const_spec(w.shape)],
        out_specs=[pl.BlockSpec((tm, n), lambda i: (i, 0)),
                   pl.BlockSpec((None, STRIDE, tm // STRIDE, n),
                                lambda i: (i // per_seq, 0, i % per_seq, 0))],
        out_shape=[jax.ShapeDtypeStruct((t, n), BF16),
                   jax.ShapeDtypeStruct((t // seq, STRIDE, seq // STRIDE, n), BF16)],
        scratch_shapes=[pltpu.VMEM((COL_CHUNK // LANES, tm, LANES), F32)] * 2,
        compiler_params=_params(1),
        name="qkv_proj",
    )(x2, g, w)


def _conv_body(*refs, n_cast, per_seq):
    (x_ref, g_ref, w_ref, k_ref), cast_src, (z_ref,), cast_dst, (cu_ref,) = _split_refs(
        refs, 4, 1, n_cast)
    _run_casts(cast_src, cast_dst)
    tm, d = x_ref.shape
    pad = F32_SUBLANES
    first = pl.program_id(0) % per_seq == 0

    @pl.when(first)
    def _():
        cu_ref[0:pad, :] = jnp.zeros((pad, d), F32)

    @pl.when(jnp.logical_not(first))
    def _():
        cu_ref[0:pad, :] = cu_ref[tm:tm + pad, :]

    h = _rms(x_ref[...], g_ref[...]).astype(BF16)
    for j in range(0, d, COL_CHUNK):
        cols = slice(j, j + COL_CHUNK)
        c = jnp.dot(h, w_ref[:, d + j:d + j + COL_CHUNK], preferred_element_type=F32)
        u = jnp.dot(h, w_ref[:, 2 * d + j:2 * d + j + COL_CHUNK], preferred_element_type=F32)
        cu_ref[pad:pad + tm, cols] = c * u
        y = None
        for w in reversed(range(CONV_WIDTH)):
            start = pad - (CONV_WIDTH - 1 - w)
            term = k_ref[w:w + 1, cols] * cu_ref[start:start + tm, cols]
            y = term if y is None else y + term
        b = jnp.dot(h, w_ref[:, cols], preferred_element_type=F32)
        z_ref[:, cols] = (b * y).astype(BF16)


def _conv_pre(x2, g, w_in, taps, layer, seq, cast_jobs):
    t, d = x2.shape
    tm = ROW_TILE
    steps = t // tm
    cast_in, cast_out, cast_shapes = _cast_io(cast_jobs, steps)
    row = pl.BlockSpec((tm, d), lambda i: (i, 0))
    outs = pl.pallas_call(
        functools.partial(_conv_body, n_cast=len(cast_jobs), per_seq=seq // tm),
        grid=(steps,),
        in_specs=[row, _const_spec((1, d)), _const_spec(w_in.shape), _layer_spec(taps, layer)]
        + cast_in,
        out_specs=[row] + cast_out,
        out_shape=[jax.ShapeDtypeStruct((t, d), BF16)] + cast_shapes,
        scratch_shapes=[pltpu.VMEM((tm + F32_SUBLANES, d), F32)],
        compiler_params=_params(1),
        name="conv_pre",
    )(x2, g, w_in, taps, *[w for w, _ in cast_jobs])
    return outs[0], outs[1:]


def _ffn_body(*refs, final, n_cast):
    ins, cast_src, (o_ref,), cast_dst, _ = _split_refs(refs, 8, 1, n_cast)
    x_ref, z_ref, wo_ref, g_ref, wg_ref, wu_ref, wd_ref, gf_ref = ins
    _run_casts(cast_src, cast_dst)
    x1 = x_ref[...] + jnp.dot(z_ref[...], wo_ref[...], preferred_element_type=F32)
    h = _rms(x1, g_ref[...]).astype(BF16)
    o_ref[...] = x1
    for c in range(0, wg_ref.shape[1], FF_CHUNK):
        gate = jnp.dot(h, wg_ref[:, c:c + FF_CHUNK], preferred_element_type=F32)
        up = jnp.dot(h, wu_ref[:, c:c + FF_CHUNK], preferred_element_type=F32)
        a = (gate * jax.nn.sigmoid(gate) * up).astype(BF16)
        o_ref[...] += jnp.dot(a, wd_ref[c:c + FF_CHUNK, :], preferred_element_type=F32)
    if final:
        o_ref[...] = _rms(o_ref[...], gf_ref[...])


def _mix_ffn(x2, z2, wo, g, wg, wu, wd, gf, final, cast_jobs):
    t, d = x2.shape
    steps = t // ROW_TILE
    cast_in, cast_out, cast_shapes = _cast_io(cast_jobs, steps)
    row = pl.BlockSpec((ROW_TILE, d), lambda i: (i, 0))
    outs = pl.pallas_call(
        functools.partial(_ffn_body, final=final, n_cast=len(cast_jobs)),
        grid=(steps,),
        in_specs=[row, row, _const_spec(wo.shape), _const_spec((1, d)), _const_spec(wg.shape),
                  _const_spec(wu.shape), _const_spec(wd.shape), _const_spec((1, d))] + cast_in,
        out_specs=[row] + cast_out,
        out_shape=[jax.ShapeDtypeStruct((t, d), F32)] + cast_shapes,
        compiler_params=_params(1),
        name="mix_ffn_final" if final else "mix_ffn",
    )(x2, z2, wo, g, wg, wu, wd, gf, *[w for w, _ in cast_jobs])
    return outs[0], outs[1:]


def _t5_bucket(dist):
    exact = NUM_BUCKETS // 2
    df = jnp.maximum(dist, 1).astype(F32)
    large = exact + (jnp.log(df / exact) / math.log(MAX_DISTANCE / exact)
                     * (NUM_BUCKETS - exact)).astype(jnp.int32)
    large = jnp.minimum(large, NUM_BUCKETS - 1)
    return jnp.where(dist < exact, dist, large)


def _bias_rows(rel_bias, n_dist, stride, branches, period):
    dist = jnp.arange(n_dist, dtype=jnp.int32) * stride
    mult = jnp.zeros((n_dist,), F32)
    for window, dilation in branches:
        mult = mult + ((dist % dilation == 0) & (dist <= window)).astype(F32)
    table = rel_bias[_t5_bucket(dist)].astype(F32)
    total = jnp.where(mult[:, None] > 0,
                      (table + jnp.log(jnp.maximum(mult, 1.0))[:, None]) * LOG2E, NEG_INF)
    pad = jnp.full((period - n_dist, total.shape[1]), NEG_INF, F32)
    return jnp.concatenate([total, pad], axis=0).T[:, None, :]


def _attn_body(tv1_ref, tv2_ref, q_ref, k_ref, v_ref, qs_ref, ks_ref, vs_ref, o_ref,
               bias1_ref, bias10_ref, bias2_ref, lhs_ref, lhss_ref, o2_ref, lse2_ref):
    tq = ATT_TILE
    seq, width = k_ref.shape
    sub = ks_ref.shape[1]
    n_sub = sub // tq
    groups = width // LANES
    heads = width // HEAD_DIM

    @pl.when(pl.program_id(1) == 0)
    def _():
        for h in range(heads):
            row = jnp.broadcast_to(tv1_ref[h], (2 * NEAR, tv1_ref.shape[2]))
            toep = pltpu.roll(row, 0, 1, stride=1, stride_axis=0)
            bias10_ref[:, h * NEAR:(h + 1) * NEAR] = toep[:NEAR, :NEAR]
            bias1_ref[:, h * NEAR:(h + 1) * NEAR] = toep[:, NEAR:2 * NEAR]
            g, hg = divmod(h, PAIR)
            cols = slice(hg * tq, (hg + 1) * tq)
            row = jnp.broadcast_to(tv2_ref[h], (tq, tv2_ref.shape[2]))
            toep = pltpu.roll(row, 0, 1, stride=1, stride_axis=0)
            for dist in range(n_sub):
                start = (n_sub - 1 - dist) * tq
                bias2_ref[g, start:start + tq, cols] = toep[:, dist * tq:(dist + 1) * tq]

    for kb in range(seq // tq):
        lhs_ref[:width, kb * tq:(kb + 1) * tq] = v_ref[kb * tq:(kb + 1) * tq, :].T
    lhs_ref[width:, :] = jnp.ones((ONES_ROWS, seq), BF16)
    for g in range(groups):
        lanes = slice(g * LANES, (g + 1) * LANES)
        for r in range(STRIDE):
            for kb in range(n_sub):
                lhss_ref[g, r, :LANES, kb * tq:(kb + 1) * tq] = (
                    vs_ref[r, kb * tq:(kb + 1) * tq, lanes].T)
            lhss_ref[g, r, LANES:, :] = jnp.ones((ONES_ROWS, sub), BF16)

    lane_head = lax.broadcasted_iota(jnp.int32, (1, width), 1) // HEAD_DIM
    items = [(g, r, j, 0, (j + 1) * tq)
             for g in range(groups) for r in range(STRIDE) for j in range(n_sub)]
    items += [(None, None, qb, max(0, (qb - 1) * NEAR), (qb + 1) * NEAR)
              for qb in range(seq // NEAR)]
    col_max, scores, probs = {}, {}, {}
    half = tq // 2
    late_cols = (half, tq + half)

    def per_head_rows(q):
        zero = jnp.zeros_like(q)
        lanes = lane_head[:, :q.shape[1]]
        return jnp.concatenate(
            [jnp.where(lanes == h, q, zero) for h in range(q.shape[1] // HEAD_DIM)], axis=0)

    def stage_a(t):
        g, r, qb, lo, hi = items[t]
        if g is None:
            q = per_head_rows(q_ref[qb * NEAR:(qb + 1) * NEAR, :])
            k = k_ref[lo:hi, :]
            bias = bias10_ref[...] if lo == qb * NEAR else bias1_ref[...]
        else:
            lanes = slice(g * LANES, (g + 1) * LANES)
            q = per_head_rows(qs_ref[r, qb * tq:(qb + 1) * tq, lanes])
            k = ks_ref[r, lo:hi, lanes]
            bias = bias2_ref[g, sub - hi:, :]
        s = lax.dot_general(k, q, NT_DIMS, preferred_element_type=F32) + bias
        if g is None:
            s = s.astype(BF16)
            scores[t] = (s, None)
            col_max[t] = jnp.max(s, axis=0, keepdims=True)
        else:
            old = s[:hi - lo - half].astype(BF16)
            new = [s[hi - lo - half:, c:c + half].astype(BF16) for c in late_cols]
            scores[t] = (old, new)
            m_old = jnp.max(old, axis=0, keepdims=True)
            m_new = [jnp.max(x, axis=0, keepdims=True) for x in new]
            col_max[t] = jnp.concatenate(
                [m_old[:, :half], jnp.maximum(m_old[:, half:tq], m_new[0]),
                 m_old[:, tq:tq + half], jnp.maximum(m_old[:, tq + half:], m_new[1])], axis=1)

    def stage_b(t):
        old, new = scores.pop(t)
        m = col_max[t]
        p = jnp.exp2(old - m)
        if new is not None:
            zero = jnp.zeros((half, half), BF16)
            p_new = [jnp.exp2(x - m[:, c:c + half]) for x, c in zip(new, late_cols)]
            p = jnp.concatenate(
                [p, jnp.concatenate([zero, p_new[0], zero, p_new[1]], axis=1)], axis=0)
        probs[t] = p

    def stage_c(t):
        g, r, qb, lo, hi = items[t]
        p, m = probs.pop(t), col_max.pop(t).astype(F32)
        lhs = lhs_ref[:, lo:hi] if g is None else lhss_ref[g, r, :, lo:hi]
        acc = jnp.dot(lhs, p, preferred_element_type=F32)
        n_heads = (lhs.shape[0] - ONES_ROWS) // HEAD_DIM
        nq = acc.shape[1] // n_heads
        den = acc[n_heads * HEAD_DIM:n_heads * HEAD_DIM + 1]
        lse = m + jnp.log2(den)
        outs, lses = [], []
        for h in range(n_heads):
            cols = slice(h * nq, (h + 1) * nq)
            outs.append(acc[h * HEAD_DIM:(h + 1) * HEAD_DIM, cols] / den[:, cols])
            lses.append(jnp.broadcast_to(lse[:, cols], (HEAD_DIM, nq)))
        out_t, lse_t = jnp.concatenate(outs, axis=0), jnp.concatenate(lses, axis=0)
        out, lse = out_t.T, lse_t.T
        if g is not None:
            token_rows = pl.ds(qb * tq * STRIDE + r, tq, stride=STRIDE)
            o2_ref[g, token_rows, :] = out
            lse2_ref[g, token_rows, :] = lse
        else:
            rows = slice(qb * NEAR, (qb + 1) * NEAR)
            out2 = jnp.concatenate([o2_ref[i, rows, :] for i in range(groups)], axis=1)
            lse2 = jnp.concatenate([lse2_ref[i, rows, :] for i in range(groups)], axis=1)
            share = 1.0 / (1.0 + jnp.exp2(lse2 - lse))
            o_ref[rows, :] = (out2 + share * (out - out2)).astype(BF16)

    for t in range(len(items) + 3):
        if t < len(items):
            stage_a(t)
        if 1 <= t <= len(items):
            stage_b(t - 1)
        if t >= 3:
            stage_c(t - 3)


def _attention(qkv, qkv_s, tv1, tv2):
    bsz, s, d3 = qkv.shape
    d = d3 // 3
    width = HEAD_GROUPS * LANES
    heads = width // HEAD_DIM
    n_blocks = d // width
    sub = s // STRIDE
    col = lambda off: pl.BlockSpec((None, s, width), lambda hb, b: (b, 0, off + hb))
    col_s = lambda off: pl.BlockSpec((None, STRIDE, sub, width),
                                     lambda hb, b: (b, 0, 0, off + hb))
    bias_row = lambda tv: pl.BlockSpec((heads, 1, tv.shape[2]), lambda hb, b: (hb, 0, 0))
    wide = 2 * ATT_TILE
    assert heads * NEAR == wide
    return pl.pallas_call(
        _attn_body,
        grid=(n_blocks, bsz),
        in_specs=[bias_row(tv1), bias_row(tv2),
                  col(0), col(n_blocks), col(2 * n_blocks),
                  col_s(0), col_s(n_blocks), col_s(2 * n_blocks)],
        out_specs=col(0),
        out_shape=jax.ShapeDtypeStruct((bsz, s, d), BF16),
        scratch_shapes=[pltpu.VMEM((2 * NEAR, wide), F32),
                        pltpu.VMEM((NEAR, wide), F32),
                        pltpu.VMEM((HEAD_GROUPS, sub, wide), F32),
                        pltpu.VMEM((width + ONES_ROWS, s), BF16),
                        pltpu.VMEM((HEAD_GROUPS, STRIDE, LANES + ONES_ROWS, sub), BF16),
                        pltpu.VMEM((HEAD_GROUPS, s, LANES), F32),
                        pltpu.VMEM((HEAD_GROUPS, s, LANES), F32)],
        compiler_params=_params(2),
        name="dilated_attention",
    )(tv1, tv2, qkv, qkv, qkv, qkv_s, qkv_s, qkv_s)


def kernel(x, mix_norm, ffn_norm, final_norm, conv_w_in, conv_kernel, conv_w_out,
           attn_w_qkv, attn_w_out, rel_bias, ffn_w_gate, ffn_w_up, ffn_w_down):
    bsz, s, d = x.shape
    depth = mix_norm.shape[0]
    assert d == N_HEADS * HEAD_DIM and s % ROW_TILE == 0 and ROW_TILE % ATT_TILE == 0
    near = tuple((w, dil) for w, dil in BRANCHES if dil == 1)
    far = tuple((w, dil) for w, dil in BRANCHES if dil > 1)
    assert all(w <= NEAR for w, _ in near) and all(dil % STRIDE == 0 for _, dil in far)
    assert s % (STRIDE * ATT_TILE) == 0 and ROW_TILE % STRIDE == 0

    tv1 = _bias_rows(rel_bias, NEAR + 1, 1, near, 3 * ATT_TILE)
    tv2 = _bias_rows(rel_bias, s // STRIDE, STRIDE, far, s // STRIDE + ATT_TILE)
    gf = final_norm.reshape(1, d)

    def layer_params(i):
        mix_in, mix_out = (conv_w_in, conv_w_out) if i % 2 == 0 else (attn_w_qkv, attn_w_out)
        return [(mix_in, i // 2), (mix_out, i // 2),
                (ffn_w_gate, i), (ffn_w_up, i), (ffn_w_down, i)]

    mix_in = conv_w_in[0].astype(BF16)
    rest = None
    x2 = x.reshape(bsz * s, d)
    for i in range(depth):
        g_mix = mix_norm[i].reshape(1, d)
        if i % 2 == 0:
            jobs = layer_params(i)[1:] if i == 0 else []
            z, made = _conv_pre(x2, g_mix, mix_in, conv_kernel, i // 2, s, jobs)
            rest = made if i == 0 else rest
        else:
            qkv, qkv_s = _qkv_proj(x2, g_mix, mix_in, s)
            z = _attention(qkv.reshape(bsz, s, 3 * d), qkv_s, tv1, tv2).reshape(bsz * s, d)
        mix_out, w_g, w_u, w_d = rest
        jobs = layer_params(i + 1) if i + 1 < depth else []
        x2, made = _mix_ffn(x2, z, mix_out, ffn_norm[i].reshape(1, d), w_g, w_u, w_d, gf,
                            final=(i == depth - 1), cast_jobs=jobs)
        if made:
            mix_in, rest = made[0], made[1:]
    return x2.reshape(bsz, s, d)
```

```python
import functools
import math

import jax
import jax.numpy as jnp
from jax import lax
from jax.experimental import pallas as pl
from jax.experimental.pallas import tpu as pltpu

N_HEADS = 16
HEAD_DIM = 64
CONV_WIDTH = 3
BRANCHES = ((128, 1), (512, 4), (2048, 16))
NUM_BUCKETS = 32
MAX_DISTANCE = 2048
EPS = 1e-6
NEG_INF = -1e30
LOG2E = math.log2(math.e)

LANES = 128
ROW_TILE = 1024
COL_CHUNK = 512
FF_CHUNK = 256
PAIR = LANES // HEAD_DIM
ATT_TILE = 256
HEAD_GROUPS = 2
STRIDE = 4
NEAR = ATT_TILE // 2
F32_SUBLANES = 8
BF16_SUBLANES = 16
ONES_ROWS = BF16_SUBLANES
VMEM_LIMIT = 56 * 1024 * 1024

F32 = jnp.float32
BF16 = jnp.bfloat16
NT_DIMS = (((1,), (1,)), ((), ()))


def _rms(x, g):
    ms = jnp.mean(x * x, axis=-1, keepdims=True)
    return x * lax.rsqrt(ms + EPS) * g


def _const_spec(shape):
    nd = len(shape)
    return pl.BlockSpec(shape, lambda *_: (0,) * nd, pipeline_mode=pl.Buffered(1))


def _layer_spec(stacked, layer):
    tail = (0,) * (stacked.ndim - 1)
    return pl.BlockSpec((None,) + stacked.shape[1:], lambda *_: (layer,) + tail,
                        pipeline_mode=pl.Buffered(1))


def _cast_io(jobs, steps):
    in_specs, out_specs, out_shapes = [], [], []
    for w, layer in jobs:
        rows, cols = w.shape[1] // steps, w.shape[2]
        assert w.shape[1] % steps == 0 and rows % BF16_SUBLANES == 0
        in_specs.append(pl.BlockSpec((None, rows, cols), lambda i, layer=layer: (layer, i, 0)))
        out_specs.append(pl.BlockSpec((rows, cols), lambda i: (i, 0)))
        out_shapes.append(jax.ShapeDtypeStruct(w.shape[1:], BF16))
    return in_specs, out_specs, out_shapes


def _split_refs(refs, n_in, n_out, n_cast):
    a, b, c = n_in + n_cast, n_in + n_cast + n_out, n_in + 2 * n_cast + n_out
    return refs[:n_in], refs[n_in:a], refs[a:b], refs[b:c], refs[c:]


def _run_casts(srcs, dsts):
    for src, dst in zip(srcs, dsts):
        dst[...] = src[...].astype(BF16)


def _params(n_axes):
    return pltpu.CompilerParams(
        dimension_semantics=("arbitrary",) * n_axes, vmem_limit_bytes=VMEM_LIMIT)


def _qkv_body(x_ref, g_ref, w_ref, o_ref, os_ref, slab0_ref, slab1_ref):
    tm, d = x_ref.shape
    h = _rms(x_ref[...], g_ref[...]).astype(BF16)
    q_scale = LOG2E * HEAD_DIM ** -0.5
    for j in range(0, w_ref.shape[1], COL_CHUNK):
        slab_ref = (slab0_ref, slab1_ref)[(j // COL_CHUNK) % 2]
        y = jnp.dot(h, w_ref[:, j:j + COL_CHUNK], preferred_element_type=F32)
        if j < d:
            y = y * q_scale
        o_ref[:, j:j + COL_CHUNK] = y.astype(BF16)
        for c in range(COL_CHUNK // LANES):
            slab_ref[c] = y[:, c * LANES:(c + 1) * LANES]
        for r in range(STRIDE):
            for c in range(COL_CHUNK // LANES):
                part = slab_ref[c, pl.ds(r, tm // STRIDE, stride=STRIDE), :]
                os_ref[r, :, j + c * LANES:j + (c + 1) * LANES] = part.astype(BF16)


def _qkv_proj(x2, g, w, seq):
    t, d = x2.shape
    n = w.shape[1]
    tm = ROW_TILE
    per_seq = seq // tm
    return pl.pallas_call(
        _qkv_body,
        grid=(t // tm,),
        in_specs=[pl.BlockSpec((tm, d), lambda i: (i, 0)),
                  _const_spec((1, d)), _const_spec(w.shape)],
        out_specs=[pl.BlockSpec((tm, n), lambda i: (i, 0)),
                   pl.BlockSpec((None, STRIDE, tm // STRIDE, n),
                                lambda i: (i // per_seq, 0, i % per_seq, 0))],
        out_shape=[jax.ShapeDtypeStruct((t, n), BF16),
                   jax.ShapeDtypeStruct((t // seq, STRIDE, seq // STRIDE, n), BF16)],
        scratch_shapes=[pltpu.VMEM((COL_CHUNK // LANES, tm, LANES), F32)] * 2,
        compiler_params=_params(1),
        name="qkv_proj",
    )(x2, g, w)


def _conv_body(*refs, n_cast, per_seq):
    (x_ref, g_ref, w_ref, k_ref), cast_src, (z_ref,), cast_dst, (cu_ref,) = _split_refs(
        refs, 4, 1, n_cast)
    _run_casts(cast_src, cast_dst)
    tm, d = x_ref.shape
    pad = F32_SUBLANES
    first = pl.program_id(0) % per_seq == 0

    @pl.when(first)
    def _():
        cu_ref[0:pad, :] = jnp.zeros((pad, d), F32)

    @pl.when(jnp.logical_not(first))
    def _():
        cu_ref[0:pad, :] = cu_ref[tm:tm + pad, :]

    h = _rms(x_ref[...], g_ref[...]).astype(BF16)
    for j in range(0, d, COL_CHUNK):
        cols = slice(j, j + COL_CHUNK)
        c = jnp.dot(h, w_ref[:, d + j:d + j + COL_CHUNK], preferred_element_type=F32)
        u = jnp.dot(h, w_ref[:, 2 * d + j:2 * d + j + COL_CHUNK], preferred_element_type=F32)
        cu_ref[pad:pad + tm, cols] = c * u
        y = None
        for w in reversed(range(CONV_WIDTH)):
            start = pad - (CONV_WIDTH - 1 - w)
            term = k_ref[w:w + 1, cols] * cu_ref[start:start + tm, cols]
            y = term if y is None else y + term
        b = jnp.dot(h, w_ref[:, cols], preferred_element_type=F32)
        z_ref[:, cols] = (b * y).astype(BF16)


def _conv_pre(x2, g, w_in, taps, layer, seq, cast_jobs):
    t, d = x2.shape
    tm = ROW_TILE
    steps = t // tm
    cast_in, cast_out, cast_shapes = _cast_io(cast_jobs, steps)
    row = pl.BlockSpec((tm, d), lambda i: (i, 0))
    outs = pl.pallas_call(
        functools.partial(_conv_body, n_cast=len(cast_jobs), per_seq=seq // tm),
        grid=(steps,),
        in_specs=[row, _const_spec((1, d)), _const_spec(w_in.shape), _layer_spec(taps, layer)]
        + cast_in,
        out_specs=[row] + cast_out,
        out_shape=[jax.ShapeDtypeStruct((t, d), BF16)] + cast_shapes,
        scratch_shapes=[pltpu.VMEM((tm + F32_SUBLANES, d), F32)],
        compiler_params=_params(1),
        name="conv_pre",
    )(x2, g, w_in, taps, *[w for w, _ in cast_jobs])
    return outs[0], outs[1:]


def _ffn_body(*refs, final, n_cast):
    ins, cast_src, (o_ref,), cast_dst, scratch = _split_refs(refs, 8, 1, n_cast)
    x_ref, z_ref, wo_ref, g_ref, wg_hbm, wu_hbm, wd_hbm, gf_ref = ins
    wg_ref, wu_ref, wd_ref, sem = scratch
    chunks = range(0, wg_ref.shape[1], FF_CHUNK)
    first = pl.program_id(0) == 0

    def chunk_copies(c):
        cols, k = pl.ds(c, FF_CHUNK), c // FF_CHUNK
        return (pltpu.make_async_copy(wg_hbm.at[:, cols], wg_ref.at[:, cols], sem.at[0, k]),
                pltpu.make_async_copy(wu_hbm.at[:, cols], wu_ref.at[:, cols], sem.at[1, k]),
                pltpu.make_async_copy(wd_hbm.at[cols, :], wd_ref.at[cols, :], sem.at[2, k]))

    @pl.when(first)
    def _():
        for c in chunks:
            for copy in chunk_copies(c):
                copy.start()

    _run_casts(cast_src, cast_dst)
    x1 = x_ref[...] + jnp.dot(z_ref[...], wo_ref[...], preferred_element_type=F32)
    h = _rms(x1, g_ref[...]).astype(BF16)
    o_ref[...] = x1

    def ffn(wait):
        for c in chunks:
            if wait:
                for copy in chunk_copies(c):
                    copy.wait()
            gate = jnp.dot(h, wg_ref[:, c:c + FF_CHUNK], preferred_element_type=F32)
            up = jnp.dot(h, wu_ref[:, c:c + FF_CHUNK], preferred_element_type=F32)
            a = (gate * jax.nn.sigmoid(gate) * up).astype(BF16)
            o_ref[...] += jnp.dot(a, wd_ref[c:c + FF_CHUNK, :], preferred_element_type=F32)

    pl.when(first)(functools.partial(ffn, True))
    pl.when(jnp.logical_not(first))(functools.partial(ffn, False))
    if final:
        o_ref[...] = _rms(o_ref[...], gf_ref[...])


def _mix_ffn(x2, z2, wo, g, wg, wu, wd, gf, final, cast_jobs):
    t, d = x2.shape
    steps = t // ROW_TILE
    cast_in, cast_out, cast_shapes = _cast_io(cast_jobs, steps)
    row = pl.BlockSpec((ROW_TILE, d), lambda i: (i, 0))
    hbm = pl.BlockSpec(memory_space=pl.ANY)
    assert wg.shape[1] % FF_CHUNK == 0
    outs = pl.pallas_call(
        functools.partial(_ffn_body, final=final, n_cast=len(cast_jobs)),
        grid=(steps,),
        in_specs=[row, row, _const_spec(wo.shape), _const_spec((1, d)), hbm, hbm, hbm,
                  _const_spec((1, d))] + cast_in,
        out_specs=[row] + cast_out,
        out_shape=[jax.ShapeDtypeStruct((t, d), F32)] + cast_shapes,
        scratch_shapes=[pltpu.VMEM(wg.shape, BF16), pltpu.VMEM(wu.shape, BF16),
                        pltpu.VMEM(wd.shape, BF16),
                        pltpu.SemaphoreType.DMA((3, wg.shape[1] // FF_CHUNK))],
        compiler_params=_params(1),
        name="mix_ffn_final" if final else "mix_ffn",
    )(x2, z2, wo, g, wg, wu, wd, gf, *[w for w, _ in cast_jobs])
    return outs[0], outs[1:]


def _t5_bucket(dist):
    exact = NUM_BUCKETS // 2
    df = jnp.maximum(dist, 1).astype(F32)
    large = exact + (jnp.log(df / exact) / math.log(MAX_DISTANCE / exact)
                     * (NUM_BUCKETS - exact)).astype(jnp.int32)
    large = jnp.minimum(large, NUM_BUCKETS - 1)
    return jnp.where(dist < exact, dist, large)


def _bias_rows(rel_bias, n_dist, stride, branches, period):
    dist = jnp.arange(n_dist, dtype=jnp.int32) * stride
    mult = jnp.zeros((n_dist,), F32)
    for window, dilation in branches:
        mult = mult + ((dist % dilation == 0) & (dist <= window)).astype(F32)
    table = rel_bias[_t5_bucket(dist)].astype(F32)
    total = jnp.where(mult[:, None] > 0,
                      (table + jnp.log(jnp.maximum(mult, 1.0))[:, None]) * LOG2E, NEG_INF)
    pad = jnp.full((period - n_dist, total.shape[1]), NEG_INF, F32)
    return jnp.concatenate([total, pad], axis=0).T[:, None, :]


def _attn_body(tv1_ref, tv2_ref, q_ref, k_ref, v_ref, qs_ref, ks_ref, vs_ref, o_ref,
               bias1_ref, bias10_ref, bias2_ref, lhs_ref, lhss_ref, o2_ref, lse2_ref):
    tq = ATT_TILE
    seq, width = k_ref.shape
    sub = ks_ref.shape[1]
    n_sub = sub // tq
    groups = width // LANES
    heads = width // HEAD_DIM

    @pl.when(pl.program_id(1) == 0)
    def _():
        for h in range(heads):
            row = jnp.broadcast_to(tv1_ref[h], (2 * NEAR, tv1_ref.shape[2]))
            toep = pltpu.roll(row, 0, 1, stride=1, stride_axis=0)
            bias10_ref[:, h * NEAR:(h + 1) * NEAR] = toep[:NEAR, :NEAR]
            bias1_ref[:, h * NEAR:(h + 1) * NEAR] = toep[:, NEAR:2 * NEAR]
            g, hg = divmod(h, PAIR)
            cols = slice(hg * tq, (hg + 1) * tq)
            row = jnp.broadcast_to(tv2_ref[h], (tq, tv2_ref.shape[2]))
            toep = pltpu.roll(row, 0, 1, stride=1, stride_axis=0)
            for dist in range(n_sub):
                start = (n_sub - 1 - dist) * tq
                bias2_ref[g, start:start + tq, cols] = toep[:, dist * tq:(dist + 1) * tq]

    for kb in range(seq // tq):
        lhs_ref[:width, kb * tq:(kb + 1) * tq] = v_ref[kb * tq:(kb + 1) * tq, :].T
    lhs_ref[width:, :] = jnp.ones((ONES_ROWS, seq), BF16)
    for g in range(groups):
        lanes = slice(g * LANES, (g + 1) * LANES)
        for r in range(STRIDE):
            for kb in range(n_sub):
                lhss_ref[g, r, :LANES, kb * tq:(kb + 1) * tq] = (
                    vs_ref[r, kb * tq:(kb + 1) * tq, lanes].T)
            lhss_ref[g, r, LANES:, :] = jnp.ones((ONES_ROWS, sub), BF16)

    lane_head = lax.broadcasted_iota(jnp.int32, (1, width), 1) // HEAD_DIM
    items = [(g, r, j, 0, (j + 1) * tq)
             for g in range(groups) for r in range(STRIDE) for j in range(n_sub)]
    items += [(None, None, qb, max(0, (qb - 1) * NEAR), (qb + 1) * NEAR)
              for qb in range(seq // NEAR)]
    col_max, scores, probs = {}, {}, {}
    half = tq // 2
    late_cols = (half, tq + half)

    def per_head_rows(q):
        zero = jnp.zeros_like(q)
        lanes = lane_head[:, :q.shape[1]]
        return jnp.concatenate(
            [jnp.where(lanes == h, q, zero) for h in range(q.shape[1] // HEAD_DIM)], axis=0)

    def stage_a(t):
        g, r, qb, lo, hi = items[t]
        if g is None:
            q = per_head_rows(q_ref[qb * NEAR:(qb + 1) * NEAR, :])
            k = k_ref[lo:hi, :]
            bias = bias10_ref[...] if lo == qb * NEAR else bias1_ref[...]
        else:
            lanes = slice(g * LANES, (g + 1) * LANES)
            q = per_head_rows(qs_ref[r, qb * tq:(qb + 1) * tq, lanes])
            k = ks_ref[r, lo:hi, lanes]
            bias = bias2_ref[g, sub - hi:, :]
        s = lax.dot_general(k, q, NT_DIMS, preferred_element_type=F32) + bias
        if g is None:
            s = s.astype(BF16)
            scores[t] = (s, None)
            col_max[t] = jnp.max(s, axis=0, keepdims=True)
        else:
            old = s[:hi - lo - half].astype(BF16)
            new = [s[hi - lo - half:, c:c + half].astype(BF16) for c in late_cols]
            scores[t] = (old, new)
            m_old = jnp.max(old, axis=0, keepdims=True)
            m_new = [jnp.max(x, axis=0, keepdims=True) for x in new]
            col_max[t] = jnp.concatenate(
                [m_old[:, :half], jnp.maximum(m_old[:, half:tq], m_new[0]),
                 m_old[:, tq:tq + half], jnp.maximum(m_old[:, tq + half:], m_new[1])], axis=1)

    def stage_b(t):
        old, new = scores.pop(t)
        m = col_max[t]
        p = jnp.exp2(old - m)
        if new is not None:
            zero = jnp.zeros((half, half), BF16)
            p_new = [jnp.exp2(x - m[:, c:c + half]) for x, c in zip(new, late_cols)]
            p = jnp.concatenate(
                [p, jnp.concatenate([zero, p_new[0], zero, p_new[1]], axis=1)], axis=0)
        probs[t] = p

    def stage_c(t):
        g, r, qb, lo, hi = items[t]
        p, m = probs.pop(t), col_max.pop(t).astype(F32)
        lhs = lhs_ref[:, lo:hi] if g is None else lhss_ref[g, r, :, lo:hi]
        acc = jnp.dot(lhs, p, preferred_element_type=F32)
        n_heads = (lhs.shape[0] - ONES_ROWS) // HEAD_DIM
        nq = acc.shape[1] // n_heads
        den = acc[n_heads * HEAD_DIM:n_heads * HEAD_DIM + 1]
        lse = m + jnp.log2(den)
        outs, lses = [], []
        for h in range(n_heads):
            cols = slice(h * nq, (h + 1) * nq)
            outs.append(acc[h * HEAD_DIM:(h + 1) * HEAD_DIM, cols] / den[:, cols])
            lses.append(jnp.broadcast_to(lse[:, cols], (HEAD_DIM, nq)))
        out_t, lse_t = jnp.concatenate(outs, axis=0), jnp.concatenate(lses, axis=0)
        out, lse = out_t.T, lse_t.T
        if g is not None:
            token_rows = pl.ds(qb * tq * STRIDE + r, tq, stride=STRIDE)
            o2_ref[g, token_rows, :] = out
            lse2_ref[g, token_rows, :] = lse
        else:
            rows = slice(qb * NEAR, (qb + 1) * NEAR)
            out2 = jnp.concatenate([o2_ref[i, rows, :] for i in range(groups)], axis=1)
            lse2 = jnp.concatenate([lse2_ref[i, rows, :] for i in range(groups)], axis=1)
            share = 1.0 / (1.0 + jnp.exp2(lse2 - lse))
            o_ref[rows, :] = (out2 + share * (out - out2)).astype(BF16)

    for t in range(len(items) + 2):
        if t < len(items):
            stage_a(t)
        if 1 <= t <= len(items):
            stage_b(t - 1)
        if t >= 2:
            stage_c(t - 2)


def _attention(qkv, qkv_s, tv1, tv2):
    bsz, s, d3 = qkv.shape
    d = d3 // 3
    width = HEAD_GROUPS * LANES
    heads = width // HEAD_DIM
    n_blocks = d // width
    sub = s // STRIDE
    col = lambda off: pl.BlockSpec((None, s, width), lambda hb, b: (b, 0, off + hb))
    col_s = lambda off: pl.BlockSpec((None, STRIDE, sub, width),
                                     lambda hb, b: (b, 0, 0, off + hb))
    bias_row = lambda tv: pl.BlockSpec((heads, 1, tv.shape[2]), lambda hb, b: (hb, 0, 0))
    wide = 2 * ATT_TILE
    assert heads * NEAR == wide
    return pl.pallas_call(
        _attn_body,
        grid=(n_blocks, bsz),
        in_specs=[bias_row(tv1), bias_row(tv2),
                  col(0), col(n_blocks), col(2 * n_blocks),
                  col_s(0), col_s(n_blocks), col_s(2 * n_blocks)],
        out_specs=col(0),
        out_shape=jax.ShapeDtypeStruct((bsz, s, d), BF16),
        scratch_shapes=[pltpu.VMEM((2 * NEAR, wide), F32),
                        pltpu.VMEM((NEAR, wide), F32),
                        pltpu.VMEM((HEAD_GROUPS, sub, wide), F32),
                        pltpu.VMEM((width + ONES_ROWS, s), BF16),
                        pltpu.VMEM((HEAD_GROUPS, STRIDE, LANES + ONES_ROWS, sub), BF16),
                        pltpu.VMEM((HEAD_GROUPS, s, LANES), F32),
                        pltpu.VMEM((HEAD_GROUPS, s, LANES), F32)],
        compiler_params=_params(2),
        name="dilated_attention",
    )(tv1, tv2, qkv, qkv, qkv, qkv_s, qkv_s, qkv_s)


def kernel(x, mix_norm, ffn_norm, final_norm, conv_w_in, conv_kernel, conv_w_out,
           attn_w_qkv, attn_w_out, rel_bias, ffn_w_gate, ffn_w_up, ffn_w_down):
    bsz, s, d = x.shape
    depth = mix_norm.shape[0]
    assert d == N_HEADS * HEAD_DIM and s % ROW_TILE == 0 and ROW_TILE % ATT_TILE == 0
    near = tuple((w, dil) for w, dil in BRANCHES if dil == 1)
    far = tuple((w, dil) for w, dil in BRANCHES if dil > 1)
    assert all(w <= NEAR for w, _ in near) and all(dil % STRIDE == 0 for _, dil in far)
    assert s % (STRIDE * ATT_TILE) == 0 and ROW_TILE % STRIDE == 0

    tv1 = _bias_rows(rel_bias, NEAR + 1, 1, near, 3 * ATT_TILE)
    tv2 = _bias_rows(rel_bias, s // STRIDE, STRIDE, far, s // STRIDE + ATT_TILE)
    gf = final_norm.reshape(1, d)

    def layer_params(i):
        mix_in, mix_out = (conv_w_in, conv_w_out) if i % 2 == 0 else (attn_w_qkv, attn_w_out)
        return [(mix_in, i // 2), (mix_out, i // 2),
                (ffn_w_gate, i), (ffn_w_up, i), (ffn_w_down, i)]

    mix_in = conv_w_in[0].astype(BF16)
    rest = None
    x2 = x.reshape(bsz * s, d)
    for i in range(depth):
        g_mix = mix_norm[i].reshape(1, d)
        if i % 2 == 0:
            jobs = layer_params(i)[1:] if i == 0 else []
            z, made = _conv_pre(x2, g_mix, mix_in, conv_kernel, i // 2, s, jobs)
            rest = made if i == 0 else rest
        else:
            qkv, qkv_s = _qkv_proj(x2, g_mix, mix_in, s)
            z = _attention(qkv.reshape(bsz, s, 3 * d), qkv_s, tv1, tv2).reshape(bsz * s, d)
        mix_out, w_g, w_u, w_d = rest
        jobs = layer_params(i + 1) if i + 1 < depth else []
        x2, made = _mix_ffn(x2, z, mix_out, ffn_norm[i].reshape(1, d), w_g, w_u, w_d, gf,
                            final=(i == depth - 1), cast_jobs=jobs)
        if made:
            mix_in, rest = made[0], made[1:]
    return x2.reshape(bsz, s, d)
```

```python
import functools
import math

import jax
import jax.numpy as jnp
from jax import lax
from jax.experimental import pallas as pl
from jax.experimental.pallas import tpu as pltpu

N_HEADS = 16
HEAD_DIM = 64
CONV_WIDTH = 3
BRANCHES = ((128, 1), (512, 4), (2048, 16))
NUM_BUCKETS = 32
MAX_DISTANCE = 2048
EPS = 1e-6
NEG_INF = -1e30
LOG2E = math.log2(math.e)

LANES = 128
ROW_TILE = 1024
COL_CHUNK = 512
FF_CHUNK = 256
PAIR = LANES // HEAD_DIM
ATT_TILE = 256
HEAD_GROUPS = 2
STRIDE = 4
NEAR = ATT_TILE // 2
F32_SUBLANES = 8
BF16_SUBLANES = 16
ONES_ROWS = BF16_SUBLANES
VMEM_LIMIT = 56 * 1024 * 1024

F32 = jnp.float32
BF16 = jnp.bfloat16
NT_DIMS = (((1,), (1,)), ((), ()))


def _rms(x, g):
    ms = jnp.mean(x * x, axis=-1, keepdims=True)
    return x * lax.rsqrt(ms + EPS) * g


def _const_spec(shape):
    nd = len(shape)
    return pl.BlockSpec(shape, lambda *_: (0,) * nd, pipeline_mode=pl.Buffered(1))


def _layer_spec(stacked, layer):
    tail = (0,) * (stacked.ndim - 1)
    return pl.BlockSpec((None,) + stacked.shape[1:], lambda *_: (layer,) + tail,
                        pipeline_mode=pl.Buffered(1))


def _cast_io(jobs, steps):
    in_specs, out_specs, out_shapes = [], [], []
    for w, layer in jobs:
        rows, cols = w.shape[1] // steps, w.shape[2]
        assert w.shape[1] % steps == 0 and rows % BF16_SUBLANES == 0
        in_specs.append(pl.BlockSpec((None, rows, cols), lambda i, layer=layer: (layer, i, 0)))
        out_specs.append(pl.BlockSpec((rows, cols), lambda i: (i, 0)))
        out_shapes.append(jax.ShapeDtypeStruct(w.shape[1:], BF16))
    return in_specs, out_specs, out_shapes


def _split_refs(refs, n_in, n_out, n_cast):
    a, b, c = n_in + n_cast, n_in + n_cast + n_out, n_in + 2 * n_cast + n_out
    return refs[:n_in], refs[n_in:a], refs[a:b], refs[b:c], refs[c:]


def _run_casts(srcs, dsts):
    for src, dst in zip(srcs, dsts):
        dst[...] = src[...].astype(BF16)


def _params(n_axes):
    return pltpu.CompilerParams(
        dimension_semantics=("arbitrary",) * n_axes, vmem_limit_bytes=VMEM_LIMIT)


def _qkv_body(x_ref, g_ref, w_ref, o_ref, os_ref, slab0_ref, slab1_ref):
    tm, d = x_ref.shape
    h = _rms(x_ref[...], g_ref[...]).astype(BF16)
    q_scale = LOG2E * HEAD_DIM ** -0.5
    for j in range(0, w_ref.shape[1], COL_CHUNK):
        slab_ref = (slab0_ref, slab1_ref)[(j // COL_CHUNK) % 2]
        y = jnp.dot(h, w_ref[:, j:j + COL_CHUNK], preferred_element_type=F32)
        if j < d:
            y = y * q_scale
        o_ref[:, j:j + COL_CHUNK] = y.astype(BF16)
        for c in range(COL_CHUNK // LANES):
            slab_ref[c] = y[:, c * LANES:(c + 1) * LANES]
        for r in range(STRIDE):
            for c in range(COL_CHUNK // LANES):
                part = slab_ref[c, pl.ds(r, tm // STRIDE, stride=STRIDE), :]
                os_ref[r, :, j + c * LANES:j + (c + 1) * LANES] = part.astype(BF16)


def _qkv_proj(x2, g, w, seq):
    t, d = x2.shape
    n = w.shape[1]
    tm = ROW_TILE
    per_seq = seq // tm
    return pl.pallas_call(
        _qkv_body,
        grid=(t // tm,),
        in_specs=[pl.BlockSpec((tm, d), lambda i: (i, 0)),
                  _const_spec((1, d)), _const_spec(w.shape)],
        out_specs=[pl.BlockSpec((tm, n), lambda i: (i, 0)),
                   pl.BlockSpec((None, STRIDE, tm // STRIDE, n),
                                lambda i: (i // per_seq, 0, i % per_seq, 0))],
        out_shape=[jax.ShapeDtypeStruct((t, n), BF16),
                   jax.ShapeDtypeStruct((t // seq, STRIDE, seq // STRIDE, n), BF16)],
        scratch_shapes=[pltpu.VMEM((COL_CHUNK // LANES, tm, LANES), F32)] * 2,
        compiler_params=_params(1),
        name="qkv_proj",
    )(x2, g, w)


def _conv_body(*refs, n_cast, per_seq):
    (x_ref, g_ref, w_ref, k_ref), cast_src, (z_ref,), cast_dst, (cu_ref,) = _split_refs(
        refs, 4, 1, n_cast)
    _run_casts(cast_src, cast_dst)
    tm, d = x_ref.shape
    pad = F32_SUBLANES
    first = pl.program_id(0) % per_seq == 0

    @pl.when(first)
    def _():
        cu_ref[0:pad, :] = jnp.zeros((pad, d), F32)

    @pl.when(jnp.logical_not(first))
    def _():
        cu_ref[0:pad, :] = cu_ref[tm:tm + pad, :]

    h = _rms(x_ref[...], g_ref[...]).astype(BF16)
    for j in range(0, d, COL_CHUNK):
        cols = slice(j, j + COL_CHUNK)
        c = jnp.dot(h, w_ref[:, d + j:d + j + COL_CHUNK], preferred_element_type=F32)
        u = jnp.dot(h, w_ref[:, 2 * d + j:2 * d + j + COL_CHUNK], preferred_element_type=F32)
        cu_ref[pad:pad + tm, cols] = c * u
        y = None
        for w in reversed(range(CONV_WIDTH)):
            start = pad - (CONV_WIDTH - 1 - w)
            term = k_ref[w:w + 1, cols] * cu_ref[start:start + tm, cols]
            y = term if y is None else y + term
        b = jnp.dot(h, w_ref[:, cols], preferred_element_type=F32)
        z_ref[:, cols] = (b * y).astype(BF16)


def _conv_pre(x2, g, w_in, taps, layer, seq, cast_jobs):
    t, d = x2.shape
    tm = ROW_TILE
    steps = t // tm
    cast_in, cast_out, cast_shapes = _cast_io(cast_jobs, steps)
    row = pl.BlockSpec((tm, d), lambda i: (i, 0))
    outs = pl.pallas_call(
        functools.partial(_conv_body, n_cast=len(cast_jobs), per_seq=seq // tm),
        grid=(steps,),
        in_specs=[row, _const_spec((1, d)), _const_spec(w_in.shape), _layer_spec(taps, layer)]
        + cast_in,
        out_specs=[row] + cast_out,
        out_shape=[jax.ShapeDtypeStruct((t, d), BF16)] + cast_shapes,
        scratch_shapes=[pltpu.VMEM((tm + F32_SUBLANES, d), F32)],
        compiler_params=_params(1),
        name="conv_pre",
    )(x2, g, w_in, taps, *[w for w, _ in cast_jobs])
    return outs[0], outs[1:]


def _ffn_body(*refs, final, n_cast):
    ins, cast_src, (o_ref,), cast_dst, _ = _split_refs(refs, 8, 1, n_cast)
    x_ref, z_ref, wo_ref, g_ref, wg_ref, wu_ref, wd_ref, gf_ref = ins
    _run_casts(cast_src, cast_dst)
    half = x_ref.shape[0] // 2
    hs = []
    for rows in (slice(0, half), slice(half, 2 * half)):
        x1 = x_ref[rows, :] + jnp.dot(z_ref[rows, :], wo_ref[...], preferred_element_type=F32)
        o_ref[rows, :] = x1
        hs.append(_rms(x1, g_ref[...]).astype(BF16))
    h = jnp.concatenate(hs, axis=0)
    for c in range(0, wg_ref.shape[1], FF_CHUNK):
        gate = jnp.dot(h, wg_ref[:, c:c + FF_CHUNK], preferred_element_type=F32)
        up = jnp.dot(h, wu_ref[:, c:c + FF_CHUNK], preferred_element_type=F32)
        a = (gate * jax.nn.sigmoid(gate) * up).astype(BF16)
        o_ref[...] += jnp.dot(a, wd_ref[c:c + FF_CHUNK, :], preferred_element_type=F32)
    if final:
        o_ref[...] = _rms(o_ref[...], gf_ref[...])


def _mix_ffn(x2, z2, wo, g, wg, wu, wd, gf, final, cast_jobs):
    t, d = x2.shape
    steps = t // ROW_TILE
    cast_in, cast_out, cast_shapes = _cast_io(cast_jobs, steps)
    row = pl.BlockSpec((ROW_TILE, d), lambda i: (i, 0))
    outs = pl.pallas_call(
        functools.partial(_ffn_body, final=final, n_cast=len(cast_jobs)),
        grid=(steps,),
        in_specs=[row, row, _const_spec(wo.shape), _const_spec((1, d)), _const_spec(wg.shape),
                  _const_spec(wu.shape), _const_spec(wd.shape), _const_spec((1, d))] + cast_in,
        out_specs=[row] + cast_out,
        out_shape=[jax.ShapeDtypeStruct((t, d), F32)] + cast_shapes,
        compiler_params=_params(1),
        name="mix_ffn_final" if final else "mix_ffn",
    )(x2, z2, wo, g, wg, wu, wd, gf, *[w for w, _ in cast_jobs])
    return outs[0], outs[1:]


def _t5_bucket(dist):
    exact = NUM_BUCKETS // 2
    df = jnp.maximum(dist, 1).astype(F32)
    large = exact + (jnp.log(df / exact) / math.log(MAX_DISTANCE / exact)
                     * (NUM_BUCKETS - exact)).astype(jnp.int32)
    large = jnp.minimum(large, NUM_BUCKETS - 1)
    return jnp.where(dist < exact, dist, large)


def _bias_rows(rel_bias, n_dist, stride, branches, period):
    dist = jnp.arange(n_dist, dtype=jnp.int32) * stride
    mult = jnp.zeros((n_dist,), F32)
    for window, dilation in branches:
        mult = mult + ((dist % dilation == 0) & (dist <= window)).astype(F32)
    table = rel_bias[_t5_bucket(dist)].astype(F32)
    total = jnp.where(mult[:, None] > 0,
                      (table + jnp.log(jnp.maximum(mult, 1.0))[:, None]) * LOG2E, NEG_INF)
    pad = jnp.full((period - n_dist, total.shape[1]), NEG_INF, F32)
    return jnp.concatenate([total, pad], axis=0).T[:, None, :]


def _attn_body(tv1_ref, tv2_ref, q_ref, k_ref, v_ref, qs_ref, ks_ref, vs_ref, o_ref,
               bias1_ref, bias10_ref, bias2_ref, lhs_ref, lhss_ref, o2_ref, lse2_ref):
    tq = ATT_TILE
    seq, width = k_ref.shape
    sub = ks_ref.shape[1]
    n_sub = sub // tq
    groups = width // LANES
    heads = width // HEAD_DIM

    @pl.when(pl.program_id(1) == 0)
    def _():
        for h in range(heads):
            row = jnp.broadcast_to(tv1_ref[h], (2 * NEAR, tv1_ref.shape[2]))
            toep = pltpu.roll(row, 0, 1, stride=1, stride_axis=0)
            bias10_ref[:, h * NEAR:(h + 1) * NEAR] = toep[:NEAR, :NEAR]
            bias1_ref[:, h * NEAR:(h + 1) * NEAR] = toep[:, NEAR:2 * NEAR]
            g, hg = divmod(h, PAIR)
            cols = slice(hg * tq, (hg + 1) * tq)
            row = jnp.broadcast_to(tv2_ref[h], (tq, tv2_ref.shape[2]))
            toep = pltpu.roll(row, 0, 1, stride=1, stride_axis=0)
            for dist in range(n_sub):
                start = (n_sub - 1 - dist) * tq
                bias2_ref[g, start:start + tq, cols] = toep[:, dist * tq:(dist + 1) * tq]

    for kb in range(seq // tq):
        lhs_ref[:width, kb * tq:(kb + 1) * tq] = v_ref[kb * tq:(kb + 1) * tq, :].T
    lhs_ref[width:, :] = jnp.ones((ONES_ROWS, seq), BF16)
    for g in range(groups):
        lanes = slice(g * LANES, (g + 1) * LANES)
        for r in range(STRIDE):
            for kb in range(n_sub):
                lhss_ref[g, r, :LANES, kb * tq:(kb + 1) * tq] = (
                    vs_ref[r, kb * tq:(kb + 1) * tq, lanes].T)
            lhss_ref[g, r, LANES:, :] = jnp.ones((ONES_ROWS, sub), BF16)

    lane_head = lax.broadcasted_iota(jnp.int32, (1, width), 1) // HEAD_DIM
    items = [(g, r, j, 0, (j + 1) * tq)
             for g in range(groups) for r in range(STRIDE) for j in range(n_sub)]
    items += [(None, None, qb, max(0, (qb - 1) * NEAR), (qb + 1) * NEAR)
              for qb in range(seq // NEAR)]
    col_max, scores, probs = {}, {}, {}
    half = tq // 2
    late_cols = (half, tq + half)

    def per_head_rows(q):
        zero = jnp.zeros_like(q)
        lanes = lane_head[:, :q.shape[1]]
        return jnp.concatenate(
            [jnp.where(lanes == h, q, zero) for h in range(q.shape[1] // HEAD_DIM)], axis=0)

    def stage_a(t):
        g, r, qb, lo, hi = items[t]
        if g is None:
            q = per_head_rows(q_ref[qb * NEAR:(qb + 1) * NEAR, :])
            k = k_ref[lo:hi, :]
            bias = bias10_ref[...] if lo == qb * NEAR else bias1_ref[...]
        else:
            lanes = slice(g * LANES, (g + 1) * LANES)
            q = per_head_rows(qs_ref[r, qb * tq:(qb + 1) * tq, lanes])
            k = ks_ref[r, lo:hi, lanes]
            bias = bias2_ref[g, sub - hi:, :]
        s = lax.dot_general(k, q, NT_DIMS, preferred_element_type=F32) + bias
        if g is None:
            s = s.astype(BF16)
            scores[t] = (s, None)
            col_max[t] = jnp.max(s, axis=0, keepdims=True)
        else:
            old = s[:hi - lo - half].astype(BF16)
            new = [s[hi - lo - half:, c:c + half].astype(BF16) for c in late_cols]
            scores[t] = (old, new)
            m_old = jnp.max(old, axis=0, keepdims=True)
            m_new = [jnp.max(x, axis=0, keepdims=True) for x in new]
            col_max[t] = jnp.concatenate(
                [m_old[:, :half], jnp.maximum(m_old[:, half:tq], m_new[0]),
                 m_old[:, tq:tq + half], jnp.maximum(m_old[:, tq + half:], m_new[1])], axis=1)

    def stage_b(t):
        old, new = scores.pop(t)
        m = col_max[t]
        p = jnp.exp2(old - m)
        if new is not None:
            zero = jnp.zeros((half, half), BF16)
            p_new = [jnp.exp2(x - m[:, c:c + half]) for x, c in zip(new, late_cols)]
            p = jnp.concatenate(
                [p, jnp.concatenate([zero, p_new[0], zero, p_new[1]], axis=1)], axis=0)
        probs[t] = p

    def stage_c(t):
        g, r, qb, lo, hi = items[t]
        p, m = probs.pop(t), col_max.pop(t).astype(F32)
        lhs = lhs_ref[:, lo:hi] if g is None else lhss_ref[g, r, :, lo:hi]
        acc = jnp.dot(lhs, p, preferred_element_type=F32)
        n_heads = (lhs.shape[0] - ONES_ROWS) // HEAD_DIM
        nq = acc.shape[1] // n_heads
        den = acc[n_heads * HEAD_DIM:n_heads * HEAD_DIM + 1]
        lse = m + jnp.log2(den)
        outs, lses = [], []
        for h in range(n_heads):
            cols = slice(h * nq, (h + 1) * nq)
            outs.append(acc[h * HEAD_DIM:(h + 1) * HEAD_DIM, cols] / den[:, cols])
            lses.append(jnp.broadcast_to(lse[:, cols], (HEAD_DIM, nq)))
        out_t, lse_t = jnp.concatenate(outs, axis=0), jnp.concatenate(lses, axis=0)
        out, lse = out_t.T, lse_t.T
        if g is not None:
            token_rows = pl.ds(qb * tq * STRIDE + r, tq, stride=STRIDE)
            o2_ref[g, token_rows, :] = out
            lse2_ref[g, token_rows, :] = lse
        else:
            rows = slice(qb * NEAR, (qb + 1) * NEAR)
            out2 = jnp.concatenate([o2_ref[i, rows, :] for i in range(groups)], axis=1)
            lse2 = jnp.concatenate([lse2_ref[i, rows, :] for i in range(groups)], axis=1)
            share = 1.0 / (1.0 + jnp.exp2(lse2 - lse))
            o_ref[rows, :] = (out2 + share * (out - out2)).astype(BF16)

    for t in range(len(items) + 2):
        if t < len(items):
            stage_a(t)
        if 1 <= t <= len(items):
            stage_b(t - 1)
        if t >= 2:
            stage_c(t - 2)


def _attention(qkv, qkv_s, tv1, tv2):
    bsz, s, d3 = qkv.shape
    d = d3 // 3
    width = HEAD_GROUPS * LANES
    heads = width // HEAD_DIM
    n_blocks = d // width
    sub = s // STRIDE
    col = lambda off: pl.BlockSpec((None, s, width), lambda hb, b: (b, 0, off + hb))
    col_s = lambda off: pl.BlockSpec((None, STRIDE, sub, width),
                                     lambda hb, b: (b, 0, 0, off + hb))
    bias_row = lambda tv: pl.BlockSpec((heads, 1, tv.shape[2]), lambda hb, b: (hb, 0, 0))
    wide = 2 * ATT_TILE
    assert heads * NEAR == wide
    return pl.pallas_call(
        _attn_body,
        grid=(n_blocks, bsz),
        in_specs=[bias_row(tv1), bias_row(tv2),
                  col(0), col(n_blocks), col(2 * n_blocks),
                  col_s(0), col_s(n_blocks), col_s(2 * n_blocks)],
        out_specs=col(0),
        out_shape=jax.ShapeDtypeStruct((bsz, s, d), BF16),
        scratch_shapes=[pltpu.VMEM((2 * NEAR, wide), F32),
                        pltpu.VMEM((NEAR, wide), F32),
                        pltpu.VMEM((HEAD_GROUPS, sub, wide), F32),
                        pltpu.VMEM((width + ONES_ROWS, s), BF16),
                        pltpu.VMEM((HEAD_GROUPS, STRIDE, LANES + ONES_ROWS, sub), BF16),
                        pltpu.VMEM((HEAD_GROUPS, s, LANES), F32),
                        pltpu.VMEM((HEAD_GROUPS, s, LANES), F32)],
        compiler_params=_params(2),
        name="dilated_attention",
    )(tv1, tv2, qkv, qkv, qkv, qkv_s, qkv_s, qkv_s)


def kernel(x, mix_norm, ffn_norm, final_norm, conv_w_in, conv_kernel, conv_w_out,
           attn_w_qkv, attn_w_out, rel_bias, ffn_w_gate, ffn_w_up, ffn_w_down):
    bsz, s, d = x.shape
    depth = mix_norm.shape[0]
    assert d == N_HEADS * HEAD_DIM and s % ROW_TILE == 0 and ROW_TILE % ATT_TILE == 0
    near = tuple((w, dil) for w, dil in BRANCHES if dil == 1)
    far = tuple((w, dil) for w, dil in BRANCHES if dil > 1)
    assert all(w <= NEAR for w, _ in near) and all(dil % STRIDE == 0 for _, dil in far)
    assert s % (STRIDE * ATT_TILE) == 0 and ROW_TILE % STRIDE == 0

    tv1 = _bias_rows(rel_bias, NEAR + 1, 1, near, 3 * ATT_TILE)
    tv2 = _bias_rows(rel_bias, s // STRIDE, STRIDE, far, s // STRIDE + ATT_TILE)
    gf = final_norm.reshape(1, d)

    def layer_params(i):
        mix_in, mix_out = (conv_w_in, conv_w_out) if i % 2 == 0 else (attn_w_qkv, attn_w_out)
        return [(mix_in, i // 2), (mix_out, i // 2),
                (ffn_w_gate, i), (ffn_w_up, i), (ffn_w_down, i)]

    mix_in = conv_w_in[0].astype(BF16)
    rest = None
    x2 = x.reshape(bsz * s, d)
    for i in range(depth):
        g_mix = mix_norm[i].reshape(1, d)
        if i % 2 == 0:
            jobs = layer_params(i)[1:] if i == 0 else []
            z, made = _conv_pre(x2, g_mix, mix_in, conv_kernel, i // 2, s, jobs)
            rest = made if i == 0 else rest
        else:
            qkv, qkv_s = _qkv_proj(x2, g_mix, mix_in, s)
            z = _attention(qkv.reshape(bsz, s, 3 * d), qkv_s, tv1, tv2).reshape(bsz * s, d)
        mix_out, w_g, w_u, w_d = rest
        jobs = layer_params(i + 1) if i + 1 < depth else []
        x2, made = _mix_ffn(x2, z, mix_out, ffn_norm[i].reshape(1, d), w_g, w_u, w_d, gf,
                            final=(i == depth - 1), cast_jobs=jobs)
        if made:
            mix_in, rest = made[0], made[1:]
    return x2.reshape(bsz, s, d)
```

```python
import functools
import math

import jax
import jax.numpy as jnp
from jax import lax
from jax.experimental import pallas as pl
from jax.experimental.pallas import tpu as pltpu

N_HEADS = 16
HEAD_DIM = 64
CONV_WIDTH = 3
BRANCHES = ((128, 1), (512, 4), (2048, 16))
NUM_BUCKETS = 32
MAX_DISTANCE = 2048
EPS = 1e-6
NEG_INF = -1e30
LOG2E = math.log2(math.e)

LANES = 128
ROW_TILE = 1024
COL_CHUNK = 512
FF_CHUNK = 256
PAIR = LANES // HEAD_DIM
ATT_TILE = 256
HEAD_GROUPS = 2
STRIDE = 4
FF_GROUP = 6 * FF_CHUNK
NEAR = ATT_TILE // 2
F32_SUBLANES = 8
BF16_SUBLANES = 16
ONES_ROWS = BF16_SUBLANES
VMEM_LIMIT = 56 * 1024 * 1024

F32 = jnp.float32
BF16 = jnp.bfloat16
NT_DIMS = (((1,), (1,)), ((), ()))


def _rms(x, g):
    ms = jnp.mean(x * x, axis=-1, keepdims=True)
    return x * lax.rsqrt(ms + EPS) * g


def _const_spec(shape):
    nd = len(shape)
    return pl.BlockSpec(shape, lambda *_: (0,) * nd, pipeline_mode=pl.Buffered(1))


def _layer_spec(stacked, layer):
    tail = (0,) * (stacked.ndim - 1)
    return pl.BlockSpec((None,) + stacked.shape[1:], lambda *_: (layer,) + tail,
                        pipeline_mode=pl.Buffered(1))


def _cast_io(jobs, steps):
    in_specs, out_specs, out_shapes = [], [], []
    for w, layer in jobs:
        rows, cols = w.shape[1] // steps, w.shape[2]
        assert w.shape[1] % steps == 0 and rows % BF16_SUBLANES == 0
        in_specs.append(pl.BlockSpec((None, rows, cols), lambda i, layer=layer: (layer, i, 0)))
        out_specs.append(pl.BlockSpec((rows, cols), lambda i: (i, 0)))
        out_shapes.append(jax.ShapeDtypeStruct(w.shape[1:], BF16))
    return in_specs, out_specs, out_shapes


def _split_refs(refs, n_in, n_out, n_cast):
    a, b, c = n_in + n_cast, n_in + n_cast + n_out, n_in + 2 * n_cast + n_out
    return refs[:n_in], refs[n_in:a], refs[a:b], refs[b:c], refs[c:]


def _run_casts(srcs, dsts):
    for src, dst in zip(srcs, dsts):
        dst[...] = src[...].astype(BF16)


def _params(n_axes):
    return pltpu.CompilerParams(
        dimension_semantics=("arbitrary",) * n_axes, vmem_limit_bytes=VMEM_LIMIT)


def _qkv_body(x_ref, g_ref, w_ref, o_ref, os_ref, slab0_ref, slab1_ref):
    tm, d = x_ref.shape
    h = _rms(x_ref[...], g_ref[...]).astype(BF16)
    q_scale = LOG2E * HEAD_DIM ** -0.5
    for j in range(0, w_ref.shape[1], COL_CHUNK):
        slab_ref = (slab0_ref, slab1_ref)[(j // COL_CHUNK) % 2]
        y = jnp.dot(h, w_ref[:, j:j + COL_CHUNK], preferred_element_type=F32)
        if j < d:
            y = y * q_scale
        o_ref[:, j:j + COL_CHUNK] = y.astype(BF16)
        for c in range(COL_CHUNK // LANES):
            slab_ref[c] = y[:, c * LANES:(c + 1) * LANES]
        for r in range(STRIDE):
            for c in range(COL_CHUNK // LANES):
                part = slab_ref[c, pl.ds(r, tm // STRIDE, stride=STRIDE), :]
                os_ref[r, :, j + c * LANES:j + (c + 1) * LANES] = part.astype(BF16)


def _qkv_proj(x2, g, w, seq):
    t, d = x2.shape
    n = w.shape[1]
    tm = ROW_TILE
    per_seq = seq // tm
    return pl.pallas_call(
        _qkv_body,
        grid=(t // tm,),
        in_specs=[pl.BlockSpec((tm, d), lambda i: (i, 0)),
                  _const_spec((1, d)), _const_spec(w.shape)],
        out_specs=[pl.BlockSpec((tm, n), lambda i: (i, 0)),
                   pl.BlockSpec((None, STRIDE, tm // STRIDE, n),
                                lambda i: (i // per_seq, 0, i % per_seq, 0))],
        out_shape=[jax.ShapeDtypeStruct((t, n), BF16),
                   jax.ShapeDtypeStruct((t // seq, STRIDE, seq // STRIDE, n), BF16)],
        scratch_shapes=[pltpu.VMEM((COL_CHUNK // LANES, tm, LANES), F32)] * 2,
        compiler_params=_params(1),
        name="qkv_proj",
    )(x2, g, w)


def _conv_body(*refs, n_cast, per_seq):
    (x_ref, g_ref, w_ref, k_ref), cast_src, (z_ref,), cast_dst, (cu_ref,) = _split_refs(
        refs, 4, 1, n_cast)
    _run_casts(cast_src, cast_dst)
    tm, d = x_ref.shape
    pad = F32_SUBLANES
    first = pl.program_id(0) % per_seq == 0

    @pl.when(first)
    def _():
        cu_ref[0:pad, :] = jnp.zeros((pad, d), F32)

    @pl.when(jnp.logical_not(first))
    def _():
        cu_ref[0:pad, :] = cu_ref[tm:tm + pad, :]

    h = _rms(x_ref[...], g_ref[...]).astype(BF16)
    for j in range(0, d, COL_CHUNK):
        cols = slice(j, j + COL_CHUNK)
        c = jnp.dot(h, w_ref[:, d + j:d + j + COL_CHUNK], preferred_element_type=F32)
        u = jnp.dot(h, w_ref[:, 2 * d + j:2 * d + j + COL_CHUNK], preferred_element_type=F32)
        cu_ref[pad:pad + tm, cols] = c * u
        y = None
        for w in reversed(range(CONV_WIDTH)):
            start = pad - (CONV_WIDTH - 1 - w)
            term = k_ref[w:w + 1, cols] * cu_ref[start:start + tm, cols]
            y = term if y is None else y + term
        b = jnp.dot(h, w_ref[:, cols], preferred_element_type=F32)
        z_ref[:, cols] = (b * y).astype(BF16)


def _conv_pre(x2, g, w_in, taps, layer, seq, cast_jobs):
    t, d = x2.shape
    tm = ROW_TILE
    steps = t // tm
    cast_in, cast_out, cast_shapes = _cast_io(cast_jobs, steps)
    row = pl.BlockSpec((tm, d), lambda i: (i, 0))
    outs = pl.pallas_call(
        functools.partial(_conv_body, n_cast=len(cast_jobs), per_seq=seq // tm),
        grid=(steps,),
        in_specs=[row, _const_spec((1, d)), _const_spec(w_in.shape), _layer_spec(taps, layer)]
        + cast_in,
        out_specs=[row] + cast_out,
        out_shape=[jax.ShapeDtypeStruct((t, d), BF16)] + cast_shapes,
        scratch_shapes=[pltpu.VMEM((tm + F32_SUBLANES, d), F32)],
        compiler_params=_params(1),
        name="conv_pre",
    )(x2, g, w_in, taps, *[w for w, _ in cast_jobs])
    return outs[0], outs[1:]


def _ffn_body(*refs, final, n_cast):
    ins, cast_src, (o_ref,), cast_dst, (a_ref,) = _split_refs(refs, 8, 1, n_cast)
    x_ref, z_ref, wo_ref, g_ref, wg_ref, wu_ref, wd_ref, gf_ref = ins
    _run_casts(cast_src, cast_dst)
    x1 = x_ref[...] + jnp.dot(z_ref[...], wo_ref[...], preferred_element_type=F32)
    h = _rms(x1, g_ref[...]).astype(BF16)
    o_ref[...] = x1
    d_ff, group = wg_ref.shape[1], a_ref.shape[1]
    for lo in range(0, d_ff, group):
        hi = min(lo + group, d_ff)
        for c in range(lo, hi, FF_CHUNK):
            gate = jnp.dot(h, wg_ref[:, c:c + FF_CHUNK], preferred_element_type=F32)
            up = jnp.dot(h, wu_ref[:, c:c + FF_CHUNK], preferred_element_type=F32)
            a_ref[:, c - lo:c - lo + FF_CHUNK] = (gate * jax.nn.sigmoid(gate) * up).astype(BF16)
        o_ref[...] += jnp.dot(a_ref[:, :hi - lo], wd_ref[lo:hi, :], preferred_element_type=F32)
    if final:
        o_ref[...] = _rms(o_ref[...], gf_ref[...])


def _mix_ffn(x2, z2, wo, g, wg, wu, wd, gf, final, cast_jobs):
    t, d = x2.shape
    steps = t // ROW_TILE
    cast_in, cast_out, cast_shapes = _cast_io(cast_jobs, steps)
    row = pl.BlockSpec((ROW_TILE, d), lambda i: (i, 0))
    outs = pl.pallas_call(
        functools.partial(_ffn_body, final=final, n_cast=len(cast_jobs)),
        grid=(steps,),
        in_specs=[row, row, _const_spec(wo.shape), _const_spec((1, d)), _const_spec(wg.shape),
                  _const_spec(wu.shape), _const_spec(wd.shape), _const_spec((1, d))] + cast_in,
        out_specs=[row] + cast_out,
        out_shape=[jax.ShapeDtypeStruct((t, d), F32)] + cast_shapes,
        scratch_shapes=[pltpu.VMEM((ROW_TILE, FF_GROUP), BF16)],
        compiler_params=_params(1),
        name="mix_ffn_final" if final else "mix_ffn",
    )(x2, z2, wo, g, wg, wu, wd, gf, *[w for w, _ in cast_jobs])
    return outs[0], outs[1:]


def _t5_bucket(dist):
    exact = NUM_BUCKETS // 2
    df = jnp.maximum(dist, 1).astype(F32)
    large = exact + (jnp.log(df / exact) / math.log(MAX_DISTANCE / exact)
                     * (NUM_BUCKETS - exact)).astype(jnp.int32)
    large = jnp.minimum(large, NUM_BUCKETS - 1)
    return jnp.where(dist < exact, dist, large)


def _bias_rows(rel_bias, n_dist, stride, branches, period):
    dist = jnp.arange(n_dist, dtype=jnp.int32) * stride
    mult = jnp.zeros((n_dist,), F32)
    for window, dilation in branches:
        mult = mult + ((dist % dilation == 0) & (dist <= window)).astype(F32)
    table = rel_bias[_t5_bucket(dist)].astype(F32)
    total = jnp.where(mult[:, None] > 0,
                      (table + jnp.log(jnp.maximum(mult, 1.0))[:, None]) * LOG2E, NEG_INF)
    pad = jnp.full((period - n_dist, total.shape[1]), NEG_INF, F32)
    return jnp.concatenate([total, pad], axis=0).T[:, None, :]


def _attn_body(tv1_ref, tv2_ref, q_ref, k_ref, v_ref, qs_ref, ks_ref, vs_ref, o_ref,
               bias1_ref, bias10_ref, bias2_ref, lhs_ref, lhss_ref, o2_ref, lse2_ref):
    tq = ATT_TILE
    seq, width = k_ref.shape
    sub = ks_ref.shape[1]
    n_sub = sub // tq
    groups = width // LANES
    heads = width // HEAD_DIM

    @pl.when(pl.program_id(1) == 0)
    def _():
        for h in range(heads):
            row = jnp.broadcast_to(tv1_ref[h], (2 * NEAR, tv1_ref.shape[2]))
            toep = pltpu.roll(row, 0, 1, stride=1, stride_axis=0)
            bias10_ref[:, h * NEAR:(h + 1) * NEAR] = toep[:NEAR, :NEAR]
            bias1_ref[:, h * NEAR:(h + 1) * NEAR] = toep[:, NEAR:2 * NEAR]
            g, hg = divmod(h, PAIR)
            cols = slice(hg * tq, (hg + 1) * tq)
            row = jnp.broadcast_to(tv2_ref[h], (tq, tv2_ref.shape[2]))
            toep = pltpu.roll(row, 0, 1, stride=1, stride_axis=0)
            for dist in range(n_sub):
                start = (n_sub - 1 - dist) * tq
                bias2_ref[g, start:start + tq, cols] = toep[:, dist * tq:(dist + 1) * tq]

    for kb in range(seq // tq):
        lhs_ref[:width, kb * tq:(kb + 1) * tq] = v_ref[kb * tq:(kb + 1) * tq, :].T
    lhs_ref[width:, :] = jnp.ones((ONES_ROWS, seq), BF16)
    for g in range(groups):
        lanes = slice(g * LANES, (g + 1) * LANES)
        for r in range(STRIDE):
            for kb in range(n_sub):
                lhss_ref[g, r, :LANES, kb * tq:(kb + 1) * tq] = (
                    vs_ref[r, kb * tq:(kb + 1) * tq, lanes].T)
            lhss_ref[g, r, LANES:, :] = jnp.ones((ONES_ROWS, sub), BF16)

    lane_head = lax.broadcasted_iota(jnp.int32, (1, width), 1) // HEAD_DIM
    items = [(g, r, j, 0, (j + 1) * tq)
             for g in range(groups) for r in range(STRIDE) for j in range(n_sub)]
    items += [(None, None, qb, max(0, (qb - 1) * NEAR), (qb + 1) * NEAR)
              for qb in range(seq // NEAR)]
    col_max, scores, probs = {}, {}, {}
    half = tq // 2
    late_cols = (half, tq + half)

    def per_head_rows(q):
        zero = jnp.zeros_like(q)
        lanes = lane_head[:, :q.shape[1]]
        return jnp.concatenate(
            [jnp.where(lanes == h, q, zero) for h in range(q.shape[1] // HEAD_DIM)], axis=0)

    def stage_a(t):
        g, r, qb, lo, hi = items[t]
        if g is None:
            q = per_head_rows(q_ref[qb * NEAR:(qb + 1) * NEAR, :])
            k = k_ref[lo:hi, :]
            bias = bias10_ref[...] if lo == qb * NEAR else bias1_ref[...]
        else:
            lanes = slice(g * LANES, (g + 1) * LANES)
            q = per_head_rows(qs_ref[r, qb * tq:(qb + 1) * tq, lanes])
            k = ks_ref[r, lo:hi, lanes]
            bias = bias2_ref[g, sub - hi:, :]
        s = lax.dot_general(k, q, NT_DIMS, preferred_element_type=F32) + bias
        if g is None:
            s = s.astype(BF16)
            scores[t] = (s, None)
            col_max[t] = jnp.max(s, axis=0, keepdims=True)
        else:
            old = s[:hi - lo - half].astype(BF16)
            new = [s[hi - lo - half:, c:c + half].astype(BF16) for c in late_cols]
            scores[t] = (old, new)
            m_old = jnp.max(old, axis=0, keepdims=True)
            m_new = [jnp.max(x, axis=0, keepdims=True) for x in new]
            col_max[t] = jnp.concatenate(
                [m_old[:, :half], jnp.maximum(m_old[:, half:tq], m_new[0]),
                 m_old[:, tq:tq + half], jnp.maximum(m_old[:, tq + half:], m_new[1])], axis=1)

    def stage_b(t):
        old, new = scores.pop(t)
        m = col_max[t]
        p = jnp.exp2(old - m)
        if new is not None:
            zero = jnp.zeros((half, half), BF16)
            p_new = [jnp.exp2(x - m[:, c:c + half]) for x, c in zip(new, late_cols)]
            p = jnp.concatenate(
                [p, jnp.concatenate([zero, p_new[0], zero, p_new[1]], axis=1)], axis=0)
        probs[t] = p

    def stage_c(t):
        g, r, qb, lo, hi = items[t]
        p, m = probs.pop(t), col_max.pop(t).astype(F32)
        lhs = lhs_ref[:, lo:hi] if g is None else lhss_ref[g, r, :, lo:hi]
        acc = jnp.dot(lhs, p, preferred_element_type=F32)
        n_heads = (lhs.shape[0] - ONES_ROWS) // HEAD_DIM
        nq = acc.shape[1] // n_heads
        den = acc[n_heads * HEAD_DIM:n_heads * HEAD_DIM + 1]
        lse = m + jnp.log2(den)
        outs, lses = [], []
        for h in range(n_heads):
            cols = slice(h * nq, (h + 1) * nq)
            outs.append(acc[h * HEAD_DIM:(h + 1) * HEAD_DIM, cols] / den[:, cols])
            lses.append(jnp.broadcast_to(lse[:, cols], (HEAD_DIM, nq)))
        out_t, lse_t = jnp.concatenate(outs, axis=0), jnp.concatenate(lses, axis=0)
        out, lse = out_t.T, lse_t.T
        if g is not None:
            token_rows = pl.ds(qb * tq * STRIDE + r, tq, stride=STRIDE)
            o2_ref[g, token_rows, :] = out
            lse2_ref[g, token_rows, :] = lse
        else:
            rows = slice(qb * NEAR, (qb + 1) * NEAR)
            out2 = jnp.concatenate([o2_ref[i, rows, :] for i in range(groups)], axis=1)
            lse2 = jnp.concatenate([lse2_ref[i, rows, :] for i in range(groups)], axis=1)
            share = 1.0 / (1.0 + jnp.exp2(lse2 - lse))
            o_ref[rows, :] = (out2 + share * (out - out2)).astype(BF16)

    for t in range(len(items) + 2):
        if t < len(items):
            stage_a(t)
        if 1 <= t <= len(items):
            stage_b(t - 1)
        if t >= 2:
            stage_c(t - 2)


def _attention(qkv, qkv_s, tv1, tv2):
    bsz, s, d3 = qkv.shape
    d = d3 // 3
    width = HEAD_GROUPS * LANES
    heads = width // HEAD_DIM
    n_blocks = d // width
    sub = s // STRIDE
    col = lambda off: pl.BlockSpec((None, s, width), lambda hb, b: (b, 0, off + hb))
    col_s = lambda off: pl.BlockSpec((None, STRIDE, sub, width),
                                     lambda hb, b: (b, 0, 0, off + hb))
    bias_row = lambda tv: pl.BlockSpec((heads, 1, tv.shape[2]), lambda hb, b: (hb, 0, 0))
    wide = 2 * ATT_TILE
    assert heads * NEAR == wide
    return pl.pallas_call(
        _attn_body,
        grid=(n_blocks, bsz),
        in_specs=[bias_row(tv1), bias_row(tv2),
                  col(0), col(n_blocks), col(2 * n_blocks),
                  col_s(0), col_s(n_blocks), col_s(2 * n_blocks)],
        out_specs=col(0),
        out_shape=jax.ShapeDtypeStruct((bsz, s, d), BF16),
        scratch_shapes=[pltpu.VMEM((2 * NEAR, wide), F32),
                        pltpu.VMEM((NEAR, wide), F32),
                        pltpu.VMEM((HEAD_GROUPS, sub, wide), F32),
                        pltpu.VMEM((width + ONES_ROWS, s), BF16),
                        pltpu.VMEM((HEAD_GROUPS, STRIDE, LANES + ONES_ROWS, sub), BF16),
                        pltpu.VMEM((HEAD_GROUPS, s, LANES), F32),
                        pltpu.VMEM((HEAD_GROUPS, s, LANES), F32)],
        compiler_params=_params(2),
        name="dilated_attention",
    )(tv1, tv2, qkv, qkv, qkv, qkv_s, qkv_s, qkv_s)


def kernel(x, mix_norm, ffn_norm, final_norm, conv_w_in, conv_kernel, conv_w_out,
           attn_w_qkv, attn_w_out, rel_bias, ffn_w_gate, ffn_w_up, ffn_w_down):
    bsz, s, d = x.shape
    depth = mix_norm.shape[0]
    assert d == N_HEADS * HEAD_DIM and s % ROW_TILE == 0 and ROW_TILE % ATT_TILE == 0
    near = tuple((w, dil) for w, dil in BRANCHES if dil == 1)
    far = tuple((w, dil) for w, dil in BRANCHES if dil > 1)
    assert all(w <= NEAR for w, _ in near) and all(dil % STRIDE == 0 for _, dil in far)
    assert s % (STRIDE * ATT_TILE) == 0 and ROW_TILE % STRIDE == 0

    tv1 = _bias_rows(rel_bias, NEAR + 1, 1, near, 3 * ATT_TILE)
    tv2 = _bias_rows(rel_bias, s // STRIDE, STRIDE, far, s // STRIDE + ATT_TILE)
    gf = final_norm.reshape(1, d)

    def layer_params(i):
        mix_in, mix_out = (conv_w_in, conv_w_out) if i % 2 == 0 else (attn_w_qkv, attn_w_out)
        return [(mix_in, i // 2), (mix_out, i // 2),
                (ffn_w_gate, i), (ffn_w_up, i), (ffn_w_down, i)]

    mix_in = conv_w_in[0].astype(BF16)
    rest = None
    x2 = x.reshape(bsz * s, d)
    for i in range(depth):
        g_mix = mix_norm[i].reshape(1, d)
        if i % 2 == 0:
            jobs = layer_params(i)[1:] if i == 0 else []
            z, made = _conv_pre(x2, g_mix, mix_in, conv_kernel, i // 2, s, jobs)
            rest = made if i == 0 else rest
        else:
            qkv, qkv_s = _qkv_proj(x2, g_mix, mix_in, s)
            z = _attention(qkv.reshape(bsz, s, 3 * d), qkv_s, tv1, tv2).reshape(bsz * s, d)
        mix_out, w_g, w_u, w_d = rest
        jobs = layer_params(i + 1) if i + 1 < depth else []
        x2, made = _mix_ffn(x2, z, mix_out, ffn_norm[i].reshape(1, d), w_g, w_u, w_d, gf,
                            final=(i == depth - 1), cast_jobs=jobs)
        if made:
            mix_in, rest = made[0], made[1:]
    return x2.reshape(bsz, s, d)
```

```python
import functools
import math

import jax
import jax.numpy as jnp
from jax import lax
from jax.experimental import pallas as pl
from jax.experimental.pallas import tpu as pltpu

N_HEADS = 16
HEAD_DIM = 64
CONV_WIDTH = 3
BRANCHES = ((128, 1), (512, 4), (2048, 16))
NUM_BUCKETS = 32
MAX_DISTANCE = 2048
EPS = 1e-6
NEG_INF = -1e30
LOG2E = math.log2(math.e)

LANES = 128
ROW_TILE = 1024
COL_CHUNK = 512
FF_CHUNK = 256
PAIR = LANES // HEAD_DIM
ATT_TILE = 256
HEAD_GROUPS = 2
STRIDE = 4
FF_GROUP = 4 * FF_CHUNK
NEAR = ATT_TILE // 2
F32_SUBLANES = 8
BF16_SUBLANES = 16
ONES_ROWS = BF16_SUBLANES
VMEM_LIMIT = 56 * 1024 * 1024

F32 = jnp.float32
BF16 = jnp.bfloat16
NT_DIMS = (((1,), (1,)), ((), ()))


def _rms(x, g):
    ms = jnp.mean(x * x, axis=-1, keepdims=True)
    return x * lax.rsqrt(ms + EPS) * g


def _const_spec(shape):
    nd = len(shape)
    return pl.BlockSpec(shape, lambda *_: (0,) * nd, pipeline_mode=pl.Buffered(1))


def _layer_spec(stacked, layer):
    tail = (0,) * (stacked.ndim - 1)
    return pl.BlockSpec((None,) + stacked.shape[1:], lambda *_: (layer,) + tail,
                        pipeline_mode=pl.Buffered(1))


def _cast_io(jobs, steps):
    in_specs, out_specs, out_shapes = [], [], []
    for w, layer in jobs:
        rows, cols = w.shape[1] // steps, w.shape[2]
        assert w.shape[1] % steps == 0 and rows % BF16_SUBLANES == 0
        in_specs.append(pl.BlockSpec((None, rows, cols), lambda i, layer=layer: (layer, i, 0)))
        out_specs.append(pl.BlockSpec((rows, cols), lambda i: (i, 0)))
        out_shapes.append(jax.ShapeDtypeStruct(w.shape[1:], BF16))
    return in_specs, out_specs, out_shapes


def _split_refs(refs, n_in, n_out, n_cast):
    a, b, c = n_in + n_cast, n_in + n_cast + n_out, n_in + 2 * n_cast + n_out
    return refs[:n_in], refs[n_in:a], refs[a:b], refs[b:c], refs[c:]


def _run_casts(srcs, dsts):
    for src, dst in zip(srcs, dsts):
        dst[...] = src[...].astype(BF16)


def _params(n_axes):
    return pltpu.CompilerParams(
        dimension_semantics=("arbitrary",) * n_axes, vmem_limit_bytes=VMEM_LIMIT)


def _qkv_body(x_ref, g_ref, w_ref, o_ref, os_ref, slab0_ref, slab1_ref):
    tm, d = x_ref.shape
    h = _rms(x_ref[...], g_ref[...]).astype(BF16)
    q_scale = LOG2E * HEAD_DIM ** -0.5
    for j in range(0, w_ref.shape[1], COL_CHUNK):
        slab_ref = (slab0_ref, slab1_ref)[(j // COL_CHUNK) % 2]
        y = jnp.dot(h, w_ref[:, j:j + COL_CHUNK], preferred_element_type=F32)
        if j < d:
            y = y * q_scale
        o_ref[:, j:j + COL_CHUNK] = y.astype(BF16)
        for c in range(COL_CHUNK // LANES):
            slab_ref[c] = y[:, c * LANES:(c + 1) * LANES]
        for r in range(STRIDE):
            for c in range(COL_CHUNK // LANES):
                part = slab_ref[c, pl.ds(r, tm // STRIDE, stride=STRIDE), :]
                os_ref[r, :, j + c * LANES:j + (c + 1) * LANES] = part.astype(BF16)


def _qkv_proj(x2, g, w, seq):
    t, d = x2.shape
    n = w.shape[1]
    tm = ROW_TILE
    per_seq = seq // tm
    return pl.pallas_call(
        _qkv_body,
        grid=(t // tm,),
        in_specs=[pl.BlockSpec((tm, d), lambda i: (i, 0)),
                  _const_spec((1, d)), _const_spec(w.shape)],
        out_specs=[pl.BlockSpec((tm, n), lambda i: (i, 0)),
                   pl.BlockSpec((None, STRIDE, tm // STRIDE, n),
                                lambda i: (i // per_seq, 0, i % per_seq, 0))],
        out_shape=[jax.ShapeDtypeStruct((t, n), BF16),
                   jax.ShapeDtypeStruct((t // seq, STRIDE, seq // STRIDE, n), BF16)],
        scratch_shapes=[pltpu.VMEM((COL_CHUNK // LANES, tm, LANES), F32)] * 2,
        compiler_params=_params(1),
        name="qkv_proj",
    )(x2, g, w)


def _conv_body(*refs, n_cast, per_seq):
    (x_ref, g_ref, w_ref, k_ref), cast_src, (z_ref,), cast_dst, (cu_ref,) = _split_refs(
        refs, 4, 1, n_cast)
    _run_casts(cast_src, cast_dst)
    tm, d = x_ref.shape
    pad = F32_SUBLANES
    first = pl.program_id(0) % per_seq == 0

    @pl.when(first)
    def _():
        cu_ref[0:pad, :] = jnp.zeros((pad, d), F32)

    @pl.when(jnp.logical_not(first))
    def _():
        cu_ref[0:pad, :] = cu_ref[tm:tm + pad, :]

    h = _rms(x_ref[...], g_ref[...]).astype(BF16)
    for j in range(0, d, COL_CHUNK):
        cols = slice(j, j + COL_CHUNK)
        c = jnp.dot(h, w_ref[:, d + j:d + j + COL_CHUNK], preferred_element_type=F32)
        u = jnp.dot(h, w_ref[:, 2 * d + j:2 * d + j + COL_CHUNK], preferred_element_type=F32)
        cu_ref[pad:pad + tm, cols] = c * u
        y = None
        for w in reversed(range(CONV_WIDTH)):
            start = pad - (CONV_WIDTH - 1 - w)
            term = k_ref[w:w + 1, cols] * cu_ref[start:start + tm, cols]
            y = term if y is None else y + term
        b = jnp.dot(h, w_ref[:, cols], preferred_element_type=F32)
        z_ref[:, cols] = (b * y).astype(BF16)


def _conv_pre(x2, g, w_in, taps, layer, seq, cast_jobs):
    t, d = x2.shape
    tm = ROW_TILE
    steps = t // tm
    cast_in, cast_out, cast_shapes = _cast_io(cast_jobs, steps)
    row = pl.BlockSpec((tm, d), lambda i: (i, 0))
    outs = pl.pallas_call(
        functools.partial(_conv_body, n_cast=len(cast_jobs), per_seq=seq // tm),
        grid=(steps,),
        in_specs=[row, _const_spec((1, d)), _const_spec(w_in.shape), _layer_spec(taps, layer)]
        + cast_in,
        out_specs=[row] + cast_out,
        out_shape=[jax.ShapeDtypeStruct((t, d), BF16)] + cast_shapes,
        scratch_shapes=[pltpu.VMEM((tm + F32_SUBLANES, d), F32)],
        compiler_params=_params(1),
        name="conv_pre",
    )(x2, g, w_in, taps, *[w for w, _ in cast_jobs])
    return outs[0], outs[1:]


def _ffn_body(*refs, final, n_cast):
    ins, cast_src, (o_ref,), cast_dst, (a_ref,) = _split_refs(refs, 8, 1, n_cast)
    x_ref, z_ref, wo_ref, g_ref, wg_ref, wu_ref, wd_ref, gf_ref = ins
    _run_casts(cast_src, cast_dst)
    x1 = x_ref[...] + jnp.dot(z_ref[...], wo_ref[...], preferred_element_type=F32)
    h = _rms(x1, g_ref[...]).astype(BF16)
    o_ref[...] = x1
    d_ff, group = wg_ref.shape[1], a_ref.shape[2]
    for lo in range(0, d_ff, group):
        hi, slot = min(lo + group, d_ff), (lo // group) % 2
        for c in range(lo, hi, FF_CHUNK):
            gate = jnp.dot(h, wg_ref[:, c:c + FF_CHUNK], preferred_element_type=F32)
            up = jnp.dot(h, wu_ref[:, c:c + FF_CHUNK], preferred_element_type=F32)
            a_ref[slot, :, c - lo:c - lo + FF_CHUNK] = (
                gate * jax.nn.sigmoid(gate) * up).astype(BF16)
        o_ref[...] += jnp.dot(a_ref[slot, :, :hi - lo], wd_ref[lo:hi, :],
                              preferred_element_type=F32)
    if final:
        o_ref[...] = _rms(o_ref[...], gf_ref[...])


def _mix_ffn(x2, z2, wo, g, wg, wu, wd, gf, final, cast_jobs):
    t, d = x2.shape
    steps = t // ROW_TILE
    cast_in, cast_out, cast_shapes = _cast_io(cast_jobs, steps)
    row = pl.BlockSpec((ROW_TILE, d), lambda i: (i, 0))
    outs = pl.pallas_call(
        functools.partial(_ffn_body, final=final, n_cast=len(cast_jobs)),
        grid=(steps,),
        in_specs=[row, row, _const_spec(wo.shape), _const_spec((1, d)), _const_spec(wg.shape),
                  _const_spec(wu.shape), _const_spec(wd.shape), _const_spec((1, d))] + cast_in,
        out_specs=[row] + cast_out,
        out_shape=[jax.ShapeDtypeStruct((t, d), F32)] + cast_shapes,
        scratch_shapes=[pltpu.VMEM((2, ROW_TILE, FF_GROUP), BF16)],
        compiler_params=_params(1),
        name="mix_ffn_final" if final else "mix_ffn",
    )(x2, z2, wo, g, wg, wu, wd, gf, *[w for w, _ in cast_jobs])
    return outs[0], outs[1:]


def _t5_bucket(dist):
    exact = NUM_BUCKETS // 2
    df = jnp.maximum(dist, 1).astype(F32)
    large = exact + (jnp.log(df / exact) / math.log(MAX_DISTANCE / exact)
                     * (NUM_BUCKETS - exact)).astype(jnp.int32)
    large = jnp.minimum(large, NUM_BUCKETS - 1)
    return jnp.where(dist < exact, dist, large)


def _bias_rows(rel_bias, n_dist, stride, branches, period):
    dist = jnp.arange(n_dist, dtype=jnp.int32) * stride
    mult = jnp.zeros((n_dist,), F32)
    for window, dilation in branches:
        mult = mult + ((dist % dilation == 0) & (dist <= window)).astype(F32)
    table = rel_bias[_t5_bucket(dist)].astype(F32)
    total = jnp.where(mult[:, None] > 0,
                      (table + jnp.log(jnp.maximum(mult, 1.0))[:, None]) * LOG2E, NEG_INF)
    pad = jnp.full((period - n_dist, total.shape[1]), NEG_INF, F32)
    return jnp.concatenate([total, pad], axis=0).T[:, None, :]


def _attn_body(tv1_ref, tv2_ref, q_ref, k_ref, v_ref, qs_ref, ks_ref, vs_ref, o_ref,
               bias1_ref, bias10_ref, bias2_ref, lhs_ref, lhss_ref, o2_ref, lse2_ref):
    tq = ATT_TILE
    seq, width = k_ref.shape
    sub = ks_ref.shape[1]
    n_sub = sub // tq
    groups = width // LANES
    heads = width // HEAD_DIM

    @pl.when(pl.program_id(1) == 0)
    def _():
        for h in range(heads):
            row = jnp.broadcast_to(tv1_ref[h], (2 * NEAR, tv1_ref.shape[2]))
            toep = pltpu.roll(row, 0, 1, stride=1, stride_axis=0)
            bias10_ref[:, h * NEAR:(h + 1) * NEAR] = toep[:NEAR, :NEAR]
            bias1_ref[:, h * NEAR:(h + 1) * NEAR] = toep[:, NEAR:2 * NEAR]
            g, hg = divmod(h, PAIR)
            cols = slice(hg * tq, (hg + 1) * tq)
            row = jnp.broadcast_to(tv2_ref[h], (tq, tv2_ref.shape[2]))
            toep = pltpu.roll(row, 0, 1, stride=1, stride_axis=0)
            for dist in range(n_sub):
                start = (n_sub - 1 - dist) * tq
                bias2_ref[g, start:start + tq, cols] = toep[:, dist * tq:(dist + 1) * tq]

    for kb in range(seq // tq):
        lhs_ref[:width, kb * tq:(kb + 1) * tq] = v_ref[kb * tq:(kb + 1) * tq, :].T
    lhs_ref[width:, :] = jnp.ones((ONES_ROWS, seq), BF16)
    for g in range(groups):
        lanes = slice(g * LANES, (g + 1) * LANES)
        for r in range(STRIDE):
            for kb in range(n_sub):
                lhss_ref[g, r, :LANES, kb * tq:(kb + 1) * tq] = (
                    vs_ref[r, kb * tq:(kb + 1) * tq, lanes].T)
            lhss_ref[g, r, LANES:, :] = jnp.ones((ONES_ROWS, sub), BF16)

    lane_head = lax.broadcasted_iota(jnp.int32, (1, width), 1) // HEAD_DIM
    items = [(g, r, j, 0, (j + 1) * tq)
             for g in range(groups) for r in range(STRIDE) for j in range(n_sub)]
    items += [(None, None, qb, max(0, (qb - 1) * NEAR), (qb + 1) * NEAR)
              for qb in range(seq // NEAR)]
    col_max, scores, probs = {}, {}, {}
    half = tq // 2
    late_cols = (half, tq + half)

    def per_head_rows(q):
        zero = jnp.zeros_like(q)
        lanes = lane_head[:, :q.shape[1]]
        return jnp.concatenate(
            [jnp.where(lanes == h, q, zero) for h in range(q.shape[1] // HEAD_DIM)], axis=0)

    def stage_a(t):
        g, r, qb, lo, hi = items[t]
        if g is None:
            q = per_head_rows(q_ref[qb * NEAR:(qb + 1) * NEAR, :])
            k = k_ref[lo:hi, :]
            bias = bias10_ref[...] if lo == qb * NEAR else bias1_ref[...]
        else:
            lanes = slice(g * LANES, (g + 1) * LANES)
            q = per_head_rows(qs_ref[r, qb * tq:(qb + 1) * tq, lanes])
            k = ks_ref[r, lo:hi, lanes]
            bias = bias2_ref[g, sub - hi:, :]
        s = lax.dot_general(k, q, NT_DIMS, preferred_element_type=F32) + bias
        if g is None:
            s = s.astype(BF16)
            scores[t] = (s, None)
            col_max[t] = jnp.max(s, axis=0, keepdims=True)
        else:
            old = s[:hi - lo - half].astype(BF16)
            new = [s[hi - lo - half:, c:c + half].astype(BF16) for c in late_cols]
            scores[t] = (old, new)
            m_old = jnp.max(old, axis=0, keepdims=True)
            m_new = [jnp.max(x, axis=0, keepdims=True) for x in new]
            col_max[t] = jnp.concatenate(
                [m_old[:, :half], jnp.maximum(m_old[:, half:tq], m_new[0]),
                 m_old[:, tq:tq + half], jnp.maximum(m_old[:, tq + half:], m_new[1])], axis=1)

    def stage_b(t):
        old, new = scores.pop(t)
        m = col_max[t]
        p = jnp.exp2(old - m)
        if new is not None:
            zero = jnp.zeros((half, half), BF16)
            p_new = [jnp.exp2(x - m[:, c:c + half]) for x, c in zip(new, late_cols)]
            p = jnp.concatenate(
                [p, jnp.concatenate([zero, p_new[0], zero, p_new[1]], axis=1)], axis=0)
        probs[t] = p

    def stage_c(t):
        g, r, qb, lo, hi = items[t]
        p, m = probs.pop(t), col_max.pop(t).astype(F32)
        lhs = lhs_ref[:, lo:hi] if g is None else lhss_ref[g, r, :, lo:hi]
        acc = jnp.dot(lhs, p, preferred_element_type=F32)
        n_heads = (lhs.shape[0] - ONES_ROWS) // HEAD_DIM
        nq = acc.shape[1] // n_heads
        den = acc[n_heads * HEAD_DIM:n_heads * HEAD_DIM + 1]
        lse = m + jnp.log2(den)
        outs, lses = [], []
        for h in range(n_heads):
            cols = slice(h * nq, (h + 1) * nq)
            outs.append(acc[h * HEAD_DIM:(h + 1) * HEAD_DIM, cols] / den[:, cols])
            lses.append(jnp.broadcast_to(lse[:, cols], (HEAD_DIM, nq)))
        out_t, lse_t = jnp.concatenate(outs, axis=0), jnp.concatenate(lses, axis=0)
        out, lse = out_t.T, lse_t.T
        if g is not None:
            token_rows = pl.ds(qb * tq * STRIDE + r, tq, stride=STRIDE)
            o2_ref[g, token_rows, :] = out
            lse2_ref[g, token_rows, :] = lse
        else:
            rows = slice(qb * NEAR, (qb + 1) * NEAR)
            out2 = jnp.concatenate([o2_ref[i, rows, :] for i in range(groups)], axis=1)
            lse2 = jnp.concatenate([lse2_ref[i, rows, :] for i in range(groups)], axis=1)
            share = 1.0 / (1.0 + jnp.exp2(lse2 - lse))
            o_ref[rows, :] = (out2 + share * (out - out2)).astype(BF16)

    for t in range(len(items) + 2):
        if t < len(items):
            stage_a(t)
        if 1 <= t <= len(items):
            stage_b(t - 1)
        if t >= 2:
            stage_c(t - 2)


def _attention(qkv, qkv_s, tv1, tv2):
    bsz, s, d3 = qkv.shape
    d = d3 // 3
    width = HEAD_GROUPS * LANES
    heads = width // HEAD_DIM
    n_blocks = d // width
    sub = s // STRIDE
    col = lambda off: pl.BlockSpec((None, s, width), lambda hb, b: (b, 0, off + hb))
    col_s = lambda off: pl.BlockSpec((None, STRIDE, sub, width),
                                     lambda hb, b: (b, 0, 0, off + hb))
    bias_row = lambda tv: pl.BlockSpec((heads, 1, tv.shape[2]), lambda hb, b: (hb, 0, 0))
    wide = 2 * ATT_TILE
    assert heads * NEAR == wide
    return pl.pallas_call(
        _attn_body,
        grid=(n_blocks, bsz),
        in_specs=[bias_row(tv1), bias_row(tv2),
                  col(0), col(n_blocks), col(2 * n_blocks),
                  col_s(0), col_s(n_blocks), col_s(2 * n_blocks)],
        out_specs=col(0),
        out_shape=jax.ShapeDtypeStruct((bsz, s, d), BF16),
        scratch_shapes=[pltpu.VMEM((2 * NEAR, wide), F32),
                        pltpu.VMEM((NEAR, wide), F32),
                        pltpu.VMEM((HEAD_GROUPS, sub, wide), F32),
                        pltpu.VMEM((width + ONES_ROWS, s), BF16),
                        pltpu.VMEM((HEAD_GROUPS, STRIDE, LANES + ONES_ROWS, sub), BF16),
                        pltpu.VMEM((HEAD_GROUPS, s, LANES), F32),
                        pltpu.VMEM((HEAD_GROUPS, s, LANES), F32)],
        compiler_params=_params(2),
        name="dilated_attention",
    )(tv1, tv2, qkv, qkv, qkv, qkv_s, qkv_s, qkv_s)


def kernel(x, mix_norm, ffn_norm, final_norm, conv_w_in, conv_kernel, conv_w_out,
           attn_w_qkv, attn_w_out, rel_bias, ffn_w_gate, ffn_w_up, ffn_w_down):
    bsz, s, d = x.shape
    depth = mix_norm.shape[0]
    assert d == N_HEADS * HEAD_DIM and s % ROW_TILE == 0 and ROW_TILE % ATT_TILE == 0
    near = tuple((w, dil) for w, dil in BRANCHES if dil == 1)
    far = tuple((w, dil) for w, dil in BRANCHES if dil > 1)
    assert all(w <= NEAR for w, _ in near) and all(dil % STRIDE == 0 for _, dil in far)
    assert s % (STRIDE * ATT_TILE) == 0 and ROW_TILE % STRIDE == 0

    tv1 = _bias_rows(rel_bias, NEAR + 1, 1, near, 3 * ATT_TILE)
    tv2 = _bias_rows(rel_bias, s // STRIDE, STRIDE, far, s // STRIDE + ATT_TILE)
    gf = final_norm.reshape(1, d)

    def layer_params(i):
        mix_in, mix_out = (conv_w_in, conv_w_out) if i % 2 == 0 else (attn_w_qkv, attn_w_out)
        return [(mix_in, i // 2), (mix_out, i // 2),
                (ffn_w_gate, i), (ffn_w_up, i), (ffn_w_down, i)]

    mix_in = conv_w_in[0].astype(BF16)
    rest = None
    x2 = x.reshape(bsz * s, d)
    for i in range(depth):
        g_mix = mix_norm[i].reshape(1, d)
        if i % 2 == 0:
            jobs = layer_params(i)[1:] if i == 0 else []
            z, made = _conv_pre(x2, g_mix, mix_in, conv_kernel, i // 2, s, jobs)
            rest = made if i == 0 else rest
        else:
            qkv, qkv_s = _qkv_proj(x2, g_mix, mix_in, s)
            z = _attention(qkv.reshape(bsz, s, 3 * d), qkv_s, tv1, tv2).reshape(bsz * s, d)
        mix_out, w_g, w_u, w_d = rest
        jobs = layer_params(i + 1) if i + 1 < depth else []
        x2, made = _mix_ffn(x2, z, mix_out, ffn_norm[i].reshape(1, d), w_g, w_u, w_d, gf,
                            final=(i == depth - 1), cast_jobs=jobs)
        if made:
            mix_in, rest = made[0], made[1:]
    return x2.reshape(bsz, s, d)
```
